```python
import math
import jax, jax.numpy as jnp
from jax import lax
import numpy as np

D_MODEL = 1024
BATCH = 16
SEQ = 256
DEPTH = 2
DEC_BATCH = 4
DEC_SEQ = 2048
PAST_LEN = 256

GRID_W = 64
N_BRANCH = 4
BRANCH_W = 512
POOL_WINDOWS = (2, 4, 8, 16)
POOL_GROUP = BRANCH_W // len(POOL_WINDOWS)
NAT_HEADS = 8
NAT_HD = 64
NAT_WIN_R = 8
NAT_WIN_C = 16
NAT_QBLK = 16
NAT_KBLK = 2 * NAT_WIN_C
CONV_W = 31
DIFF_HEADS = 4
DIFF_HD = 64
POOL_IN = BRANCH_W
NAT_IN = 3 * NAT_HEADS * NAT_HD
CONV_IN = 2 * BRANCH_W
DIFF_IN = 2 * (DIFF_HEADS * 2 * DIFF_HD) + DIFF_HEADS * 2 * DIFF_HD
IN_COLS = POOL_IN + NAT_IN + CONV_IN + DIFF_IN
PEER_HEADS = 8
PEER_NKEYS = 128
PEER_N = PEER_NKEYS * PEER_NKEYS
PEER_DK = 256
PEER_TOPK = 16
PEER_CHUNK = 128
QBLOCK = 128
ROPE_BASE = 10000.0
EPS = 1e-6
NEG_INF = -1e30

kernel_name = 'hybrid_diffusion_pool_nat_conv_diff_peer_step'

F32 = jnp.float32


def rms_norm(x, g):
    xf = x.astype(F32)
    y = xf * lax.rsqrt(jnp.mean(xf * xf, axis=-1, keepdims=True) + EPS)
    return (y * g.astype(F32)).astype(x.dtype)


def layer_norm(x, g, b):
    xf = x.astype(F32)
    mu = jnp.mean(xf, axis=-1, keepdims=True)
    var = jnp.mean(jnp.square(xf - mu), axis=-1, keepdims=True)
    y = (xf - mu) * lax.rsqrt(var + EPS) * g.astype(F32) + b.astype(F32)
    return y.astype(x.dtype)


def modulation(cvec, w_ada, b_ada):
    mod = jax.nn.silu(cvec) @ w_ada + b_ada
    return jnp.split(mod[:, None, :], 6, axis=-1)


def modulate(x, g, shift, scale):
    return rms_norm(x, g) * (1 + scale) + shift


def map_query_blocks(fn, q):
    B, L = q.shape[:2]
    nb = L // QBLOCK
    qb = jnp.moveaxis(q.reshape((B, nb, QBLOCK) + q.shape[2:]), 1, 0)
    out = lax.map(fn, qb)
    return jnp.moveaxis(out, 0, 1).reshape((B, L) + out.shape[3:])


def axial_rope(x):
    L, hd = x.shape[1], x.shape[-1]
    half = hd // 2
    quarter = half // 2
    t = jnp.arange(L)
    freqs = ROPE_BASE ** (-jnp.arange(quarter, dtype=F32) / quarter)

    def rot(xa, pos):
        ang = pos.astype(F32)[:, None] * freqs
        cos = jnp.cos(ang)[None, :, None, :]
        sin = jnp.sin(ang)[None, :, None, :]
        x1 = xa[..., :quarter].astype(F32)
        x2 = xa[..., quarter:].astype(F32)
        return jnp.concatenate([x1 * cos - x2 * sin, x1 * sin + x2 * cos], axis=-1)

    out = jnp.concatenate([rot(x[..., :half], t // GRID_W), rot(x[..., half:], t % GRID_W)], axis=-1)
    return out.astype(x.dtype)


def pool_branch(u, pool_w, pool_scale):
    B, L, C = u.shape
    uf = u.astype(F32)
    cs = jnp.concatenate([jnp.zeros((B, 1, C), F32), jnp.cumsum(uf, axis=1)], axis=1)
    t = jnp.arange(L)
    outs = []
    for gi, w in enumerate(POOL_WINDOWS):
        sl = slice(gi * POOL_GROUP, (gi + 1) * POOL_GROUP)
        lo = jnp.clip(t - w // 2, 0, L)
        hi = jnp.clip(t + (w - w // 2), 0, L)
        csg = cs[..., sl]
        mean = (csg[:, hi] - csg[:, lo]) / (hi - lo).astype(F32)[None, :, None]
        d = (mean - uf[..., sl]).astype(u.dtype)
        outs.append(jnp.einsum('blc,ce->ble', d, pool_w[gi]))
    return jnp.concatenate(outs, axis=-1) * pool_scale


def conv_branch(z, conv_w, conv_b, ln_g, ln_b):
    a, b = jnp.split(z, 2, axis=-1)
    u = a * jax.nn.sigmoid(b)
    y = lax.conv_general_dilated(u, conv_w[:, None, :].astype(u.dtype), window_strides=(1,),
                                 padding=[(CONV_W // 2, CONV_W // 2)],
                                 dimension_numbers=('NWC', 'WIO', 'NWC'),
                                 feature_group_count=BRANCH_W) + conv_b
    return jax.nn.silu(layer_norm(y, ln_g, ln_b))


def softmax_attend(q, k, v):
    scale = q.shape[-1] ** -0.5

    def blk(qb):
        s = jnp.einsum('bqhd,bkhd->bhqk', qb, k, preferred_element_type=F32) * scale
        p = jax.nn.softmax(s, axis=-1).astype(v.dtype)
        return jnp.einsum('bhqk,bkhd->bqhd', p, v)

    return map_query_blocks(blk, q)


def nat_latent(q, k, v, ck, cv, rel_bias):
    B, L, H, hd = q.shape
    rows = L // GRID_W
    kr = min(NAT_WIN_R, rows)
    nj = GRID_W // NAT_QBLK
    scale = hd ** -0.5
    r = jnp.arange(rows)
    row_idx = jnp.clip(r - kr // 2, 0, rows - kr)[:, None] + jnp.arange(kr)[None, :]
    j = jnp.arange(nj)
    col_idx = jnp.clip(j * NAT_QBLK - NAT_WIN_C // 2, 0, GRID_W - NAT_KBLK)[:, None] + jnp.arange(NAT_KBLK)[None, :]

    def gather(a):
        ag = a.reshape(B, rows, GRID_W, H, hd)
        blk = ag[:, row_idx[:, None, :, None], col_idx[None, :, None, :]]
        return blk.reshape(B, rows, nj, kr * NAT_KBLK, H, hd)

    kb = gather(k)
    vb = gather(v)
    qg = q.reshape(B, rows, nj, NAT_QBLK, H, hd)
    s_loc = jnp.einsum('brjqhd,brjkhd->bhrjqk', qg, kb, preferred_element_type=F32) * scale
    qcol = j[:, None] * NAT_QBLK + jnp.arange(NAT_QBLK)[None, :]
    wstart = jnp.clip(qcol - NAT_WIN_C // 2, 0, GRID_W - NAT_WIN_C)
    kcol_rel = col_idx[:, None, :] - wstart[:, :, None]
    valid = (kcol_rel >= 0) & (kcol_rel < NAT_WIN_C)
    dcol = col_idx[:, None, :] - qcol[:, :, None]
    drow = row_idx - r[:, None]
    bias = rel_bias[:, (drow + NAT_WIN_R - 1)[:, None, None, :, None],
                    jnp.clip(dcol + NAT_WIN_C - 1, 0, 2 * NAT_WIN_C - 2)[None, :, :, None, :]]
    nloc = kr * NAT_KBLK
    bias = bias.reshape(H, rows, nj, NAT_QBLK, nloc).astype(F32)
    mask = jnp.broadcast_to(valid[:, :, None, :], (nj, NAT_QBLK, kr, NAT_KBLK)).reshape(nj, NAT_QBLK, nloc)
    s_loc = jnp.where(mask[None, None, None], s_loc + bias[None], NEG_INF)
    s_ctx = jnp.einsum('brjqhd,bchd->bhrjqc', qg, ck, preferred_element_type=F32) * scale
    p = jax.nn.softmax(jnp.concatenate([s_loc, s_ctx], axis=-1), axis=-1).astype(v.dtype)
    out = (jnp.einsum('bhrjqk,brjkhd->brjqhd', p[..., :nloc], vb)
           + jnp.einsum('bhrjqc,bchd->brjqhd', p[..., nloc:], cv))
    return out.reshape(B, L, H * hd)


def diff_split(z):
    B, L, _ = z.shape
    nq = DIFF_HEADS * 2 * DIFF_HD
    q = z[..., :nq].reshape(B, L, DIFF_HEADS, 2, DIFF_HD)
    k = z[..., nq:2 * nq].reshape(B, L, DIFF_HEADS, 2, DIFF_HD)
    v = z[..., 2 * nq:].reshape(B, L, DIFF_HEADS, 2 * DIFF_HD)
    return q, k, v


def diff_lambda(lam_p, l):
    lam_init = 0.8 - 0.6 * math.exp(-0.3 * l)
    lf = lam_p.astype(F32)
    lam = jnp.exp(jnp.sum(lf[0] * lf[1])) - jnp.exp(jnp.sum(lf[2] * lf[3])) + lam_init
    return lam, lam_init


def diff_attend(q, ks, vs, lam):
    scale = q.shape[-1] ** -0.5

    def blk(qb):
        s = jnp.concatenate([jnp.einsum('bqhid,bkhid->bhiqk', qb, kk, preferred_element_type=F32) for kk in ks], axis=-1) * scale
        p = jax.nn.softmax(s, axis=-1)
        pd = (p[:, :, 0] - lam * p[:, :, 1]).astype(vs[0].dtype)
        out = 0
        off = 0
        for vv in vs:
            n = vv.shape[1]
            out = out + jnp.einsum('bhqk,bkhe->bqhe', pd[..., off:off + n], vv)
            off += n
        return out

    return map_query_blocks(blk, q)


def merge_branches(h, branches, w_branch, w_gate, b_gate, w_out):
    gates = jax.nn.sigmoid(h @ w_gate + b_gate)
    merged = 0
    for i, y in enumerate(branches):
        merged = merged + gates[..., i * D_MODEL:(i + 1) * D_MODEL] * (y @ w_branch[i])
    return merged @ w_out


def mixer(h, lp, l, ctx):
    B, L, _ = h.shape
    z = h @ lp['w_in']
    o1 = POOL_IN
    o2 = o1 + NAT_IN
    o3 = o2 + CONV_IN
    zp, zn, zc, zd = z[..., :o1], z[..., o1:o2], z[..., o2:o3], z[..., o3:]
    y_pool = pool_branch(zp, lp['pool_w'], lp['pool_scale'])
    qn, kn, vn = [a.reshape(B, L, NAT_HEADS, NAT_HD) for a in jnp.split(zn, 3, axis=-1)]
    qn = rms_norm(qn, lp['nat_q_g'])
    kn = rms_norm(kn, lp['nat_k_g'])
    y_conv = conv_branch(zc, lp['conv_w'], lp['conv_b'], lp['conv_ln_g'], lp['conv_ln_b'])
    qd, kd, vd = diff_split(zd)
    qd = rms_norm(qd, lp['diff_q_g'])
    kd = rms_norm(kd, lp['diff_k_g'])
    lam, lam_init = diff_lambda(lp['diff_lambda_p'], l)
    if ctx is None:
        y_nat = softmax_attend(qn, kn, vn).reshape(B, L, BRANCH_W)
        o_diff = diff_attend(qd, (kd,), (vd,), lam)
        new_ctx = (kn, vn, kd, vd)
    else:
        ck_n, cv_n, ck_d, cv_d = ctx
        y_nat = nat_latent(qn, kn, vn, ck_n, cv_n, lp['nat_rel_bias'])
        qd = axial_rope(qd.reshape(B, L, 2 * DIFF_HEADS, DIFF_HD)).reshape(qd.shape)
        kd = axial_rope(kd.reshape(B, L, 2 * DIFF_HEADS, DIFF_HD)).reshape(kd.shape)
        o_diff = diff_attend(qd, (kd, ck_d), (vd, cv_d), lam)
        new_ctx = None
    y_diff = (rms_norm(o_diff, lp['diff_subln_g']) * (1 - lam_init)).reshape(B, L, BRANCH_W)
    out = merge_branches(h, (y_pool, y_nat, y_conv, y_diff), lp['w_branch'], lp['w_gate'], lp['b_gate'], lp['w_out'])
    return out, new_ctx


def peer_ffn(h, w_query, sub_keys, u_tab, v_tab):
    B, L, D = h.shape
    nc = (B * L) // PEER_CHUNK

    def chunk(hc):
        qry = (hc @ w_query).reshape(-1, PEER_HEADS, 2, PEER_DK // 2)
        s = jnp.einsum('thpk,hpnk->thpn', qry, sub_keys, preferred_element_type=F32)
        s1, i1 = lax.top_k(s[:, :, 0], PEER_TOPK)
        s2, i2 = lax.top_k(s[:, :, 1], PEER_TOPK)
        cand = (s1[..., :, None] + s2[..., None, :]).reshape(-1, PEER_HEADS, PEER_TOPK * PEER_TOPK)
        sc, ci = lax.top_k(cand, PEER_TOPK)
        e = (jnp.take_along_axis(i1, ci // PEER_TOPK, axis=-1) * PEER_NKEYS
             + jnp.take_along_axis(i2, ci % PEER_TOPK, axis=-1))
        g = jax.nn.softmax(sc, axis=-1)
        act = jax.nn.gelu(jnp.einsum('thkd,td->thk', u_tab[e], hc, preferred_element_type=F32), approximate=False)
        return jnp.einsum('thk,thkd->td', (g * act).astype(hc.dtype), v_tab[e])

    y = lax.map(chunk, h.reshape(nc, PEER_CHUNK, D))
    return y.reshape(B, L, D)


def trunk_layer(x, mod, lp, l, ctx):
    sh1, sc1, g1, sh2, sc2, g2 = mod
    h = modulate(x, lp['norm1_g'], sh1, sc1)
    y, new_ctx = mixer(h, lp, l, ctx)
    x = x + g1 * y
    h = modulate(x, lp['norm2_g'], sh2, sc2)
    x = x + g2 * peer_ffn(h, lp['peer_w_query'], lp['peer_sub_keys'], lp['peer_u'], lp['peer_v'])
    return x, new_ctx


def setup_inputs(seed: int = 0) -> dict:
    key = jax.random.key(seed)
    ks = iter(jax.random.split(key, 48))
    D = D_MODEL

    def nrm(shape, s):
        return jax.random.normal(next(ks), shape, F32) * s

    def gain(shape):
        return 1.0 + nrm(shape, 0.02)

    return {
        'x_prompt': nrm((BATCH, SEQ, D), 1.0),
        'x_sample': nrm((DEC_BATCH, DEC_SEQ, D), 1.0),
        'cache_nat_k': nrm((DEC_BATCH, DEPTH, PAST_LEN, NAT_HEADS, NAT_HD), 1.0),
        'cache_nat_v': nrm((DEC_BATCH, DEPTH, PAST_LEN, NAT_HEADS, NAT_HD), 1.0),
        'cache_diff_k': nrm((DEC_BATCH, DEPTH, PAST_LEN, DIFF_HEADS, 2, DIFF_HD), 1.0),
        'cache_diff_v': nrm((DEC_BATCH, DEPTH, PAST_LEN, DIFF_HEADS, 2 * DIFF_HD), 1.0),
        'c': nrm((DEC_BATCH, D), 1.0),
        'c_ctx': nrm((D,), 1.0),
        'w_ada': nrm((DEPTH, D, 6 * D), 0.5 * D ** -0.5),
        'b_ada': nrm((DEPTH, 6 * D), 0.02),
        'norm1_g': gain((DEPTH, D)),
        'norm2_g': gain((DEPTH, D)),
        'w_in': nrm((DEPTH, D, IN_COLS), D ** -0.5),
        'pool_w': nrm((DEPTH, len(POOL_WINDOWS), POOL_GROUP, POOL_GROUP), POOL_GROUP ** -0.5),
        'pool_scale': gain((DEPTH, BRANCH_W)),
        'nat_q_g': gain((DEPTH, NAT_HD)),
        'nat_k_g': gain((DEPTH, NAT_HD)),
        'nat_rel_bias': nrm((DEPTH, NAT_HEADS, 2 * NAT_WIN_R - 1, 2 * NAT_WIN_C - 1), 0.1),
        'conv_w': nrm((DEPTH, CONV_W, BRANCH_W), CONV_W ** -0.5),
        'conv_b': nrm((DEPTH, BRANCH_W), 0.02),
        'conv_ln_g': gain((DEPTH, BRANCH_W)),
        'conv_ln_b': nrm((DEPTH, BRANCH_W), 0.02),
        'diff_q_g': gain((DEPTH, DIFF_HD)),
        'diff_k_g': gain((DEPTH, DIFF_HD)),
        'diff_lambda_p': nrm((DEPTH, 4, DIFF_HD), 0.1),
        'diff_subln_g': gain((DEPTH, 2 * DIFF_HD)),
        'w_branch': nrm((DEPTH, N_BRANCH, BRANCH_W, D), BRANCH_W ** -0.5),
        'w_gate': nrm((DEPTH, D, N_BRANCH * D), D ** -0.5),
        'b_gate': nrm((DEPTH, N_BRANCH * D), 0.02),
        'w_out': nrm((DEPTH, D, D), D ** -0.5),
        'peer_w_query': nrm((DEPTH, D, PEER_HEADS * PEER_DK), D ** -0.5),
        'peer_sub_keys': nrm((DEPTH, PEER_HEADS, 2, PEER_NKEYS, PEER_DK // 2), (PEER_DK // 2) ** -0.5),
        'peer_u': nrm((DEPTH, PEER_N, D), D ** -0.5),
        'peer_v': nrm((DEPTH, PEER_N, D), PEER_HEADS ** -0.5),
    }


def reference(x_prompt, x_sample, cache_nat_k, cache_nat_v, cache_diff_k, cache_diff_v, c, c_ctx,
              w_ada, b_ada, norm1_g, norm2_g, w_in, pool_w, pool_scale, nat_q_g, nat_k_g, nat_rel_bias,
              conv_w, conv_b, conv_ln_g, conv_ln_b, diff_q_g, diff_k_g, diff_lambda_p, diff_subln_g,
              w_branch, w_gate, b_gate, w_out, peer_w_query, peer_sub_keys, peer_u, peer_v):
    xp = x_prompt
    xs = x_sample
    nat_ks, nat_vs, diff_ks, diff_vs = [], [], [], []
    for l in range(DEPTH):
        lp = {
            'norm1_g': norm1_g[l], 'norm2_g': norm2_g[l], 'w_in': w_in[l],
            'pool_w': pool_w[l], 'pool_scale': pool_scale[l],
            'nat_q_g': nat_q_g[l], 'nat_k_g': nat_k_g[l], 'nat_rel_bias': nat_rel_bias[l],
            'conv_w': conv_w[l], 'conv_b': conv_b[l], 'conv_ln_g': conv_ln_g[l], 'conv_ln_b': conv_ln_b[l],
            'diff_q_g': diff_q_g[l], 'diff_k_g': diff_k_g[l], 'diff_lambda_p': diff_lambda_p[l],
            'diff_subln_g': diff_subln_g[l],
            'w_branch': w_branch[l], 'w_gate': w_gate[l], 'b_gate': b_gate[l], 'w_out': w_out[l],
            'peer_w_query': peer_w_query[l], 'peer_sub_keys': peer_sub_keys[l],
            'peer_u': peer_u[l], 'peer_v': peer_v[l],
        }
        mod_ctx = modulation(c_ctx[None, :], w_ada[l], b_ada[l])
        xp, (nk, nv, dk, dv) = trunk_layer(xp, mod_ctx, lp, l, None)
        nat_ks.append(nk)
        nat_vs.append(nv)
        diff_ks.append(dk)
        diff_vs.append(dv)
        mod_lat = modulation(c, w_ada[l], b_ada[l])
        ctx = (cache_nat_k[:, l], cache_nat_v[:, l], cache_diff_k[:, l], cache_diff_v[:, l])
        xs, _ = trunk_layer(xs, mod_lat, lp, l, ctx)
    state_nat_k = jnp.stack(nat_ks, axis=1)
    state_nat_v = jnp.stack(nat_vs, axis=1)
    state_diff_k = jnp.stack(diff_ks, axis=1)
    state_diff_v = jnp.stack(diff_vs, axis=1)
    return (xp, xs, state_nat_k, state_nat_v, state_diff_k, state_diff_v)
```

```python
import functools
import math

import jax
import jax.numpy as jnp
from jax import lax
from jax.experimental import pallas as pl
from jax.experimental.pallas import tpu as pltpu

F32 = jnp.float32
BF16 = jnp.bfloat16

D_MODEL = 1024
BATCH = 16
SEQ = 256
DEPTH = 2
DEC_BATCH = 4
DEC_SEQ = 2048
PAST_LEN = 256
GRID_W = 64
BRANCH_W = 512
POOL_WINDOWS = (2, 4, 8, 16)
POOL_GROUP = 128
NAT_HEADS = 8
NAT_HD = 64
NAT_WIN_R = 8
NAT_WIN_C = 16
CONV_W = 31
DIFF_HEADS = 4
DIFF_HD = 64
IN_COLS = 4608
GATE_COLS = 4096
PEER_HEADS = 8
PEER_NKEYS = 128
PEER_N = PEER_NKEYS * PEER_NKEYS
PEER_TOPK = 16
ROPE_BASE = 10000.0
EPS = 1e-6
NEG_INF = -1e30

N_CTX = BATCH * SEQ
N_LAT = DEC_BATCH * DEC_SEQ
N_TOK = N_CTX + N_LAT
GRID_ROWS = DEC_SEQ // GRID_W

CB = 512
N_GATE_CB = GATE_COLS // CB
Z_CB0 = N_GATE_CB
ZB_POOL, ZB_NQ, ZB_NK, ZB_NV, ZB_CA, ZB_CGATE, ZB_DQ, ZB_DK, ZB_DV = range(9)
N_CB = N_GATE_CB + 9

V7X_VMEM_LIMIT = 52 * 1024 * 1024

T_PRE = 1024
T_SEQ = 256
HALO = 16
T_MERGE = 512
T_SEL = 256
T_PEER = 512
E_CHUNK = 1024
LANE = 128


def _sigmoid(x):
    return 1.0 / (1.0 + jnp.exp(-x))


def _dot(a, b):
    return jnp.dot(a, b, preferred_element_type=F32)


def _dot_nt(a, b):
    return lax.dot_general(a, b, (((1,), (1,)), ((), ())), preferred_element_type=F32)


def _dot_tn(a, b):
    return lax.dot_general(a, b, (((0,), (0,)), ((), ())), preferred_element_type=F32)


def _split_bf16(a):
    hi = a.astype(BF16)
    lo = (a - hi.astype(F32)).astype(BF16)
    return hi, lo


def _params(*sem):
    return pltpu.CompilerParams(dimension_semantics=sem, vmem_limit_bytes=V7X_VMEM_LIMIT)


def _mod_row(i, tile):
    nct = N_CTX // tile
    per = DEC_SEQ // tile
    return jnp.where(i < nct, 0, 1 + (i - nct) // per)


def _mod_kernel(c_ref, w_ref, b_ref, o_ref):
    c = c_ref[...]
    a = c * _sigmoid(c)
    w = w_ref[0]
    a_hi, a_lo = _split_bf16(a)
    w_hi, w_lo = _split_bf16(w)
    o_ref[0] = _dot(a_hi, w_hi) + _dot(a_lo, w_hi) + _dot(a_hi, w_lo) + b_ref[0]


def _modulation(cvec, w_ada, b_ada):
    tn = 1024
    return pl.pallas_call(
        _mod_kernel,
        grid=(DEPTH, 6 * D_MODEL // tn),
        in_specs=[
            pl.BlockSpec((8, D_MODEL), lambda l, j: (0, 0)),
            pl.BlockSpec((1, D_MODEL, tn), lambda l, j: (l, 0, j)),
            pl.BlockSpec((1, 1, tn), lambda l, j: (l, 0, j)),
        ],
        out_specs=pl.BlockSpec((1, 8, tn), lambda l, j: (l, 0, j)),
        out_shape=jax.ShapeDtypeStruct((DEPTH, 8, 6 * D_MODEL), F32),
        compiler_params=_params("parallel", "parallel"),
        name="modulation",
    )(cvec, w_ada, b_ada.reshape(DEPTH, 1, 6 * D_MODEL))


def _tile4(t):
    return jnp.concatenate([t, t, t, t], axis=1)


def _pre_kernel(x_ref, mod_ref, g_ref, w_ref, b_ref, gg_ref, gmat_ref, rc_ref, rp_ref, rm_ref,
                o_ref, h_scr):
    j = pl.program_id(1)

    @pl.when(j == 0)
    def _():
        x = x_ref[...]
        ms = jnp.mean(x * x, axis=-1, keepdims=True)
        y = x * lax.rsqrt(ms + EPS) * g_ref[...]
        h = y * (1.0 + mod_ref[1:2, :]) + mod_ref[0:1, :]
        h_scr[...] = h.astype(BF16)

    acc = _dot(h_scr[...], w_ref[...])
    is_gate = j < N_GATE_CB
    k = j - Z_CB0
    is_norm = (k == ZB_NQ) | (k == ZB_NK) | (k == ZB_DQ) | (k == ZB_DK)

    @pl.when(is_gate)
    def _():
        o_ref[...] = _sigmoid(acc + b_ref[...])

    @pl.when(jnp.logical_not(is_gate) & jnp.logical_not(is_norm))
    def _():
        o_ref[...] = acc

    @pl.when(is_norm)
    def _():
        hi, lo = _split_bf16(acc * acc)
        ss = _dot(hi, gmat_ref[...]) + _dot(lo, gmat_ref[...])
        y = acc * lax.rsqrt(ss * (1.0 / NAT_HD) + EPS) * gg_ref[0]
        y = (y * _tile4(rc_ref[...])
             + pltpu.roll(y, 16, 1) * _tile4(rp_ref[...])
             + pltpu.roll(y, CB - 16, 1) * _tile4(rm_ref[...]))
        o_ref[...] = y


def _rope_tables(tile):
    quarter = DIFF_HD // 4
    t = jnp.arange(DEC_SEQ)
    lane = jnp.arange(LANE)
    d = lane % DIFF_HD
    freqs = ROPE_BASE ** (-jnp.arange(quarter, dtype=F32) / quarter)
    pos = jnp.where(d[None, :] < DIFF_HD // 2, (t // GRID_W)[:, None], (t % GRID_W)[:, None]).astype(F32)
    ang = pos * freqs[d % quarter][None, :]
    cos = jnp.cos(ang)
    sin = jnp.sin(ang)
    second = (d % (2 * quarter)) >= quarter
    s_plus = jnp.where(second[None, :], sin, 0.0)
    s_minus = jnp.where(second[None, :], 0.0, -sin)
    ones = jnp.ones((tile, LANE), F32)
    zeros = jnp.zeros((tile, LANE), F32)
    return (jnp.concatenate([cos, ones], 0), jnp.concatenate([s_plus, zeros], 0),
            jnp.concatenate([s_minus, zeros], 0))


def _pre(x, mod_l, norm_g, wcat, bcat, gg, gmat, rope):
    nct = N_CTX // T_PRE
    per = DEC_SEQ // T_PRE
    ident_blk = DEC_SEQ // T_PRE

    def rope_idx(i, j):
        k = j - Z_CB0
        use = ((k == ZB_DQ) | (k == ZB_DK)) & (i >= nct)
        return jnp.where(use, (i - nct) % per, ident_blk), 0

    rope_spec = pl.BlockSpec((T_PRE, LANE), rope_idx)
    return pl.pallas_call(
        _pre_kernel,
        grid=(N_TOK // T_PRE, N_CB),
        in_specs=[
            pl.BlockSpec((T_PRE, D_MODEL), lambda i, j: (i, 0)),
            pl.BlockSpec((None, 6, D_MODEL), lambda i, j: (_mod_row(i, T_PRE), 0, 0)),
            pl.BlockSpec((1, D_MODEL), lambda i, j: (0, 0)),
            pl.BlockSpec((D_MODEL, CB), lambda i, j: (0, j)),
            pl.BlockSpec((1, CB), lambda i, j: (0, j)),
            pl.BlockSpec((1, 1, CB), lambda i, j: (j, 0, 0)),
            pl.BlockSpec((CB, CB), lambda i, j: (0, 0)),
            rope_spec, rope_spec, rope_spec,
        ],
        out_specs=pl.BlockSpec((T_PRE, CB), lambda i, j: (i, j)),
        out_shape=jax.ShapeDtypeStruct((N_TOK, N_CB * CB), F32),
        scratch_shapes=[pltpu.VMEM((T_PRE, D_MODEL), BF16)],
        compiler_params=_params("parallel", "arbitrary"),
        name="pre",
    )(x, mod_l, norm_g, wcat, bcat, gg, gmat, *rope)


def _seq_kernel(zp_ref, zp_prev, zp_next, za_ref, za_prev, za_next, zb_ref, zb_prev, zb_next,
                pw_ref, ps_ref, cw_ref, cb_ref, lg_ref, lb_ref,
                yp_ref, yc_ref, pbuf, ubuf, ybuf):
    i = pl.program_id(0)
    nct = N_CTX // T_SEQ
    per = DEC_SEQ // T_SEQ
    is_lat = i >= nct
    kk = (i - nct) % per
    has_prev = is_lat & (kk != 0)
    has_next = is_lat & (kk != per - 1)
    pos0 = jnp.where(is_lat, kk * T_SEQ, 0)
    seq_len = jnp.where(is_lat, DEC_SEQ, SEQ)
    fp = jnp.where(has_prev, 1.0, 0.0).astype(F32)
    fn = jnp.where(has_next, 1.0, 0.0).astype(F32)

    pbuf[0:HALO, :] = zp_prev[...] * fp
    pbuf[HALO:HALO + T_SEQ, :] = zp_ref[...]
    pbuf[HALO + T_SEQ:, :] = zp_next[...] * fn
    ubuf[0:HALO, :] = za_prev[...] * _sigmoid(zb_prev[...]) * fp
    ubuf[HALO:HALO + T_SEQ, :] = za_ref[...] * _sigmoid(zb_ref[...])
    ubuf[HALO + T_SEQ:, :] = za_next[...] * _sigmoid(zb_next[...]) * fn

    pos = pos0 + lax.broadcasted_iota(jnp.int32, (T_SEQ, LANE), 0)
    for gi, w in enumerate(POOL_WINDOWS):
        ls = slice(gi * LANE, (gi + 1) * LANE)
        acc = None
        for s in range(-(w // 2), w - w // 2):
            v = pbuf[HALO + s:HALO + s + T_SEQ, ls]
            acc = v if acc is None else acc + v
        lo = jnp.maximum(pos - w // 2, 0)
        hi = jnp.minimum(pos + (w - w // 2), seq_len)
        mean = acc / (hi - lo).astype(F32)
        d = mean - pbuf[HALO:HALO + T_SEQ, ls]
        y = _dot(d.astype(BF16), pw_ref[gi].astype(BF16)) * ps_ref[:, ls]
        yp_ref[:, ls] = y.astype(yp_ref.dtype)

    base = HALO - CONV_W // 2
    for c in range(BRANCH_W // LANE):
        ls = slice(c * LANE, (c + 1) * LANE)
        acc = None
        for k in range(CONV_W):
            v = ubuf[base + k:base + k + T_SEQ, ls] * cw_ref[k:k + 1, ls]
            acc = v if acc is None else acc + v
        ybuf[:, ls] = acc + cb_ref[:, ls]
    y = ybuf[...]
    mu = jnp.mean(y, axis=-1, keepdims=True)
    yc = y - mu
    var = jnp.mean(yc * yc, axis=-1, keepdims=True)
    yn = yc * lax.rsqrt(var + EPS) * lg_ref[...] + lb_ref[...]
    yc_ref[...] = (yn * _sigmoid(yn)).astype(yc_ref.dtype)


def _seq_mixers(zg, pool_w, pool_scale, conv_w, conv_b, ln_g, ln_b):
    nt = N_TOK // T_SEQ
    hb = T_SEQ // HALO
    n_hblk = N_TOK // HALO

    def specs(cb):
        return [
            pl.BlockSpec((T_SEQ, CB), lambda i: (i, cb)),
            pl.BlockSpec((HALO, CB), lambda i: (jnp.maximum(i * hb - 1, 0), cb)),
            pl.BlockSpec((HALO, CB), lambda i: (jnp.minimum((i + 1) * hb, n_hblk - 1), cb)),
        ]

    def full(shape):
        return pl.BlockSpec(shape, lambda i: (0,) * len(shape))

    cw = jnp.concatenate([conv_w, jnp.zeros((1, BRANCH_W), F32)], axis=0)
    out_spec = pl.BlockSpec((T_SEQ, BRANCH_W), lambda i: (i, 0))
    return pl.pallas_call(
        _seq_kernel,
        grid=(nt,),
        in_specs=(specs(Z_CB0 + ZB_POOL) + specs(Z_CB0 + ZB_CA) + specs(Z_CB0 + ZB_CGATE)
                  + [full((4, POOL_GROUP, POOL_GROUP)), full((1, BRANCH_W)), full((CONV_W + 1, BRANCH_W)),
                     full((1, BRANCH_W)), full((1, BRANCH_W)), full((1, BRANCH_W))]),
        out_specs=[out_spec, out_spec],
        out_shape=[jax.ShapeDtypeStruct((N_TOK, BRANCH_W), BF16)] * 2,
        scratch_shapes=[pltpu.VMEM((T_SEQ + 2 * HALO, BRANCH_W), F32),
                        pltpu.VMEM((T_SEQ + 2 * HALO, BRANCH_W), F32),
                        pltpu.VMEM((T_SEQ, BRANCH_W), F32)],
        compiler_params=_params("parallel"),
        name="pool_conv",
    )(zg, zg, zg, zg, zg, zg, zg, zg, zg, pool_w, pool_scale.reshape(1, -1), cw,
      conv_b.reshape(1, -1), ln_g.reshape(1, -1), ln_b.reshape(1, -1))


def _nat_ctx_kernel(q_ref, k_ref, v_ref, o_ref):
    for h in range(NAT_HEADS):
        sl = slice(h * NAT_HD, (h + 1) * NAT_HD)
        q = (q_ref[:, sl] * NAT_HD ** -0.5).astype(BF16)
        s = _dot_nt(q, k_ref[:, sl].astype(BF16))
        m = jnp.max(s, axis=-1, keepdims=True)
        e = jnp.exp(s - m)
        den = jnp.sum(e, axis=-1, keepdims=True)
        o = _dot(e.astype(BF16), v_ref[:, sl].astype(BF16)) / den
        o_ref[:, sl] = o.astype(o_ref.dtype)


def _nat_ctx(zg):
    def spec(cb):
        return pl.BlockSpec((SEQ, CB), lambda b: (b, cb))

    return pl.pallas_call(
        _nat_ctx_kernel,
        grid=(BATCH,),
        in_specs=[spec(Z_CB0 + ZB_NQ), spec(Z_CB0 + ZB_NK), spec(Z_CB0 + ZB_NV)],
        out_specs=pl.BlockSpec((SEQ, BRANCH_W), lambda b: (b, 0)),
        out_shape=jax.ShapeDtypeStruct((N_CTX, BRANCH_W), BF16),
        compiler_params=_params("parallel"),
        name="nat_ctx",
    )(zg, zg, zg)


def _nat_lat_kernel(q_ref, k_ref, v_ref, ck_ref, cv_ref, bias_ref, o_ref):
    r = pl.program_id(1)
    r0 = jnp.clip(r - NAT_WIN_R // 2, 0, GRID_ROWS - NAT_WIN_R)
    start = pl.multiple_of(r0 * GRID_W, GRID_W)
    nwin = NAT_WIN_R * GRID_W
    for h in range(NAT_HEADS):
        sl = slice(h * NAT_HD, (h + 1) * NAT_HD)
        q = (q_ref[:, sl] * NAT_HD ** -0.5).astype(BF16)
        kw = k_ref[pl.ds(start, nwin), sl].astype(BF16)
        vw = v_ref[pl.ds(start, nwin), sl].astype(BF16)
        s_loc = _dot_nt(q, kw) + bias_ref[h]
        s_ctx = _dot_nt(q, ck_ref[:, sl].astype(BF16))
        m = jnp.maximum(jnp.max(s_loc, axis=-1, keepdims=True), jnp.max(s_ctx, axis=-1, keepdims=True))
        e_loc = jnp.exp(s_loc - m)
        e_ctx = jnp.exp(s_ctx - m)
        den = jnp.sum(e_loc, axis=-1, keepdims=True) + jnp.sum(e_ctx, axis=-1, keepdims=True)
        o = (_dot(e_loc.astype(BF16), vw) + _dot(e_ctx.astype(BF16), cv_ref[:, sl].astype(BF16))) / den
        o_ref[:, sl] = o.astype(o_ref.dtype)


def _nat_bias_table(rel_bias):
    var = jnp.arange(NAT_WIN_R)
    i = jnp.arange(NAT_WIN_R)
    q = jnp.arange(GRID_W)
    kc = jnp.arange(GRID_W)
    drow_idx = i[None, :] - var[:, None] + NAT_WIN_R - 1
    dcol_idx = jnp.clip(kc[None, :] - q[:, None] + NAT_WIN_C - 1, 0, 2 * NAT_WIN_C - 2)
    wstart = jnp.clip(q - NAT_WIN_C // 2, 0, GRID_W - NAT_WIN_C)
    valid = (kc[None, :] >= wstart[:, None]) & (kc[None, :] < wstart[:, None] + NAT_WIN_C)
    t = rel_bias[:, drow_idx[:, :, None, None], dcol_idx[None, None, :, :]]
    t = jnp.where(valid[None, None, None], t.astype(F32), NEG_INF)
    return t.transpose(1, 0, 3, 2, 4).reshape(NAT_WIN_R, NAT_HEADS, GRID_W, NAT_WIN_R * GRID_W)


def _nat_lat(zg, cache_k, cache_v, bias_tab, layer):
    q_blk0 = N_CTX // GRID_W
    kv_blk0 = N_CTX // DEC_SEQ

    def variant(r):
        return r - jnp.clip(r - NAT_WIN_R // 2, 0, GRID_ROWS - NAT_WIN_R)

    cache_spec = pl.BlockSpec((None, None, PAST_LEN, BRANCH_W), lambda b, r: (b, layer, 0, 0))
    return pl.pallas_call(
        _nat_lat_kernel,
        grid=(DEC_BATCH, GRID_ROWS),
        in_specs=[
            pl.BlockSpec((GRID_W, CB), lambda b, r: (q_blk0 + b * GRID_ROWS + r, Z_CB0 + ZB_NQ)),
            pl.BlockSpec((DEC_SEQ, CB), lambda b, r: (kv_blk0 + b, Z_CB0 + ZB_NK)),
            pl.BlockSpec((DEC_SEQ, CB), lambda b, r: (kv_blk0 + b, Z_CB0 + ZB_NV)),
            cache_spec, cache_spec,
            pl.BlockSpec((None, NAT_HEADS, GRID_W, NAT_WIN_R * GRID_W), lambda b, r: (variant(r), 0, 0, 0)),
        ],
        out_specs=pl.BlockSpec((GRID_W, BRANCH_W), lambda b, r: (b * GRID_ROWS + r, 0)),
        out_shape=jax.ShapeDtypeStruct((N_LAT, BRANCH_W), BF16),
        compiler_params=_params("parallel", "arbitrary"),
        name="nat_lat",
    )(zg, zg, zg, cache_k, cache_v, bias_tab)


def _diff_kernel(has_cache, lam_init, *refs):
    if has_cache:
        q_ref, k_ref, v_ref, ck_ref, cv_ref, lamp_ref, g_ref, o_ref = refs
    else:
        q_ref, k_ref, v_ref, lamp_ref, g_ref, o_ref = refs
    lp = lamp_ref[...]
    lam = (jnp.exp(jnp.sum(lp[0:1] * lp[1:2], axis=-1, keepdims=True))
           - jnp.exp(jnp.sum(lp[2:3] * lp[3:4], axis=-1, keepdims=True)) + lam_init)
    hv = 2 * DIFF_HD
    for h in range(DIFF_HEADS):
        parts = []
        for i in range(2):
            sl = slice(h * hv + i * DIFF_HD, h * hv + (i + 1) * DIFF_HD)
            q = (q_ref[:, sl] * DIFF_HD ** -0.5).astype(BF16)
            s = _dot_nt(q, k_ref[:, sl].astype(BF16))
            m = jnp.max(s, axis=-1, keepdims=True)
            if has_cache:
                sc = _dot_nt(q, ck_ref[:, sl].astype(BF16))
                m = jnp.maximum(m, jnp.max(sc, axis=-1, keepdims=True))
                ec = jnp.exp(sc - m)
            e = jnp.exp(s - m)
            den = jnp.sum(e, axis=-1, keepdims=True)
            if has_cache:
                den = den + jnp.sum(ec, axis=-1, keepdims=True)
            parts.append((e, ec if has_cache else None, 1.0 / den))
        (e1, ec1, r1), (e2, ec2, r2) = parts
        vs = slice(h * hv, (h + 1) * hv)
        pd = e1 * r1 - e2 * (lam * r2)
        o = _dot(pd.astype(BF16), v_ref[:, vs].astype(BF16))
        if has_cache:
            pdc = ec1 * r1 - ec2 * (lam * r2)
            o = o + _dot(pdc.astype(BF16), cv_ref[:, vs].astype(BF16))
        ms = jnp.mean(o * o, axis=-1, keepdims=True)
        y = o * lax.rsqrt(ms + EPS) * g_ref[...] * (1.0 - lam_init)
        o_ref[:, vs] = y.astype(o_ref.dtype)


def _lam_init(layer):
    return 0.8 - 0.6 * math.exp(-0.3 * layer)


def _diff_ctx(zg, lam_p, subln_g, layer):
    def spec(cb):
        return pl.BlockSpec((SEQ, CB), lambda b: (b, cb))

    return pl.pallas_call(
        functools.partial(_diff_kernel, False, _lam_init(layer)),
        grid=(BATCH,),
        in_specs=[spec(Z_CB0 + ZB_DQ), spec(Z_CB0 + ZB_DK), spec(Z_CB0 + ZB_DV),
                  pl.BlockSpec((4, DIFF_HD), lambda b: (0, 0)),
                  pl.BlockSpec((1, 2 * DIFF_HD), lambda b: (0, 0))],
        out_specs=pl.BlockSpec((SEQ, BRANCH_W), lambda b: (b, 0)),
        out_shape=jax.ShapeDtypeStruct((N_CTX, BRANCH_W), BF16),
        compiler_params=_params("parallel"),
        name="diff_ctx",
    )(zg, zg, zg, lam_p, subln_g.reshape(1, -1))


T_DQ = 256


def _diff_lat(zg, cache_k, cache_v, lam_p, subln_g, layer):
    nq = DEC_SEQ // T_DQ
    q_blk0 = N_CTX // T_DQ
    kv_blk0 = N_CTX // DEC_SEQ
    cache_spec = pl.BlockSpec((None, None, PAST_LEN, BRANCH_W), lambda b, t: (b, layer, 0, 0))
    return pl.pallas_call(
        functools.partial(_diff_kernel, True, _lam_init(layer)),
        grid=(DEC_BATCH, nq),
        in_specs=[
            pl.BlockSpec((T_DQ, CB), lambda b, t: (q_blk0 + b * nq + t, Z_CB0 + ZB_DQ)),
            pl.BlockSpec((DEC_SEQ, CB), lambda b, t: (kv_blk0 + b, Z_CB0 + ZB_DK)),
            pl.BlockSpec((DEC_SEQ, CB), lambda b, t: (kv_blk0 + b, Z_CB0 + ZB_DV)),
            cache_spec, cache_spec,
            pl.BlockSpec((4, DIFF_HD), lambda b, t: (0, 0)),
            pl.BlockSpec((1, 2 * DIFF_HD), lambda b, t: (0, 0)),
        ],
        out_specs=pl.BlockSpec((T_DQ, BRANCH_W), lambda b, t: (b * nq + t, 0)),
        out_shape=jax.ShapeDtypeStruct((N_LAT, BRANCH_W), BF16),
        compiler_params=_params("parallel", "arbitrary"),
        name="diff_lat",
    )(zg, zg, zg, cache_k, cache_v, lam_p, subln_g.reshape(1, -1))


def _merge_kernel(yp_ref, yn_ref, yc_ref, yd_ref, gate_ref, x_ref, mod_ref, g2_ref, wb_ref, wo_ref, wq_ref,
                  xo_ref, h2_ref, q_ref):
    merged = None
    for br, y_ref in enumerate((yp_ref, yn_ref, yc_ref, yd_ref)):
        t = gate_ref[:, br * D_MODEL:(br + 1) * D_MODEL] * _dot(y_ref[...], wb_ref[br])
        merged = t if merged is None else merged + t
    out = _dot(merged.astype(BF16), wo_ref[...])
    x = x_ref[...] + mod_ref[2:3, :] * out
    xo_ref[...] = x
    ms = jnp.mean(x * x, axis=-1, keepdims=True)
    h = x * lax.rsqrt(ms + EPS) * g2_ref[...] * (1.0 + mod_ref[4:5, :]) + mod_ref[3:4, :]
    hb = h.astype(BF16)
    h2_ref[...] = hb
    q_ref[...] = _dot(hb, wq_ref[...])


def _merge(y_pool, y_nat, y_conv, y_diff, zg, x, mod_l, norm2_g, wb, wo, wq):
    yspec = pl.BlockSpec((T_MERGE, BRANCH_W), lambda i: (i, 0))
    qcols = wq.shape[1]
    return pl.pallas_call(
        _merge_kernel,
        grid=(N_TOK // T_MERGE,),
        in_specs=[yspec, yspec, yspec, yspec,
                  pl.BlockSpec((T_MERGE, GATE_COLS), lambda i: (i, 0)),
                  pl.BlockSpec((T_MERGE, D_MODEL), lambda i: (i, 0)),
                  pl.BlockSpec((None, 6, D_MODEL), lambda i: (_mod_row(i, T_MERGE), 0, 0)),
                  pl.BlockSpec((1, D_MODEL), lambda i: (0, 0)),
                  pl.BlockSpec((4, BRANCH_W, D_MODEL), lambda i: (0, 0, 0)),
                  pl.BlockSpec((D_MODEL, D_MODEL), lambda i: (0, 0)),
                  pl.BlockSpec((D_MODEL, qcols), lambda i: (0, 0))],
        out_specs=[pl.BlockSpec((T_MERGE, D_MODEL), lambda i: (i, 0)),
                   pl.BlockSpec((T_MERGE, D_MODEL), lambda i: (i, 0)),
                   pl.BlockSpec((T_MERGE, qcols), lambda i: (i, 0))],
        out_shape=[jax.ShapeDtypeStruct((N_TOK, D_MODEL), F32),
                   jax.ShapeDtypeStruct((N_TOK, D_MODEL), BF16),
                   jax.ShapeDtypeStruct((N_TOK, qcols), F32)],
        compiler_params=_params("parallel"),
        name="merge",
    )(y_pool, y_nat, y_conv, y_diff, zg, x, mod_l, norm2_g, wb, wo, wq)


def _top_rows(s, n):
    rid = lax.broadcasted_iota(jnp.int32, (n, s.shape[1]), 0)
    out = jnp.full((n, s.shape[1]), -jnp.inf, F32)
    for k in range(n):
        m = jnp.max(s, axis=0, keepdims=True)
        out = jnp.where(rid == k, m, out)
        s = jnp.where(s == m, -jnp.inf, s)
    return out


def _peer_select_kernel(q_ref, sk_ref, s1_ref, s2_ref, e2_ref, coef_ref, tau_ref):
    q = q_ref[...].astype(BF16)
    half = PEER_NKEYS
    s1_all = _dot_nt(sk_ref[0].astype(BF16), q[:, :half])
    s2_all = _dot_nt(sk_ref[1].astype(BF16), q[:, half:])
    s1_ref[...] = s1_all
    s2_ref[...] = s2_all
    for c in range(T_SEL // LANE):
        ls = slice(c * LANE, (c + 1) * LANE)
        s1 = s1_all[:, ls]
        s2 = s2_all[:, ls]
        v1 = _top_rows(s1, PEER_TOPK)
        v2 = _top_rows(s2, PEER_TOPK)
        cands = [v1[0:1] + v2]
        cands += [v1[a:a + 1] + v2[0:8] for a in range(1, 8)]
        cands += [v1[8:16] + v2[0:1]]
        top = _top_rows(jnp.concatenate(cands, axis=0), PEER_TOPK)
        tau = top[PEER_TOPK - 1:PEER_TOPK]
        zsum = jnp.sum(jnp.exp(top - top[0:1]), axis=0, keepdims=True)
        e2_ref[:, ls] = jnp.exp(s2 - v2[0:1])
        coef_ref[:, ls] = jnp.exp(s1 - v1[0:1]) / zsum
        tau_ref[:, ls] = tau


def _peer_select(qry, sub_keys):
    nt = N_TOK // T_SEL
    kspec = pl.BlockSpec((None, PEER_NKEYS, T_SEL), lambda i, h: (h, 0, i))
    kshape = jax.ShapeDtypeStruct((PEER_HEADS, PEER_NKEYS, N_TOK), F32)
    return pl.pallas_call(
        _peer_select_kernel,
        grid=(nt, PEER_HEADS),
        in_specs=[pl.BlockSpec((T_SEL, 2 * PEER_NKEYS), lambda i, h: (i, h)),
                  pl.BlockSpec((None, 2, PEER_NKEYS, PEER_NKEYS), lambda i, h: (h, 0, 0, 0))],
        out_specs=[kspec, kspec, kspec, kspec,
                   pl.BlockSpec((None, 1, T_SEL), lambda i, h: (h, 0, i))],
        out_shape=[kshape, kshape, kshape, kshape,
                   jax.ShapeDtypeStruct((PEER_HEADS, 1, N_TOK), F32)],
        compiler_params=_params("parallel", "arbitrary"),
        name="peer_select",
    )(qry, sub_keys)


def _gelu(x):
    return 0.5 * x * (1.0 + lax.erf(x * (2.0 ** -0.5)))


def _peer_dense_kernel(h_ref, u_ref, v_ref, s1_ref, coef_ref, s2_ref, e2_ref, tau_ref, x_ref, mod_ref,
                       o_ref, acc_ref, a_scr, p_scr):
    c = pl.program_id(1)

    @pl.when(c == 0)
    def _():
        acc_ref[...] = jnp.zeros_like(acc_ref)

    hb = h_ref[...]

    def one_key(n1l, carry):
        row0 = pl.multiple_of(n1l * PEER_NKEYS, PEER_NKEYS)
        a_scr[...] = _dot_nt(u_ref[pl.ds(row0, PEER_NKEYS), :].astype(BF16), hb)
        for tc in range(T_PEER // LANE):
            ls = slice(tc * LANE, (tc + 1) * LANE)
            g = jnp.zeros((PEER_NKEYS, LANE), F32)
            s1_rows = s1_ref[n1l, :, ls]
            coef_rows = coef_ref[n1l, :, ls]
            for h in range(PEER_HEADS):
                srow = s1_rows[h:h + 1]
                crow = coef_rows[h:h + 1]
                sel = (s2_ref[h, :, ls] + srow) >= tau_ref[h, :, ls]
                g = g + jnp.where(sel, e2_ref[h, :, ls], 0.0) * crow
            p = g * _gelu(a_scr[:, ls])
            p_scr[pl.ds(row0, PEER_NKEYS), ls] = p.astype(BF16)
        return carry

    lax.fori_loop(0, E_CHUNK // PEER_NKEYS, one_key, 0)
    acc_ref[...] += _dot_tn(p_scr[...], v_ref[...].astype(BF16))

    @pl.when(c == pl.num_programs(1) - 1)
    def _():
        o_ref[...] = x_ref[...] + mod_ref[5:6, :] * acc_ref[...]


def _peer_dense(h2, peer_u, peer_v, s1, coef, s2, e2, tau, x, mod_l):
    nt = N_TOK // T_PEER
    nc = PEER_N // E_CHUNK
    n1c = E_CHUNK // PEER_NKEYS
    rowspec = pl.BlockSpec((n1c, PEER_HEADS, T_PEER), lambda i, c: (c, 0, i))
    fullspec = pl.BlockSpec((PEER_HEADS, PEER_NKEYS, T_PEER), lambda i, c: (0, 0, i))
    return pl.pallas_call(
        _peer_dense_kernel,
        grid=(nt, nc),
        in_specs=[pl.BlockSpec((T_PEER, D_MODEL), lambda i, c: (i, 0)),
                  pl.BlockSpec((E_CHUNK, D_MODEL), lambda i, c: (c, 0)),
                  pl.BlockSpec((E_CHUNK, D_MODEL), lambda i, c: (c, 0)),
                  rowspec, rowspec, fullspec, fullspec,
                  pl.BlockSpec((PEER_HEADS, 1, T_PEER), lambda i, c: (0, 0, i)),
                  pl.BlockSpec((T_PEER, D_MODEL), lambda i, c: (i, 0)),
                  pl.BlockSpec((None, 6, D_MODEL), lambda i, c: (_mod_row(i, T_PEER), 0, 0))],
        out_specs=pl.BlockSpec((T_PEER, D_MODEL), lambda i, c: (i, 0)),
        out_shape=jax.ShapeDtypeStruct((N_TOK, D_MODEL), F32),
        scratch_shapes=[pltpu.VMEM((T_PEER, D_MODEL), F32),
                        pltpu.VMEM((PEER_NKEYS, T_PEER), F32),
                        pltpu.VMEM((E_CHUNK, T_PEER), BF16)],
        compiler_params=_params("parallel", "arbitrary"),
        name="peer_dense",
    )(h2, peer_u, peer_v, s1, coef, s2, e2, tau, x, mod_l)


def kernel(x_prompt, x_sample, cache_nat_k, cache_nat_v, cache_diff_k, cache_diff_v, c, c_ctx, w_ada, b_ada, norm1_g, norm2_g, w_in, pool_w, pool_scale, nat_q_g, nat_k_g, nat_rel_bias, conv_w, conv_b, conv_ln_g, conv_ln_b, diff_q_g, diff_k_g, diff_lambda_p, diff_subln_g, w_branch, w_gate, b_gate, w_out, peer_w_query, peer_sub_keys, peer_u, peer_v):
    x = jnp.concatenate([x_prompt.reshape(N_CTX, D_MODEL), x_sample.reshape(N_LAT, D_MODEL)], axis=0)
    cvec = jnp.concatenate([c_ctx[None, :], c, jnp.zeros((8 - 1 - DEC_BATCH, D_MODEL), F32)], axis=0)
    mod = _modulation(cvec, w_ada, b_ada).reshape(DEPTH, 8, 6, D_MODEL)

    gid = jnp.arange(CB) // NAT_HD
    gmat = (gid[:, None] == gid[None, :]).astype(BF16)
    rope = _rope_tables(T_PRE)
    ck_n = cache_nat_k.reshape(DEC_BATCH, DEPTH, PAST_LEN, BRANCH_W)
    cv_n = cache_nat_v.reshape(DEC_BATCH, DEPTH, PAST_LEN, BRANCH_W)
    ck_d = cache_diff_k.reshape(DEC_BATCH, DEPTH, PAST_LEN, BRANCH_W)
    cv_d = cache_diff_v.reshape(DEC_BATCH, DEPTH, PAST_LEN, BRANCH_W)

    states = []
    for l in range(DEPTH):
        wcat = jnp.concatenate([w_gate[l], w_in[l]], axis=1).astype(BF16)
        bcat = jnp.concatenate([b_gate[l], jnp.zeros((IN_COLS,), F32)]).reshape(1, -1)
        ones = jnp.ones((CB,), F32)
        gg_rows = [ones] * N_CB
        gg_rows[Z_CB0 + ZB_NQ] = jnp.tile(nat_q_g[l], NAT_HEADS)
        gg_rows[Z_CB0 + ZB_NK] = jnp.tile(nat_k_g[l], NAT_HEADS)
        gg_rows[Z_CB0 + ZB_DQ] = jnp.tile(diff_q_g[l], 2 * DIFF_HEADS)
        gg_rows[Z_CB0 + ZB_DK] = jnp.tile(diff_k_g[l], 2 * DIFF_HEADS)
        gg = jnp.stack(gg_rows).reshape(N_CB, 1, CB)

        zg = _pre(x, mod[l], norm1_g[l].reshape(1, -1), wcat, bcat, gg, gmat, rope)

        y_pool, y_conv = _seq_mixers(zg, pool_w[l], pool_scale[l], conv_w[l], conv_b[l],
                                     conv_ln_g[l], conv_ln_b[l])
        y_nat = jnp.concatenate(
            [_nat_ctx(zg), _nat_lat(zg, ck_n, cv_n, _nat_bias_table(nat_rel_bias[l]), l)], axis=0)
        y_diff = jnp.concatenate(
            [_diff_ctx(zg, diff_lambda_p[l], diff_subln_g[l], l),
             _diff_lat(zg, ck_d, cv_d, diff_lambda_p[l], diff_subln_g[l], l)], axis=0)

        x, h2, qry = _merge(y_pool, y_nat, y_conv, y_diff, zg, x, mod[l], norm2_g[l].reshape(1, -1),
                            w_branch[l].astype(BF16), w_out[l].astype(BF16), peer_w_query[l].astype(BF16))
        s1, s2, e2, coef, tau = _peer_select(qry, peer_sub_keys[l])
        x = _peer_dense(h2, peer_u[l], peer_v[l], s1.transpose(1, 0, 2), coef.transpose(1, 0, 2),
                        s2, e2, tau, x, mod[l])

        zc = zg[:N_CTX]
        states.append([zc[:, (Z_CB0 + b) * CB:(Z_CB0 + b + 1) * CB] for b in (ZB_NK, ZB_NV, ZB_DK, ZB_DV)])

    def stack(idx, shape):
        return jnp.stack([states[l][idx].reshape((BATCH, SEQ) + shape) for l in range(DEPTH)], axis=1)

    return (x[:N_CTX].reshape(BATCH, SEQ, D_MODEL),
            x[N_CTX:].reshape(DEC_BATCH, DEC_SEQ, D_MODEL),
            stack(0, (NAT_HEADS, NAT_HD)),
            stack(1, (NAT_HEADS, NAT_HD)),
            stack(2, (DIFF_HEADS, 2, DIFF_HD)),
            stack(3, (DIFF_HEADS, 2 * DIFF_HD)))
```

```python
import functools
import math

import jax
import jax.numpy as jnp
from jax import lax
from jax.experimental import pallas as pl
from jax.experimental.pallas import tpu as pltpu

F32 = jnp.float32
BF16 = jnp.bfloat16

D_MODEL = 1024
BATCH = 16
SEQ = 256
DEPTH = 2
DEC_BATCH = 4
DEC_SEQ = 2048
PAST_LEN = 256
GRID_W = 64
BRANCH_W = 512
POOL_WINDOWS = (2, 4, 8, 16)
POOL_GROUP = 128
NAT_HEADS = 8
NAT_HD = 64
NAT_WIN_R = 8
NAT_WIN_C = 16
CONV_W = 31
DIFF_HEADS = 4
DIFF_HD = 64
IN_COLS = 4608
GATE_COLS = 4096
PEER_HEADS = 8
PEER_NKEYS = 128
PEER_N = PEER_NKEYS * PEER_NKEYS
PEER_TOPK = 16
ROPE_BASE = 10000.0
EPS = 1e-6
NEG_INF = -1e30

N_CTX = BATCH * SEQ
N_LAT = DEC_BATCH * DEC_SEQ
N_TOK = N_CTX + N_LAT
GRID_ROWS = DEC_SEQ // GRID_W

CB = 512
N_GATE_CB = GATE_COLS // CB
Z_CB0 = N_GATE_CB
ZB_POOL, ZB_NQ, ZB_NK, ZB_NV, ZB_CA, ZB_CGATE, ZB_DQ, ZB_DK, ZB_DV = range(9)
N_CB = N_GATE_CB + 9

V7X_VMEM_LIMIT = 52 * 1024 * 1024

T_PRE = 1024
T_SEQ = 256
HALO = 16
T_MERGE = 512
T_SEL = 256
T_PEER = 512
E_CHUNK = 1024
LANE = 128


def _sigmoid(x):
    return 1.0 / (1.0 + jnp.exp(-x))


def _dot(a, b):
    return jnp.dot(a, b, preferred_element_type=F32)


def _dot_nt(a, b):
    return lax.dot_general(a, b, (((1,), (1,)), ((), ())), preferred_element_type=F32)


def _dot_tn(a, b):
    return lax.dot_general(a, b, (((0,), (0,)), ((), ())), preferred_element_type=F32)


def _split_bf16(a):
    hi = a.astype(BF16)
    lo = (a - hi.astype(F32)).astype(BF16)
    return hi, lo


def _params(*sem):
    return pltpu.CompilerParams(dimension_semantics=sem, vmem_limit_bytes=V7X_VMEM_LIMIT)


def _mod_row(i, tile):
    nct = N_CTX // tile
    per = DEC_SEQ // tile
    return jnp.where(i < nct, 0, 1 + (i - nct) // per)


def _mod_kernel(c_ref, w_ref, b_ref, o_ref):
    c = c_ref[...]
    a = c * _sigmoid(c)
    w = w_ref[0]
    a_hi, a_lo = _split_bf16(a)
    w_hi, w_lo = _split_bf16(w)
    o_ref[0] = _dot(a_hi, w_hi) + _dot(a_lo, w_hi) + _dot(a_hi, w_lo) + b_ref[0]


def _modulation(cvec, w_ada, b_ada):
    tn = 1024
    return pl.pallas_call(
        _mod_kernel,
        grid=(DEPTH, 6 * D_MODEL // tn),
        in_specs=[
            pl.BlockSpec((8, D_MODEL), lambda l, j: (0, 0)),
            pl.BlockSpec((1, D_MODEL, tn), lambda l, j: (l, 0, j)),
            pl.BlockSpec((1, 1, tn), lambda l, j: (l, 0, j)),
        ],
        out_specs=pl.BlockSpec((1, 8, tn), lambda l, j: (l, 0, j)),
        out_shape=jax.ShapeDtypeStruct((DEPTH, 8, 6 * D_MODEL), F32),
        compiler_params=_params("parallel", "parallel"),
        name="modulation",
    )(cvec, w_ada, b_ada.reshape(DEPTH, 1, 6 * D_MODEL))


def _tile4(t):
    return jnp.concatenate([t, t, t, t], axis=1)


def _pre_kernel(x_ref, mod_ref, g_ref, w_ref, b_ref, gg_ref, gmat_ref, rc_ref, rp_ref, rm_ref,
                o_ref, h_scr):
    j = pl.program_id(1)

    @pl.when(j == 0)
    def _():
        x = x_ref[...]
        ms = jnp.mean(x * x, axis=-1, keepdims=True)
        y = x * lax.rsqrt(ms + EPS) * g_ref[...]
        h = y * (1.0 + mod_ref[1:2, :]) + mod_ref[0:1, :]
        h_scr[...] = h.astype(BF16)

    acc = _dot(h_scr[...], w_ref[...])
    is_gate = j < N_GATE_CB
    k = j - Z_CB0
    is_norm = (k == ZB_NQ) | (k == ZB_NK) | (k == ZB_DQ) | (k == ZB_DK)

    @pl.when(is_gate)
    def _():
        o_ref[...] = _sigmoid(acc + b_ref[...])

    @pl.when(jnp.logical_not(is_gate) & jnp.logical_not(is_norm))
    def _():
        o_ref[...] = acc

    @pl.when(is_norm)
    def _():
        hi, lo = _split_bf16(acc * acc)
        ss = _dot(hi, gmat_ref[...]) + _dot(lo, gmat_ref[...])
        y = acc * lax.rsqrt(ss * (1.0 / NAT_HD) + EPS) * gg_ref[0]
        y = (y * _tile4(rc_ref[...])
             + pltpu.roll(y, 16, 1) * _tile4(rp_ref[...])
             + pltpu.roll(y, CB - 16, 1) * _tile4(rm_ref[...]))
        o_ref[...] = y


def _rope_tables(tile):
    quarter = DIFF_HD // 4
    t = jnp.arange(DEC_SEQ)
    lane = jnp.arange(LANE)
    d = lane % DIFF_HD
    freqs = ROPE_BASE ** (-jnp.arange(quarter, dtype=F32) / quarter)
    pos = jnp.where(d[None, :] < DIFF_HD // 2, (t // GRID_W)[:, None], (t % GRID_W)[:, None]).astype(F32)
    ang = pos * freqs[d % quarter][None, :]
    cos = jnp.cos(ang)
    sin = jnp.sin(ang)
    second = (d % (2 * quarter)) >= quarter
    s_plus = jnp.where(second[None, :], sin, 0.0)
    s_minus = jnp.where(second[None, :], 0.0, -sin)
    ones = jnp.ones((tile, LANE), F32)
    zeros = jnp.zeros((tile, LANE), F32)
    return (jnp.concatenate([cos, ones], 0), jnp.concatenate([s_plus, zeros], 0),
            jnp.concatenate([s_minus, zeros], 0))


def _pre(x, mod_l, norm_g, wcat, bcat, gg, gmat, rope):
    nct = N_CTX // T_PRE
    per = DEC_SEQ // T_PRE
    ident_blk = DEC_SEQ // T_PRE

    def rope_idx(i, j):
        k = j - Z_CB0
        use = ((k == ZB_DQ) | (k == ZB_DK)) & (i >= nct)
        return jnp.where(use, (i - nct) % per, ident_blk), 0

    rope_spec = pl.BlockSpec((T_PRE, LANE), rope_idx)
    return pl.pallas_call(
        _pre_kernel,
        grid=(N_TOK // T_PRE, N_CB),
        in_specs=[
            pl.BlockSpec((T_PRE, D_MODEL), lambda i, j: (i, 0)),
            pl.BlockSpec((None, 6, D_MODEL), lambda i, j: (_mod_row(i, T_PRE), 0, 0)),
            pl.BlockSpec((1, D_MODEL), lambda i, j: (0, 0)),
            pl.BlockSpec((D_MODEL, CB), lambda i, j: (0, j)),
            pl.BlockSpec((1, CB), lambda i, j: (0, j)),
            pl.BlockSpec((1, 1, CB), lambda i, j: (j, 0, 0)),
            pl.BlockSpec((CB, CB), lambda i, j: (0, 0)),
            rope_spec, rope_spec, rope_spec,
        ],
        out_specs=pl.BlockSpec((T_PRE, CB), lambda i, j: (i, j)),
        out_shape=jax.ShapeDtypeStruct((N_TOK, N_CB * CB), F32),
        scratch_shapes=[pltpu.VMEM((T_PRE, D_MODEL), BF16)],
        compiler_params=_params("parallel", "arbitrary"),
        name="pre",
    )(x, mod_l, norm_g, wcat, bcat, gg, gmat, *rope)


def _seq_kernel(zp_ref, zp_prev, zp_next, za_ref, za_prev, za_next, zb_ref, zb_prev, zb_next,
                pw_ref, ps_ref, cw_ref, cb_ref, lg_ref, lb_ref,
                yp_ref, yc_ref, pbuf, ubuf, ybuf):
    i = pl.program_id(0)
    nct = N_CTX // T_SEQ
    per = DEC_SEQ // T_SEQ
    is_lat = i >= nct
    kk = (i - nct) % per
    has_prev = is_lat & (kk != 0)
    has_next = is_lat & (kk != per - 1)
    pos0 = jnp.where(is_lat, kk * T_SEQ, 0)
    seq_len = jnp.where(is_lat, DEC_SEQ, SEQ)
    fp = jnp.where(has_prev, 1.0, 0.0).astype(F32)
    fn = jnp.where(has_next, 1.0, 0.0).astype(F32)

    pbuf[0:HALO, :] = zp_prev[...] * fp
    pbuf[HALO:HALO + T_SEQ, :] = zp_ref[...]
    pbuf[HALO + T_SEQ:, :] = zp_next[...] * fn
    ubuf[0:HALO, :] = za_prev[...] * _sigmoid(zb_prev[...]) * fp
    ubuf[HALO:HALO + T_SEQ, :] = za_ref[...] * _sigmoid(zb_ref[...])
    ubuf[HALO + T_SEQ:, :] = za_next[...] * _sigmoid(zb_next[...]) * fn

    pos = pos0 + lax.broadcasted_iota(jnp.int32, (T_SEQ, LANE), 0)
    for gi, w in enumerate(POOL_WINDOWS):
        ls = slice(gi * LANE, (gi + 1) * LANE)
        acc = None
        for s in range(-(w // 2), w - w // 2):
            v = pbuf[HALO + s:HALO + s + T_SEQ, ls]
            acc = v if acc is None else acc + v
        lo = jnp.maximum(pos - w // 2, 0)
        hi = jnp.minimum(pos + (w - w // 2), seq_len)
        mean = acc / (hi - lo).astype(F32)
        d = mean - pbuf[HALO:HALO + T_SEQ, ls]
        y = _dot(d.astype(BF16), pw_ref[gi].astype(BF16)) * ps_ref[:, ls]
        yp_ref[:, ls] = y.astype(yp_ref.dtype)

    base = HALO - CONV_W // 2
    for c in range(BRANCH_W // LANE):
        ls = slice(c * LANE, (c + 1) * LANE)
        acc = None
        for k in range(CONV_W):
            v = ubuf[base + k:base + k + T_SEQ, ls] * cw_ref[k:k + 1, ls]
            acc = v if acc is None else acc + v
        ybuf[:, ls] = acc + cb_ref[:, ls]
    y = ybuf[...]
    mu = jnp.mean(y, axis=-1, keepdims=True)
    yc = y - mu
    var = jnp.mean(yc * yc, axis=-1, keepdims=True)
    yn = yc * lax.rsqrt(var + EPS) * lg_ref[...] + lb_ref[...]
    yc_ref[...] = (yn * _sigmoid(yn)).astype(yc_ref.dtype)


def _seq_mixers(zg, pool_w, pool_scale, conv_w, conv_b, ln_g, ln_b):
    nt = N_TOK // T_SEQ
    hb = T_SEQ // HALO
    n_hblk = N_TOK // HALO

    def specs(cb):
        return [
            pl.BlockSpec((T_SEQ, CB), lambda i: (i, cb)),
            pl.BlockSpec((HALO, CB), lambda i: (jnp.maximum(i * hb - 1, 0), cb)),
            pl.BlockSpec((HALO, CB), lambda i: (jnp.minimum((i + 1) * hb, n_hblk - 1), cb)),
        ]

    def full(shape):
        return pl.BlockSpec(shape, lambda i: (0,) * len(shape))

    cw = jnp.concatenate([conv_w, jnp.zeros((1, BRANCH_W), F32)], axis=0)
    out_spec = pl.BlockSpec((T_SEQ, BRANCH_W), lambda i: (i, 0))
    return pl.pallas_call(
        _seq_kernel,
        grid=(nt,),
        in_specs=(specs(Z_CB0 + ZB_POOL) + specs(Z_CB0 + ZB_CA) + specs(Z_CB0 + ZB_CGATE)
                  + [full((4, POOL_GROUP, POOL_GROUP)), full((1, BRANCH_W)), full((CONV_W + 1, BRANCH_W)),
                     full((1, BRANCH_W)), full((1, BRANCH_W)), full((1, BRANCH_W))]),
        out_specs=[out_spec, out_spec],
        out_shape=[jax.ShapeDtypeStruct((N_TOK, BRANCH_W), BF16)] * 2,
        scratch_shapes=[pltpu.VMEM((T_SEQ + 2 * HALO, BRANCH_W), F32),
                        pltpu.VMEM((T_SEQ + 2 * HALO, BRANCH_W), F32),
                        pltpu.VMEM((T_SEQ, BRANCH_W), F32)],
        compiler_params=_params("parallel"),
        name="pool_conv",
    )(zg, zg, zg, zg, zg, zg, zg, zg, zg, pool_w, pool_scale.reshape(1, -1), cw,
      conv_b.reshape(1, -1), ln_g.reshape(1, -1), ln_b.reshape(1, -1))


def _nat_ctx_kernel(q_ref, k_ref, v_ref, o_ref):
    for h in range(NAT_HEADS):
        sl = slice(h * NAT_HD, (h + 1) * NAT_HD)
        q = (q_ref[:, sl] * NAT_HD ** -0.5).astype(BF16)
        s = _dot_nt(q, k_ref[:, sl].astype(BF16))
        m = jnp.max(s, axis=-1, keepdims=True)
        e = jnp.exp(s - m)
        den = jnp.sum(e, axis=-1, keepdims=True)
        o = _dot(e.astype(BF16), v_ref[:, sl].astype(BF16)) / den
        o_ref[:, sl] = o.astype(o_ref.dtype)


def _nat_ctx(zg):
    def spec(cb):
        return pl.BlockSpec((SEQ, CB), lambda b: (b, cb))

    return pl.pallas_call(
        _nat_ctx_kernel,
        grid=(BATCH,),
        in_specs=[spec(Z_CB0 + ZB_NQ), spec(Z_CB0 + ZB_NK), spec(Z_CB0 + ZB_NV)],
        out_specs=pl.BlockSpec((SEQ, BRANCH_W), lambda b: (b, 0)),
        out_shape=jax.ShapeDtypeStruct((N_CTX, BRANCH_W), BF16),
        compiler_params=_params("parallel"),
        name="nat_ctx",
    )(zg, zg, zg)


def _nat_lat_kernel(q_ref, k_ref, v_ref, ck_ref, cv_ref, bias_ref, o_ref):
    r = pl.program_id(1)
    r0 = jnp.clip(r - NAT_WIN_R // 2, 0, GRID_ROWS - NAT_WIN_R)
    start = pl.multiple_of(r0 * GRID_W, GRID_W)
    nwin = NAT_WIN_R * GRID_W
    for h in range(NAT_HEADS):
        sl = slice(h * NAT_HD, (h + 1) * NAT_HD)
        q = (q_ref[:, sl] * NAT_HD ** -0.5).astype(BF16)
        kw = k_ref[pl.ds(start, nwin), sl].astype(BF16)
        vw = v_ref[pl.ds(start, nwin), sl].astype(BF16)
        s_loc = _dot_nt(q, kw) + bias_ref[h]
        s_ctx = _dot_nt(q, ck_ref[:, sl].astype(BF16))
        m = jnp.maximum(jnp.max(s_loc, axis=-1, keepdims=True), jnp.max(s_ctx, axis=-1, keepdims=True))
        e_loc = jnp.exp(s_loc - m)
        e_ctx = jnp.exp(s_ctx - m)
        den = jnp.sum(e_loc, axis=-1, keepdims=True) + jnp.sum(e_ctx, axis=-1, keepdims=True)
        o = (_dot(e_loc.astype(BF16), vw) + _dot(e_ctx.astype(BF16), cv_ref[:, sl].astype(BF16))) / den
        o_ref[:, sl] = o.astype(o_ref.dtype)


def _nat_bias_kernel(rb_ref, oh_ref, o_ref):
    x = rb_ref[...]
    hi = x.astype(BF16)
    r1 = x - hi.astype(F32)
    mid = r1.astype(BF16)
    lo = (r1 - mid.astype(F32)).astype(BF16)
    oh = oh_ref[...]
    o_ref[...] = _dot(hi, oh) + _dot(mid, oh) + _dot(lo, oh)


def _nat_bias_table(rel_bias):
    ndr = 2 * NAT_WIN_R - 1
    ndc = 2 * NAT_WIN_C - 1
    q = jnp.arange(GRID_W)
    kc = jnp.arange(GRID_W)
    dcol = jnp.clip(kc[None, :] - q[:, None] + NAT_WIN_C - 1, 0, ndc - 1)
    wstart = jnp.clip(q - NAT_WIN_C // 2, 0, GRID_W - NAT_WIN_C)
    valid = (kc[None, :] >= wstart[:, None]) & (kc[None, :] < wstart[:, None] + NAT_WIN_C)
    d = jnp.arange(LANE)
    onehot = jnp.where(d[:, None, None] == ndc, jnp.logical_not(valid)[None],
                       (d[:, None, None] == dcol[None]) & valid[None])
    onehot = onehot.reshape(LANE, GRID_W * GRID_W).astype(BF16)
    nrow = NAT_HEADS * ndr
    rb = jnp.concatenate([rel_bias.reshape(nrow, ndc).astype(F32), jnp.full((nrow, 1), NEG_INF, F32),
                          jnp.zeros((nrow, LANE - ndc - 1), F32)], axis=1)
    rb = jnp.concatenate([rb, jnp.zeros((LANE - nrow, LANE), F32)], axis=0)
    tcol = pl.pallas_call(
        _nat_bias_kernel,
        out_shape=jax.ShapeDtypeStruct((LANE, GRID_W * GRID_W), F32),
        compiler_params=pltpu.CompilerParams(vmem_limit_bytes=V7X_VMEM_LIMIT),
        name="nat_bias",
    )(rb, onehot)
    tcol = tcol[:nrow].reshape(NAT_HEADS, ndr, GRID_W, GRID_W)
    tabs = []
    for var in range(NAT_WIN_R):
        lo = NAT_WIN_R - 1 - var
        t = tcol[:, lo:lo + NAT_WIN_R]
        tabs.append(t.transpose(0, 2, 1, 3).reshape(NAT_HEADS, GRID_W, NAT_WIN_R * GRID_W))
    return jnp.stack(tabs, axis=0)


def _nat_lat(zg, cache_k, cache_v, bias_tab, layer):
    q_blk0 = N_CTX // GRID_W
    kv_blk0 = N_CTX // DEC_SEQ

    def variant(r):
        return r - jnp.clip(r - NAT_WIN_R // 2, 0, GRID_ROWS - NAT_WIN_R)

    cache_spec = pl.BlockSpec((None, None, PAST_LEN, BRANCH_W), lambda b, r: (b, layer, 0, 0))
    return pl.pallas_call(
        _nat_lat_kernel,
        grid=(DEC_BATCH, GRID_ROWS),
        in_specs=[
            pl.BlockSpec((GRID_W, CB), lambda b, r: (q_blk0 + b * GRID_ROWS + r, Z_CB0 + ZB_NQ)),
            pl.BlockSpec((DEC_SEQ, CB), lambda b, r: (kv_blk0 + b, Z_CB0 + ZB_NK)),
            pl.BlockSpec((DEC_SEQ, CB), lambda b, r: (kv_blk0 + b, Z_CB0 + ZB_NV)),
            cache_spec, cache_spec,
            pl.BlockSpec((None, NAT_HEADS, GRID_W, NAT_WIN_R * GRID_W), lambda b, r: (variant(r), 0, 0, 0)),
        ],
        out_specs=pl.BlockSpec((GRID_W, BRANCH_W), lambda b, r: (b * GRID_ROWS + r, 0)),
        out_shape=jax.ShapeDtypeStruct((N_LAT, BRANCH_W), BF16),
        compiler_params=_params("parallel", "arbitrary"),
        name="nat_lat",
    )(zg, zg, zg, cache_k, cache_v, bias_tab)


def _diff_kernel(has_cache, lam_init, *refs):
    if has_cache:
        q_ref, k_ref, v_ref, ck_ref, cv_ref, lamp_ref, g_ref, o_ref = refs
    else:
        q_ref, k_ref, v_ref, lamp_ref, g_ref, o_ref = refs
    lp = lamp_ref[...]
    lam = (jnp.exp(jnp.sum(lp[0:1] * lp[1:2], axis=-1, keepdims=True))
           - jnp.exp(jnp.sum(lp[2:3] * lp[3:4], axis=-1, keepdims=True)) + lam_init)
    hv = 2 * DIFF_HD
    for h in range(DIFF_HEADS):
        parts = []
        for i in range(2):
            sl = slice(h * hv + i * DIFF_HD, h * hv + (i + 1) * DIFF_HD)
            q = (q_ref[:, sl] * DIFF_HD ** -0.5).astype(BF16)
            s = _dot_nt(q, k_ref[:, sl].astype(BF16))
            m = jnp.max(s, axis=-1, keepdims=True)
            if has_cache:
                sc = _dot_nt(q, ck_ref[:, sl].astype(BF16))
                m = jnp.maximum(m, jnp.max(sc, axis=-1, keepdims=True))
                ec = jnp.exp(sc - m)
            e = jnp.exp(s - m)
            den = jnp.sum(e, axis=-1, keepdims=True)
            if has_cache:
                den = den + jnp.sum(ec, axis=-1, keepdims=True)
            parts.append((e, ec if has_cache else None, 1.0 / den))
        (e1, ec1, r1), (e2, ec2, r2) = parts
        vs = slice(h * hv, (h + 1) * hv)
        pd = e1 * r1 - e2 * (lam * r2)
        o = _dot(pd.astype(BF16), v_ref[:, vs].astype(BF16))
        if has_cache:
            pdc = ec1 * r1 - ec2 * (lam * r2)
            o = o + _dot(pdc.astype(BF16), cv_ref[:, vs].astype(BF16))
        ms = jnp.mean(o * o, axis=-1, keepdims=True)
        y = o * lax.rsqrt(ms + EPS) * g_ref[...] * (1.0 - lam_init)
        o_ref[:, vs] = y.astype(o_ref.dtype)


def _lam_init(layer):
    return 0.8 - 0.6 * math.exp(-0.3 * layer)


def _diff_ctx(zg, lam_p, subln_g, layer):
    def spec(cb):
        return pl.BlockSpec((SEQ, CB), lambda b: (b, cb))

    return pl.pallas_call(
        functools.partial(_diff_kernel, False, _lam_init(layer)),
        grid=(BATCH,),
        in_specs=[spec(Z_CB0 + ZB_DQ), spec(Z_CB0 + ZB_DK), spec(Z_CB0 + ZB_DV),
                  pl.BlockSpec((4, DIFF_HD), lambda b: (0, 0)),
                  pl.BlockSpec((1, 2 * DIFF_HD), lambda b: (0, 0))],
        out_specs=pl.BlockSpec((SEQ, BRANCH_W), lambda b: (b, 0)),
        out_shape=jax.ShapeDtypeStruct((N_CTX, BRANCH_W), BF16),
        compiler_params=_params("parallel"),
        name="diff_ctx",
    )(zg, zg, zg, lam_p, subln_g.reshape(1, -1))


T_DQ = 256


def _diff_lat(zg, cache_k, cache_v, lam_p, subln_g, layer):
    nq = DEC_SEQ // T_DQ
    q_blk0 = N_CTX // T_DQ
    kv_blk0 = N_CTX // DEC_SEQ
    cache_spec = pl.BlockSpec((None, None, PAST_LEN, BRANCH_W), lambda b, t: (b, layer, 0, 0))
    return pl.pallas_call(
        functools.partial(_diff_kernel, True, _lam_init(layer)),
        grid=(DEC_BATCH, nq),
        in_specs=[
            pl.BlockSpec((T_DQ, CB), lambda b, t: (q_blk0 + b * nq + t, Z_CB0 + ZB_DQ)),
            pl.BlockSpec((DEC_SEQ, CB), lambda b, t: (kv_blk0 + b, Z_CB0 + ZB_DK)),
            pl.BlockSpec((DEC_SEQ, CB), lambda b, t: (kv_blk0 + b, Z_CB0 + ZB_DV)),
            cache_spec, cache_spec,
            pl.BlockSpec((4, DIFF_HD), lambda b, t: (0, 0)),
            pl.BlockSpec((1, 2 * DIFF_HD), lambda b, t: (0, 0)),
        ],
        out_specs=pl.BlockSpec((T_DQ, BRANCH_W), lambda b, t: (b * nq + t, 0)),
        out_shape=jax.ShapeDtypeStruct((N_LAT, BRANCH_W), BF16),
        compiler_params=_params("parallel", "arbitrary"),
        name="diff_lat",
    )(zg, zg, zg, cache_k, cache_v, lam_p, subln_g.reshape(1, -1))


def _merge_kernel(yp_ref, yn_ref, yc_ref, yd_ref, gate_ref, x_ref, mod_ref, g2_ref, wb_ref, wo_ref, wq_ref,
                  xo_ref, h2_ref, q_ref):
    merged = None
    for br, y_ref in enumerate((yp_ref, yn_ref, yc_ref, yd_ref)):
        t = gate_ref[:, br * D_MODEL:(br + 1) * D_MODEL] * _dot(y_ref[...], wb_ref[br])
        merged = t if merged is None else merged + t
    out = _dot(merged.astype(BF16), wo_ref[...])
    x = x_ref[...] + mod_ref[2:3, :] * out
    xo_ref[...] = x
    ms = jnp.mean(x * x, axis=-1, keepdims=True)
    h = x * lax.rsqrt(ms + EPS) * g2_ref[...] * (1.0 + mod_ref[4:5, :]) + mod_ref[3:4, :]
    hb = h.astype(BF16)
    h2_ref[...] = hb
    q_ref[...] = _dot(hb, wq_ref[...])


def _merge(y_pool, y_nat, y_conv, y_diff, zg, x, mod_l, norm2_g, wb, wo, wq):
    yspec = pl.BlockSpec((T_MERGE, BRANCH_W), lambda i: (i, 0))
    qcols = wq.shape[1]
    return pl.pallas_call(
        _merge_kernel,
        grid=(N_TOK // T_MERGE,),
        in_specs=[yspec, yspec, yspec, yspec,
                  pl.BlockSpec((T_MERGE, GATE_COLS), lambda i: (i, 0)),
                  pl.BlockSpec((T_MERGE, D_MODEL), lambda i: (i, 0)),
                  pl.BlockSpec((None, 6, D_MODEL), lambda i: (_mod_row(i, T_MERGE), 0, 0)),
                  pl.BlockSpec((1, D_MODEL), lambda i: (0, 0)),
                  pl.BlockSpec((4, BRANCH_W, D_MODEL), lambda i: (0, 0, 0)),
                  pl.BlockSpec((D_MODEL, D_MODEL), lambda i: (0, 0)),
                  pl.BlockSpec((D_MODEL, qcols), lambda i: (0, 0))],
        out_specs=[pl.BlockSpec((T_MERGE, D_MODEL), lambda i: (i, 0)),
                   pl.BlockSpec((T_MERGE, D_MODEL), lambda i: (i, 0)),
                   pl.BlockSpec((T_MERGE, qcols), lambda i: (i, 0))],
        out_shape=[jax.ShapeDtypeStruct((N_TOK, D_MODEL), F32),
                   jax.ShapeDtypeStruct((N_TOK, D_MODEL), BF16),
                   jax.ShapeDtypeStruct((N_TOK, qcols), F32)],
        compiler_params=_params("parallel"),
        name="merge",
    )(y_pool, y_nat, y_conv, y_diff, zg, x, mod_l, norm2_g, wb, wo, wq)


NO_RANK = float(PEER_TOPK)


def _top_values_ranks(s, n):
    vid = lax.broadcasted_iota(jnp.int32, (n, s.shape[1]), 0)
    vals = jnp.full((n, s.shape[1]), -jnp.inf, F32)
    rank = jnp.full(s.shape, NO_RANK, F32)
    for k in range(n):
        m = jnp.max(s, axis=0, keepdims=True)
        hit = s == m
        vals = jnp.where(vid == k, m, vals)
        rank = jnp.where(hit, float(k), rank)
        s = jnp.where(hit, -jnp.inf, s)
    return vals, rank


N_CAND = 16 + 7 * 8 + 8


def _peer_select_kernel(q_ref, sk_ref, r2_ref, e2_ref, brow_ref, crow_ref):
    q = q_ref[...].astype(BF16)
    half = PEER_NKEYS
    s1_all = _dot_nt(sk_ref[0].astype(BF16), q[:, :half])
    s2_all = _dot_nt(sk_ref[1].astype(BF16), q[:, half:])
    cid = lax.broadcasted_iota(jnp.int32, (N_CAND, LANE), 0)
    rid8 = lax.broadcasted_iota(jnp.int32, (8, LANE), 0)
    for c in range(T_SEL // LANE):
        ls = slice(c * LANE, (c + 1) * LANE)
        s1 = s1_all[:, ls]
        s2 = s2_all[:, ls]
        v1, rank1 = _top_values_ranks(s1, PEER_TOPK)
        v2, rank2 = _top_values_ranks(s2, PEER_TOPK)
        cand = jnp.concatenate([v1[0:1] + v2] + [v1[a:a + 1] + v2[0:8] for a in range(1, 8)]
                               + [v1[8:16] + v2[0:1]], axis=0)
        top0 = v1[0:1] + v2[0:1]
        chosen = jnp.zeros((N_CAND, LANE), F32)
        zsum = jnp.zeros((1, LANE), F32)
        for k in range(PEER_TOPK):
            m = jnp.max(cand, axis=0, keepdims=True)
            first = jnp.min(jnp.where(cand == m, cid, N_CAND), axis=0, keepdims=True)
            hit = cid == first
            chosen = jnp.where(hit, 1.0, chosen)
            zsum = zsum + jnp.exp(m - top0)
            cand = jnp.where(hit, -jnp.inf, cand)
        cnt_lo = jnp.zeros((8, LANE), F32)
        cnt_lo = jnp.where(rid8 == 0, jnp.sum(chosen[0:16], axis=0, keepdims=True), cnt_lo)
        for a in range(1, 8):
            cnt_lo = jnp.where(rid8 == a, jnp.sum(chosen[8 + 8 * a:16 + 8 * a], axis=0, keepdims=True), cnt_lo)
        cnt = jnp.concatenate([cnt_lo, chosen[N_CAND - 8:N_CAND]], axis=0)
        brow = jnp.zeros((PEER_NKEYS, LANE), F32)
        for a in range(PEER_TOPK):
            brow = jnp.where(rank1 == float(a), cnt[a:a + 1], brow)
        brow_ref[:, ls] = brow
        crow_ref[:, ls] = jnp.exp(s1 - v1[0:1]) / zsum
        r2_ref[:, ls] = rank2.astype(BF16)
        e2_ref[:, ls] = jnp.exp(s2 - v2[0:1]).astype(BF16)


def _peer_select(qry, sub_keys):
    nt = N_TOK // T_SEL
    kspec = pl.BlockSpec((None, PEER_NKEYS, T_SEL), lambda i, h: (h, 0, i))

    def kshape(dt):
        return jax.ShapeDtypeStruct((PEER_HEADS, PEER_NKEYS, N_TOK), dt)

    return pl.pallas_call(
        _peer_select_kernel,
        grid=(nt, PEER_HEADS),
        in_specs=[pl.BlockSpec((T_SEL, 2 * PEER_NKEYS), lambda i, h: (i, h)),
                  pl.BlockSpec((None, 2, PEER_NKEYS, PEER_NKEYS), lambda i, h: (h, 0, 0, 0))],
        out_specs=[kspec, kspec, kspec, kspec],
        out_shape=[kshape(BF16), kshape(BF16), kshape(F32), kshape(F32)],
        compiler_params=_params("parallel", "arbitrary"),
        name="peer_select",
    )(qry, sub_keys)


E_PAIR = 2 * PEER_NKEYS
BF16_ROWS = 16


def _gelu(x):
    return 0.5 * x * (1.0 + lax.erf(x * (2.0 ** -0.5)))


def _peer_dense_kernel(h_ref, u_ref, v_ref, brow_ref, crow_ref, r2_ref, e2_ref, x_ref, mod_ref,
                       o_ref, acc_ref, a_scr, p_scr):
    c = pl.program_id(1)

    @pl.when(c == 0)
    def _():
        acc_ref[...] = jnp.zeros_like(acc_ref)

    hb = h_ref[...]
    zero = jnp.zeros((BF16_ROWS, LANE), BF16)
    npair = E_CHUNK // E_PAIR
    for j in range(npair + 1):
        slot = j % 2
        if j < npair:
            a_scr[slot] = _dot_nt(u_ref[j * E_PAIR:(j + 1) * E_PAIR, :].astype(BF16), hb)
        if j > 0:
            acc_ref[...] += _dot_tn(p_scr[1 - slot], v_ref[(j - 1) * E_PAIR:j * E_PAIR, :].astype(BF16))
        if j == npair:
            break
        for half in range(2):
            n1l = 2 * j + half
            for tc in range(T_PEER // LANE):
                ls = slice(tc * LANE, (tc + 1) * LANE)
                b16 = [jnp.broadcast_to(brow_ref[h, n1l:n1l + 1, ls], (BF16_ROWS, LANE)).astype(BF16)
                       for h in range(PEER_HEADS)]
                c16 = [jnp.broadcast_to(crow_ref[h, n1l:n1l + 1, ls], (BF16_ROWS, LANE)).astype(BF16)
                       for h in range(PEER_HEADS)]
                e0 = half * PEER_NKEYS
                act = _gelu(a_scr[slot, e0:e0 + PEER_NKEYS, ls]).astype(BF16)
                for rg in range(PEER_NKEYS // BF16_ROWS):
                    rs = slice(rg * BF16_ROWS, (rg + 1) * BF16_ROWS)
                    g = None
                    for h in range(PEER_HEADS):
                        t = jnp.where(r2_ref[h, rs, ls] < b16[h], e2_ref[h, rs, ls], zero) * c16[h]
                        g = t if g is None else g + t
                    p_scr[slot, e0 + rg * BF16_ROWS:e0 + (rg + 1) * BF16_ROWS, ls] = g * act[rs]

    @pl.when(c == pl.num_programs(1) - 1)
    def _():
        o_ref[...] = x_ref[...] + mod_ref[5:6, :] * acc_ref[...]


def _peer_dense(h2, peer_u, peer_v, r2, e2, brow, crow, x, mod_l):
    nt = N_TOK // T_PEER
    nc = PEER_N // E_CHUNK
    n1c = E_CHUNK // PEER_NKEYS
    rowspec = pl.BlockSpec((PEER_HEADS, n1c, T_PEER), lambda i, c: (0, c, i))
    fullspec = pl.BlockSpec((PEER_HEADS, PEER_NKEYS, T_PEER), lambda i, c: (0, 0, i))
    return pl.pallas_call(
        _peer_dense_kernel,
        grid=(nt, nc),
        in_specs=[pl.BlockSpec((T_PEER, D_MODEL), lambda i, c: (i, 0)),
                  pl.BlockSpec((E_CHUNK, D_MODEL), lambda i, c: (c, 0)),
                  pl.BlockSpec((E_CHUNK, D_MODEL), lambda i, c: (c, 0)),
                  rowspec, rowspec, fullspec, fullspec,
                  pl.BlockSpec((T_PEER, D_MODEL), lambda i, c: (i, 0)),
                  pl.BlockSpec((None, 6, D_MODEL), lambda i, c: (_mod_row(i, T_PEER), 0, 0))],
        out_specs=pl.BlockSpec((T_PEER, D_MODEL), lambda i, c: (i, 0)),
        out_shape=jax.ShapeDtypeStruct((N_TOK, D_MODEL), F32),
        scratch_shapes=[pltpu.VMEM((T_PEER, D_MODEL), F32),
                        pltpu.VMEM((2, E_PAIR, T_PEER), F32),
                        pltpu.VMEM((2, E_PAIR, T_PEER), BF16)],
        compiler_params=_params("parallel", "arbitrary"),
        name="peer_dense",
    )(h2, peer_u, peer_v, brow, crow, r2, e2, x, mod_l)


def kernel(x_prompt, x_sample, cache_nat_k, cache_nat_v, cache_diff_k, cache_diff_v, c, c_ctx, w_ada, b_ada, norm1_g, norm2_g, w_in, pool_w, pool_scale, nat_q_g, nat_k_g, nat_rel_bias, conv_w, conv_b, conv_ln_g, conv_ln_b, diff_q_g, diff_k_g, diff_lambda_p, diff_subln_g, w_branch, w_gate, b_gate, w_out, peer_w_query, peer_sub_keys, peer_u, peer_v):
    x = jnp.concatenate([x_prompt.reshape(N_CTX, D_MODEL), x_sample.reshape(N_LAT, D_MODEL)], axis=0)
    cvec = jnp.concatenate([c_ctx[None, :], c, jnp.zeros((8 - 1 - DEC_BATCH, D_MODEL), F32)], axis=0)
    mod = _modulation(cvec, w_ada, b_ada).reshape(DEPTH, 8, 6, D_MODEL)

    gid = jnp.arange(CB) // NAT_HD
    gmat = (gid[:, None] == gid[None, :]).astype(BF16)
    rope = _rope_tables(T_PRE)
    ck_n = cache_nat_k.reshape(DEC_BATCH, DEPTH, PAST_LEN, BRANCH_W)
    cv_n = cache_nat_v.reshape(DEC_BATCH, DEPTH, PAST_LEN, BRANCH_W)
    ck_d = cache_diff_k.reshape(DEC_BATCH, DEPTH, PAST_LEN, BRANCH_W)
    cv_d = cache_diff_v.reshape(DEC_BATCH, DEPTH, PAST_LEN, BRANCH_W)

    states = []
    for l in range(DEPTH):
        wcat = jnp.concatenate([w_gate[l], w_in[l]], axis=1).astype(BF16)
        bcat = jnp.concatenate([b_gate[l], jnp.zeros((IN_COLS,), F32)]).reshape(1, -1)
        ones = jnp.ones((CB,), F32)
        gg_rows = [ones] * N_CB
        gg_rows[Z_CB0 + ZB_NQ] = jnp.tile(nat_q_g[l], NAT_HEADS)
        gg_rows[Z_CB0 + ZB_NK] = jnp.tile(nat_k_g[l], NAT_HEADS)
        gg_rows[Z_CB0 + ZB_DQ] = jnp.tile(diff_q_g[l], 2 * DIFF_HEADS)
        gg_rows[Z_CB0 + ZB_DK] = jnp.tile(diff_k_g[l], 2 * DIFF_HEADS)
        gg = jnp.stack(gg_rows).reshape(N_CB, 1, CB)

        zg = _pre(x, mod[l], norm1_g[l].reshape(1, -1), wcat, bcat, gg, gmat, rope)

        y_pool, y_conv = _seq_mixers(zg, pool_w[l], pool_scale[l], conv_w[l], conv_b[l],
                                     conv_ln_g[l], conv_ln_b[l])
        y_nat = jnp.concatenate(
            [_nat_ctx(zg), _nat_lat(zg, ck_n, cv_n, _nat_bias_table(nat_rel_bias[l]), l)], axis=0)
        y_diff = jnp.concatenate(
            [_diff_ctx(zg, diff_lambda_p[l], diff_subln_g[l], l),
             _diff_lat(zg, ck_d, cv_d, diff_lambda_p[l], diff_subln_g[l], l)], axis=0)

        x, h2, qry = _merge(y_pool, y_nat, y_conv, y_diff, zg, x, mod[l], norm2_g[l].reshape(1, -1),
                            w_branch[l].astype(BF16), w_out[l].astype(BF16), peer_w_query[l].astype(BF16))
        r2, e2, brow, crow = _peer_select(qry, peer_sub_keys[l])
        x = _peer_dense(h2, peer_u[l], peer_v[l], r2, e2, brow, crow, x, mod[l])

        zc = zg[:N_CTX]
        states.append([zc[:, (Z_CB0 + b) * CB:(Z_CB0 + b + 1) * CB] for b in (ZB_NK, ZB_NV, ZB_DK, ZB_DV)])

    def stack(idx, shape):
        return jnp.stack([states[l][idx].reshape((BATCH, SEQ) + shape) for l in range(DEPTH)], axis=1)

    return (x[:N_CTX].reshape(BATCH, SEQ, D_MODEL),
            x[N_CTX:].reshape(DEC_BATCH, DEC_SEQ, D_MODEL),
            stack(0, (NAT_HEADS, NAT_HD)),
            stack(1, (NAT_HEADS, NAT_HD)),
            stack(2, (DIFF_HEADS, 2, DIFF_HD)),
            stack(3, (DIFF_HEADS, 2 * DIFF_HD)))
```

```python
import functools
import math

import jax
import jax.numpy as jnp
from jax import lax
from jax.experimental import pallas as pl
from jax.experimental.pallas import tpu as pltpu

F32 = jnp.float32
BF16 = jnp.bfloat16

D_MODEL = 1024
BATCH = 16
SEQ = 256
DEPTH = 2
DEC_BATCH = 4
DEC_SEQ = 2048
PAST_LEN = 256
GRID_W = 64
BRANCH_W = 512
POOL_WINDOWS = (2, 4, 8, 16)
POOL_GROUP = 128
NAT_HEADS = 8
NAT_HD = 64
NAT_WIN_R = 8
NAT_WIN_C = 16
CONV_W = 31
DIFF_HEADS = 4
DIFF_HD = 64
IN_COLS = 4608
GATE_COLS = 4096
PEER_HEADS = 8
PEER_NKEYS = 128
PEER_N = PEER_NKEYS * PEER_NKEYS
PEER_TOPK = 16
ROPE_BASE = 10000.0
EPS = 1e-6
NEG_INF = -1e30

N_CTX = BATCH * SEQ
N_LAT = DEC_BATCH * DEC_SEQ
N_TOK = N_CTX + N_LAT
GRID_ROWS = DEC_SEQ // GRID_W

CB = 512
N_GATE_CB = GATE_COLS // CB
ZB_POOL, ZB_NQ, ZB_NK, ZB_NV, ZB_CA, ZB_CGATE, ZB_DQ, ZB_DK, ZB_DV = range(9)

V7X_VMEM_LIMIT = 52 * 1024 * 1024

T_PRE = 1024
T_SEQ = 256
HALO = 16
T_MERGE = 256
T_SEL = 512
T_PEER = 512
E_CHUNK = 1024
LANE = 128


def _sigmoid(x):
    return 1.0 / (1.0 + jnp.exp(-x))


def _dot(a, b):
    return jnp.dot(a, b, preferred_element_type=F32)


def _dot_nt(a, b):
    return lax.dot_general(a, b, (((1,), (1,)), ((), ())), preferred_element_type=F32)


def _dot_tn(a, b):
    return lax.dot_general(a, b, (((0,), (0,)), ((), ())), preferred_element_type=F32)


def _split_bf16(a):
    hi = a.astype(BF16)
    lo = (a - hi.astype(F32)).astype(BF16)
    return hi, lo


def _params(*sem):
    return pltpu.CompilerParams(dimension_semantics=sem, vmem_limit_bytes=V7X_VMEM_LIMIT)


def _mod_row(i, tile):
    nct = N_CTX // tile
    per = DEC_SEQ // tile
    return jnp.where(i < nct, 0, 1 + (i - nct) // per)


def _mod_kernel(c_ref, w_ref, b_ref, o_ref):
    c = c_ref[...]
    a = c * _sigmoid(c)
    w = w_ref[0]
    a_hi, a_lo = _split_bf16(a)
    w_hi, w_lo = _split_bf16(w)
    o_ref[0] = _dot(a_hi, w_hi) + _dot(a_lo, w_hi) + _dot(a_hi, w_lo) + b_ref[0]


def _modulation(cvec, w_ada, b_ada):
    tn = 1024
    return pl.pallas_call(
        _mod_kernel,
        grid=(DEPTH, 6 * D_MODEL // tn),
        in_specs=[
            pl.BlockSpec((8, D_MODEL), lambda l, j: (0, 0)),
            pl.BlockSpec((1, D_MODEL, tn), lambda l, j: (l, 0, j)),
            pl.BlockSpec((1, 1, tn), lambda l, j: (l, 0, j)),
        ],
        out_specs=pl.BlockSpec((1, 8, tn), lambda l, j: (l, 0, j)),
        out_shape=jax.ShapeDtypeStruct((DEPTH, 8, 6 * D_MODEL), F32),
        compiler_params=_params("parallel", "parallel"),
        name="modulation",
    )(cvec, w_ada, b_ada.reshape(DEPTH, 1, 6 * D_MODEL))


def _tile4(t):
    return jnp.concatenate([t, t, t, t], axis=1)


def _hnorm_kernel(x_ref, mod_ref, g_ref, h_ref):
    x = x_ref[...]
    ms = jnp.mean(x * x, axis=-1, keepdims=True)
    y = x * lax.rsqrt(ms + EPS) * g_ref[...]
    h_ref[...] = (y * (1.0 + mod_ref[1:2, :]) + mod_ref[0:1, :]).astype(h_ref.dtype)


def _proj_gate_kernel(h_ref, w_ref, b_ref, o_ref):
    o_ref[...] = _sigmoid(_dot(h_ref[...], w_ref[...]) + b_ref[...])


def _proj_plain_kernel(h_ref, w_ref, o_ref):
    o_ref[...] = _dot(h_ref[...], w_ref[...])


def _proj_norm_kernel(h_ref, w_ref, gg_ref, gmat_ref, rc_ref, rp_ref, rm_ref, o_ref):
    acc = _dot(h_ref[...], w_ref[...])
    hi, lo = _split_bf16(acc * acc)
    ss = _dot(hi, gmat_ref[...]) + _dot(lo, gmat_ref[...])
    y = acc * lax.rsqrt(ss * (1.0 / NAT_HD) + EPS) * gg_ref[0]
    o_ref[...] = (y * _tile4(rc_ref[...])
                  + pltpu.roll(y, 16, 1) * _tile4(rp_ref[...])
                  + pltpu.roll(y, CB - 16, 1) * _tile4(rm_ref[...]))


def _rope_tables(tile):
    quarter = DIFF_HD // 4
    t = jnp.arange(DEC_SEQ)
    lane = jnp.arange(LANE)
    d = lane % DIFF_HD
    freqs = ROPE_BASE ** (-jnp.arange(quarter, dtype=F32) / quarter)
    pos = jnp.where(d[None, :] < DIFF_HD // 2, (t // GRID_W)[:, None], (t % GRID_W)[:, None]).astype(F32)
    ang = pos * freqs[d % quarter][None, :]
    cos = jnp.cos(ang)
    sin = jnp.sin(ang)
    second = (d % (2 * quarter)) >= quarter
    s_plus = jnp.where(second[None, :], sin, 0.0)
    s_minus = jnp.where(second[None, :], 0.0, -sin)
    ones = jnp.ones((tile, LANE), F32)
    zeros = jnp.zeros((tile, LANE), F32)
    return (jnp.concatenate([cos, ones], 0), jnp.concatenate([s_plus, zeros], 0),
            jnp.concatenate([s_minus, zeros], 0))


PLAIN_ZB = (ZB_POOL, ZB_NV, ZB_CA, ZB_CGATE, ZB_DV)
NORM_ZB = (ZB_NQ, ZB_NK, ZB_DQ, ZB_DK)
P_POOL, P_NV, P_CA, P_CGATE, P_DV = range(5)
Q_NQ, Q_NK, Q_DQ, Q_DK = range(4)


def _pick(j, values):
    out = values[0]
    for n, v in enumerate(values[1:], start=1):
        out = jnp.where(j == n, v, out)
    return out


def _pre(x, mod_l, norm_g, wg, bg, w_in, gg, gmat, rope):
    nct = N_CTX // T_PRE
    per = DEC_SEQ // T_PRE
    ident_blk = DEC_SEQ // T_PRE
    nt = N_TOK // T_PRE
    h = pl.pallas_call(
        _hnorm_kernel,
        grid=(nt,),
        in_specs=[pl.BlockSpec((T_PRE, D_MODEL), lambda i: (i, 0)),
                  pl.BlockSpec((None, 6, D_MODEL), lambda i: (_mod_row(i, T_PRE), 0, 0)),
                  pl.BlockSpec((1, D_MODEL), lambda i: (0, 0))],
        out_specs=pl.BlockSpec((T_PRE, D_MODEL), lambda i: (i, 0)),
        out_shape=jax.ShapeDtypeStruct((N_TOK, D_MODEL), BF16),
        compiler_params=_params("parallel"),
        name="hnorm",
    )(x, mod_l, norm_g)

    hspec = pl.BlockSpec((T_PRE, D_MODEL), lambda i, j: (i, 0))
    ospec = pl.BlockSpec((T_PRE, CB), lambda i, j: (i, j))

    zp = pl.pallas_call(
        _proj_plain_kernel,
        grid=(nt, len(PLAIN_ZB)),
        in_specs=[hspec, pl.BlockSpec((D_MODEL, CB), lambda i, j: (0, _pick(j, PLAIN_ZB)))],
        out_specs=ospec,
        out_shape=jax.ShapeDtypeStruct((N_TOK, len(PLAIN_ZB) * CB), F32),
        compiler_params=_params("parallel", "arbitrary"),
        name="proj_plain",
    )(h, w_in)

    def rope_idx(i, j):
        use = ((j == Q_DQ) | (j == Q_DK)) & (i >= nct)
        return jnp.where(use, (i - nct) % per, ident_blk), 0

    rope_spec = pl.BlockSpec((T_PRE, LANE), rope_idx)
    zn = pl.pallas_call(
        _proj_norm_kernel,
        grid=(nt, len(NORM_ZB)),
        in_specs=[hspec, pl.BlockSpec((D_MODEL, CB), lambda i, j: (0, _pick(j, NORM_ZB))),
                  pl.BlockSpec((1, 1, CB), lambda i, j: (j, 0, 0)),
                  pl.BlockSpec((CB, CB), lambda i, j: (0, 0)),
                  rope_spec, rope_spec, rope_spec],
        out_specs=ospec,
        out_shape=jax.ShapeDtypeStruct((N_TOK, len(NORM_ZB) * CB), F32),
        compiler_params=_params("parallel", "arbitrary"),
        name="proj_norm",
    )(h, w_in, gg, gmat, *rope)

    gates = pl.pallas_call(
        _proj_gate_kernel,
        grid=(nt, N_GATE_CB),
        in_specs=[hspec, pl.BlockSpec((D_MODEL, CB), lambda i, j: (0, j)),
                  pl.BlockSpec((1, CB), lambda i, j: (0, j))],
        out_specs=ospec,
        out_shape=jax.ShapeDtypeStruct((N_TOK, GATE_COLS), F32),
        compiler_params=_params("parallel", "arbitrary"),
        name="proj_gate",
    )(h, wg, bg)
    return gates, zp, zn


def _seq_kernel(zp_ref, zp_prev, zp_next, za_ref, za_prev, za_next, zb_ref, zb_prev, zb_next,
                pw_ref, ps_ref, cw_ref, cb_ref, lg_ref, lb_ref,
                yp_ref, yc_ref, pbuf, ubuf, ybuf):
    i = pl.program_id(0)
    nct = N_CTX // T_SEQ
    per = DEC_SEQ // T_SEQ
    is_lat = i >= nct
    kk = (i - nct) % per
    has_prev = is_lat & (kk != 0)
    has_next = is_lat & (kk != per - 1)
    pos0 = jnp.where(is_lat, kk * T_SEQ, 0)
    seq_len = jnp.where(is_lat, DEC_SEQ, SEQ)
    fp = jnp.where(has_prev, 1.0, 0.0).astype(F32)
    fn = jnp.where(has_next, 1.0, 0.0).astype(F32)

    pbuf[0:HALO, :] = zp_prev[...] * fp
    pbuf[HALO:HALO + T_SEQ, :] = zp_ref[...]
    pbuf[HALO + T_SEQ:, :] = zp_next[...] * fn
    ubuf[0:HALO, :] = za_prev[...] * _sigmoid(zb_prev[...]) * fp
    ubuf[HALO:HALO + T_SEQ, :] = za_ref[...] * _sigmoid(zb_ref[...])
    ubuf[HALO + T_SEQ:, :] = za_next[...] * _sigmoid(zb_next[...]) * fn

    pos = pos0 + lax.broadcasted_iota(jnp.int32, (T_SEQ, LANE), 0)
    for gi, w in enumerate(POOL_WINDOWS):
        ls = slice(gi * LANE, (gi + 1) * LANE)
        acc = None
        for s in range(-(w // 2), w - w // 2):
            v = pbuf[HALO + s:HALO + s + T_SEQ, ls]
            acc = v if acc is None else acc + v
        lo = jnp.maximum(pos - w // 2, 0)
        hi = jnp.minimum(pos + (w - w // 2), seq_len)
        mean = acc / (hi - lo).astype(F32)
        d = mean - pbuf[HALO:HALO + T_SEQ, ls]
        y = _dot(d.astype(BF16), pw_ref[gi].astype(BF16)) * ps_ref[:, ls]
        yp_ref[:, ls] = y.astype(yp_ref.dtype)

    base = HALO - CONV_W // 2
    for c in range(BRANCH_W // LANE):
        ls = slice(c * LANE, (c + 1) * LANE)
        acc = None
        for k in range(CONV_W):
            v = ubuf[base + k:base + k + T_SEQ, ls] * cw_ref[k:k + 1, ls]
            acc = v if acc is None else acc + v
        ybuf[:, ls] = acc + cb_ref[:, ls]
    y = ybuf[...]
    mu = jnp.mean(y, axis=-1, keepdims=True)
    yc = y - mu
    var = jnp.mean(yc * yc, axis=-1, keepdims=True)
    yn = yc * lax.rsqrt(var + EPS) * lg_ref[...] + lb_ref[...]
    yc_ref[...] = (yn * _sigmoid(yn)).astype(yc_ref.dtype)


def _seq_mixers(zp, pool_w, pool_scale, conv_w, conv_b, ln_g, ln_b):
    nt = N_TOK // T_SEQ
    hb = T_SEQ // HALO
    n_hblk = N_TOK // HALO

    def specs(cb):
        return [
            pl.BlockSpec((T_SEQ, CB), lambda i: (i, cb)),
            pl.BlockSpec((HALO, CB), lambda i: (jnp.maximum(i * hb - 1, 0), cb)),
            pl.BlockSpec((HALO, CB), lambda i: (jnp.minimum((i + 1) * hb, n_hblk - 1), cb)),
        ]

    def full(shape):
        return pl.BlockSpec(shape, lambda i: (0,) * len(shape))

    cw = jnp.concatenate([conv_w, jnp.zeros((1, BRANCH_W), F32)], axis=0)
    out_spec = pl.BlockSpec((T_SEQ, BRANCH_W), lambda i: (i, 0))
    return pl.pallas_call(
        _seq_kernel,
        grid=(nt,),
        in_specs=(specs(P_POOL) + specs(P_CA) + specs(P_CGATE)
                  + [full((4, POOL_GROUP, POOL_GROUP)), full((1, BRANCH_W)), full((CONV_W + 1, BRANCH_W)),
                     full((1, BRANCH_W)), full((1, BRANCH_W)), full((1, BRANCH_W))]),
        out_specs=[out_spec, out_spec],
        out_shape=[jax.ShapeDtypeStruct((N_TOK, BRANCH_W), BF16)] * 2,
        scratch_shapes=[pltpu.VMEM((T_SEQ + 2 * HALO, BRANCH_W), F32),
                        pltpu.VMEM((T_SEQ + 2 * HALO, BRANCH_W), F32),
                        pltpu.VMEM((T_SEQ, BRANCH_W), F32)],
        compiler_params=_params("parallel"),
        name="pool_conv",
    )(zp, zp, zp, zp, zp, zp, zp, zp, zp, pool_w, pool_scale.reshape(1, -1), cw,
      conv_b.reshape(1, -1), ln_g.reshape(1, -1), ln_b.reshape(1, -1))


def _nat_ctx_kernel(q_ref, k_ref, v_ref, o_ref):
    for h in range(NAT_HEADS):
        sl = slice(h * NAT_HD, (h + 1) * NAT_HD)
        q = (q_ref[:, sl] * NAT_HD ** -0.5).astype(BF16)
        s = _dot_nt(q, k_ref[:, sl].astype(BF16))
        m = jnp.max(s, axis=-1, keepdims=True)
        e = jnp.exp(s - m)
        den = jnp.sum(e, axis=-1, keepdims=True)
        o = _dot(e.astype(BF16), v_ref[:, sl].astype(BF16)) / den
        o_ref[:, sl] = o.astype(o_ref.dtype)


def _nat_ctx(zp, zn):
    def spec(cb):
        return pl.BlockSpec((SEQ, CB), lambda b: (b, cb))

    return pl.pallas_call(
        _nat_ctx_kernel,
        grid=(BATCH,),
        in_specs=[spec(Q_NQ), spec(Q_NK), spec(P_NV)],
        out_specs=pl.BlockSpec((SEQ, BRANCH_W), lambda b: (b, 0)),
        out_shape=jax.ShapeDtypeStruct((N_CTX, BRANCH_W), BF16),
        compiler_params=_params("parallel"),
        name="nat_ctx",
    )(zn, zn, zp)


def _nat_lat_kernel(q_ref, k_ref, v_ref, ck_ref, cv_ref, bias_ref, o_ref):
    r = pl.program_id(1)
    r0 = jnp.clip(r - NAT_WIN_R // 2, 0, GRID_ROWS - NAT_WIN_R)
    start = pl.multiple_of(r0 * GRID_W, GRID_W)
    nwin = NAT_WIN_R * GRID_W
    for h in range(NAT_HEADS):
        sl = slice(h * NAT_HD, (h + 1) * NAT_HD)
        q = (q_ref[:, sl] * NAT_HD ** -0.5).astype(BF16)
        kw = k_ref[pl.ds(start, nwin), sl].astype(BF16)
        vw = v_ref[pl.ds(start, nwin), sl].astype(BF16)
        s_loc = _dot_nt(q, kw) + bias_ref[h]
        s_ctx = _dot_nt(q, ck_ref[:, sl].astype(BF16))
        m = jnp.maximum(jnp.max(s_loc, axis=-1, keepdims=True), jnp.max(s_ctx, axis=-1, keepdims=True))
        e_loc = jnp.exp(s_loc - m)
        e_ctx = jnp.exp(s_ctx - m)
        den = jnp.sum(e_loc, axis=-1, keepdims=True) + jnp.sum(e_ctx, axis=-1, keepdims=True)
        o = (_dot(e_loc.astype(BF16), vw) + _dot(e_ctx.astype(BF16), cv_ref[:, sl].astype(BF16))) / den
        o_ref[:, sl] = o.astype(o_ref.dtype)


def _nat_bias_kernel(rb_ref, oh_ref, o_ref):
    x = rb_ref[...]
    hi = x.astype(BF16)
    r1 = x - hi.astype(F32)
    mid = r1.astype(BF16)
    lo = (r1 - mid.astype(F32)).astype(BF16)
    oh = oh_ref[...]
    o_ref[...] = _dot(hi, oh) + _dot(mid, oh) + _dot(lo, oh)


def _nat_bias_table(rel_bias):
    ndr = 2 * NAT_WIN_R - 1
    ndc = 2 * NAT_WIN_C - 1
    q = jnp.arange(GRID_W)
    kc = jnp.arange(GRID_W)
    dcol = jnp.clip(kc[None, :] - q[:, None] + NAT_WIN_C - 1, 0, ndc - 1)
    wstart = jnp.clip(q - NAT_WIN_C // 2, 0, GRID_W - NAT_WIN_C)
    valid = (kc[None, :] >= wstart[:, None]) & (kc[None, :] < wstart[:, None] + NAT_WIN_C)
    d = jnp.arange(LANE)
    onehot = jnp.where(d[:, None, None] == ndc, jnp.logical_not(valid)[None],
                       (d[:, None, None] == dcol[None]) & valid[None])
    onehot = onehot.reshape(LANE, GRID_W * GRID_W).astype(BF16)
    nrow = NAT_HEADS * ndr
    rb = jnp.concatenate([rel_bias.reshape(nrow, ndc).astype(F32), jnp.full((nrow, 1), NEG_INF, F32),
                          jnp.zeros((nrow, LANE - ndc - 1), F32)], axis=1)
    rb = jnp.concatenate([rb, jnp.zeros((LANE - nrow, LANE), F32)], axis=0)
    tcol = pl.pallas_call(
        _nat_bias_kernel,
        out_shape=jax.ShapeDtypeStruct((LANE, GRID_W * GRID_W), F32),
        compiler_params=pltpu.CompilerParams(vmem_limit_bytes=V7X_VMEM_LIMIT),
        name="nat_bias",
    )(rb, onehot)
    tcol = tcol[:nrow].reshape(NAT_HEADS, ndr, GRID_W, GRID_W)
    tabs = []
    for var in range(NAT_WIN_R):
        lo = NAT_WIN_R - 1 - var
        t = tcol[:, lo:lo + NAT_WIN_R]
        tabs.append(t.transpose(0, 2, 1, 3).reshape(NAT_HEADS, GRID_W, NAT_WIN_R * GRID_W))
    return jnp.stack(tabs, axis=0)


def _nat_lat(zp, zn, cache_k, cache_v, bias_tab, layer):
    q_blk0 = N_CTX // GRID_W
    kv_blk0 = N_CTX // DEC_SEQ

    def variant(r):
        return r - jnp.clip(r - NAT_WIN_R // 2, 0, GRID_ROWS - NAT_WIN_R)

    cache_spec = pl.BlockSpec((None, None, PAST_LEN, BRANCH_W), lambda b, r: (b, layer, 0, 0))
    return pl.pallas_call(
        _nat_lat_kernel,
        grid=(DEC_BATCH, GRID_ROWS),
        in_specs=[
            pl.BlockSpec((GRID_W, CB), lambda b, r: (q_blk0 + b * GRID_ROWS + r, Q_NQ)),
            pl.BlockSpec((DEC_SEQ, CB), lambda b, r: (kv_blk0 + b, Q_NK)),
            pl.BlockSpec((DEC_SEQ, CB), lambda b, r: (kv_blk0 + b, P_NV)),
            cache_spec, cache_spec,
            pl.BlockSpec((None, NAT_HEADS, GRID_W, NAT_WIN_R * GRID_W), lambda b, r: (variant(r), 0, 0, 0)),
        ],
        out_specs=pl.BlockSpec((GRID_W, BRANCH_W), lambda b, r: (b * GRID_ROWS + r, 0)),
        out_shape=jax.ShapeDtypeStruct((N_LAT, BRANCH_W), BF16),
        compiler_params=_params("parallel", "arbitrary"),
        name="nat_lat",
    )(zn, zn, zp, cache_k, cache_v, bias_tab)


def _diff_kernel(has_cache, lam_init, *refs):
    if has_cache:
        q_ref, k_ref, v_ref, ck_ref, cv_ref, lamp_ref, g_ref, o_ref = refs
    else:
        q_ref, k_ref, v_ref, lamp_ref, g_ref, o_ref = refs
    lp = lamp_ref[...]
    lam = (jnp.exp(jnp.sum(lp[0:1] * lp[1:2], axis=-1, keepdims=True))
           - jnp.exp(jnp.sum(lp[2:3] * lp[3:4], axis=-1, keepdims=True)) + lam_init)
    hv = 2 * DIFF_HD
    for h in range(DIFF_HEADS):
        vs = slice(h * hv, (h + 1) * hv)
        vb = v_ref[:, vs].astype(BF16)
        if has_cache:
            cvb = cv_ref[:, vs].astype(BF16)
        parts = []
        for i in range(2):
            sl = slice(h * hv + i * DIFF_HD, h * hv + (i + 1) * DIFF_HD)
            q = (q_ref[:, sl] * DIFF_HD ** -0.5).astype(BF16)
            s = _dot_nt(q, k_ref[:, sl].astype(BF16))
            m = jnp.max(s, axis=-1, keepdims=True)
            if has_cache:
                sc = _dot_nt(q, ck_ref[:, sl].astype(BF16))
                m = jnp.maximum(m, jnp.max(sc, axis=-1, keepdims=True))
            e = jnp.exp(s - m)
            den = jnp.sum(e, axis=-1, keepdims=True)
            o = _dot(e.astype(BF16), vb)
            if has_cache:
                ec = jnp.exp(sc - m)
                den = den + jnp.sum(ec, axis=-1, keepdims=True)
                o = o + _dot(ec.astype(BF16), cvb)
            parts.append((o, 1.0 / den))
        (o1, r1), (o2, r2) = parts
        o = o1 * r1 - o2 * (lam * r2)
        ms = jnp.mean(o * o, axis=-1, keepdims=True)
        y = o * lax.rsqrt(ms + EPS) * g_ref[...] * (1.0 - lam_init)
        o_ref[:, vs] = y.astype(o_ref.dtype)


def _lam_init(layer):
    return 0.8 - 0.6 * math.exp(-0.3 * layer)


def _diff_ctx(zp, zn, lam_p, subln_g, layer):
    def spec(cb):
        return pl.BlockSpec((SEQ, CB), lambda b: (b, cb))

    return pl.pallas_call(
        functools.partial(_diff_kernel, False, _lam_init(layer)),
        grid=(BATCH,),
        in_specs=[spec(Q_DQ), spec(Q_DK), spec(P_DV),
                  pl.BlockSpec((4, DIFF_HD), lambda b: (0, 0)),
                  pl.BlockSpec((1, 2 * DIFF_HD), lambda b: (0, 0))],
        out_specs=pl.BlockSpec((SEQ, BRANCH_W), lambda b: (b, 0)),
        out_shape=jax.ShapeDtypeStruct((N_CTX, BRANCH_W), BF16),
        compiler_params=_params("parallel"),
        name="diff_ctx",
    )(zn, zn, zp, lam_p, subln_g.reshape(1, -1))


T_DQ = 256


def _diff_lat(zp, zn, cache_k, cache_v, lam_p, subln_g, layer):
    nq = DEC_SEQ // T_DQ
    q_blk0 = N_CTX // T_DQ
    kv_blk0 = N_CTX // DEC_SEQ
    cache_spec = pl.BlockSpec((None, None, PAST_LEN, BRANCH_W), lambda b, t: (b, layer, 0, 0))
    return pl.pallas_call(
        functools.partial(_diff_kernel, True, _lam_init(layer)),
        grid=(DEC_BATCH, nq),
        in_specs=[
            pl.BlockSpec((T_DQ, CB), lambda b, t: (q_blk0 + b * nq + t, Q_DQ)),
            pl.BlockSpec((DEC_SEQ, CB), lambda b, t: (kv_blk0 + b, Q_DK)),
            pl.BlockSpec((DEC_SEQ, CB), lambda b, t: (kv_blk0 + b, P_DV)),
            cache_spec, cache_spec,
            pl.BlockSpec((4, DIFF_HD), lambda b, t: (0, 0)),
            pl.BlockSpec((1, 2 * DIFF_HD), lambda b, t: (0, 0)),
        ],
        out_specs=pl.BlockSpec((T_DQ, BRANCH_W), lambda b, t: (b * nq + t, 0)),
        out_shape=jax.ShapeDtypeStruct((N_LAT, BRANCH_W), BF16),
        compiler_params=_params("parallel", "arbitrary"),
        name="diff_lat",
    )(zn, zn, zp, cache_k, cache_v, lam_p, subln_g.reshape(1, -1))


def _merge_kernel(yp_ref, ync_ref, ynl_ref, yc_ref, ydc_ref, ydl_ref, gate_ref, x_ref, mod_ref, g2_ref,
                  wb_ref, wo_ref, wq_ref, xo_ref, h2_ref, q_ref):
    is_ctx = pl.program_id(0) < N_CTX // T_MERGE
    y_nat = jnp.where(is_ctx, ync_ref[...], ynl_ref[...])
    y_diff = jnp.where(is_ctx, ydc_ref[...], ydl_ref[...])
    merged = None
    for br, y in enumerate((yp_ref[...], y_nat, yc_ref[...], y_diff)):
        t = gate_ref[:, br * D_MODEL:(br + 1) * D_MODEL] * _dot(y, wb_ref[br])
        merged = t if merged is None else merged + t
    out = _dot(merged.astype(BF16), wo_ref[...])
    x = x_ref[...] + mod_ref[2:3, :] * out
    xo_ref[...] = x
    ms = jnp.mean(x * x, axis=-1, keepdims=True)
    h = x * lax.rsqrt(ms + EPS) * g2_ref[...] * (1.0 + mod_ref[4:5, :]) + mod_ref[3:4, :]
    hb = h.astype(BF16)
    h2_ref[...] = hb
    q_ref[...] = _dot(hb, wq_ref[...])


def _merge(y_pool, y_nat_ctx, y_nat_lat, y_conv, y_diff_ctx, y_diff_lat, gates, x, mod_l, norm2_g, wb, wo, wq):
    nct = N_CTX // T_MERGE
    yspec = pl.BlockSpec((T_MERGE, BRANCH_W), lambda i: (i, 0))
    cspec = pl.BlockSpec((T_MERGE, BRANCH_W), lambda i: (jnp.minimum(i, nct - 1), 0))
    lspec = pl.BlockSpec((T_MERGE, BRANCH_W), lambda i: (jnp.maximum(i - nct, 0), 0))
    qcols = wq.shape[1]
    return pl.pallas_call(
        _merge_kernel,
        grid=(N_TOK // T_MERGE,),
        in_specs=[yspec, cspec, lspec, yspec, cspec, lspec,
                  pl.BlockSpec((T_MERGE, GATE_COLS), lambda i: (i, 0)),
                  pl.BlockSpec((T_MERGE, D_MODEL), lambda i: (i, 0)),
                  pl.BlockSpec((None, 6, D_MODEL), lambda i: (_mod_row(i, T_MERGE), 0, 0)),
                  pl.BlockSpec((1, D_MODEL), lambda i: (0, 0)),
                  pl.BlockSpec((4, BRANCH_W, D_MODEL), lambda i: (0, 0, 0)),
                  pl.BlockSpec((D_MODEL, D_MODEL), lambda i: (0, 0)),
                  pl.BlockSpec((D_MODEL, qcols), lambda i: (0, 0))],
        out_specs=[pl.BlockSpec((T_MERGE, D_MODEL), lambda i: (i, 0)),
                   pl.BlockSpec((T_MERGE, D_MODEL), lambda i: (i, 0)),
                   pl.BlockSpec((T_MERGE, qcols), lambda i: (i, 0))],
        out_shape=[jax.ShapeDtypeStruct((N_TOK, D_MODEL), F32),
                   jax.ShapeDtypeStruct((N_TOK, D_MODEL), BF16),
                   jax.ShapeDtypeStruct((N_TOK, qcols), F32)],
        compiler_params=_params("parallel"),
        name="merge",
    )(y_pool, y_nat_ctx, y_nat_lat, y_conv, y_diff_ctx, y_diff_lat, gates, x, mod_l, norm2_g, wb, wo, wq)


NO_RANK = float(PEER_TOPK)


def _top_values_ranks(s, n, want_rank):
    vid = lax.broadcasted_iota(jnp.int32, (n, s.shape[1]), 0)
    vals = jnp.full((n, s.shape[1]), -jnp.inf, F32)
    rank = jnp.full(s.shape, NO_RANK, F32) if want_rank else None
    for k in range(n):
        m = jnp.max(s, axis=0, keepdims=True)
        hit = s == m
        vals = jnp.where(vid == k, m, vals)
        if want_rank:
            rank = jnp.where(hit, float(k), rank)
        s = jnp.where(hit, -jnp.inf, s)
    return vals, rank


N_CAND = 16 + 7 * 8 + 8


def _peer_select_kernel(q_ref, sk_ref, r2_ref, e2_ref, brow_ref, crow_ref):
    q = q_ref[...].astype(BF16)
    half = PEER_NKEYS
    s1_all = _dot_nt(sk_ref[0].astype(BF16), q[:, :half])
    s2_all = _dot_nt(sk_ref[1].astype(BF16), q[:, half:])
    cid = lax.broadcasted_iota(jnp.int32, (N_CAND, LANE), 0)
    rid8 = lax.broadcasted_iota(jnp.int32, (8, LANE), 0)
    for c in range(T_SEL // LANE):
        ls = slice(c * LANE, (c + 1) * LANE)
        s1 = s1_all[:, ls]
        s2 = s2_all[:, ls]
        v1, _ = _top_values_ranks(s1, PEER_TOPK, False)
        v2, rank2 = _top_values_ranks(s2, PEER_TOPK, True)
        cand = jnp.concatenate([v1[0:1] + v2] + [v1[a:a + 1] + v2[0:8] for a in range(1, 8)]
                               + [v1[8:16] + v2[0:1]], axis=0)
        top0 = v1[0:1] + v2[0:1]
        chosen = jnp.zeros((N_CAND, LANE), F32)
        zsum = jnp.zeros((1, LANE), F32)
        for k in range(PEER_TOPK):
            m = jnp.max(cand, axis=0, keepdims=True)
            first = jnp.min(jnp.where(cand == m, cid, N_CAND), axis=0, keepdims=True)
            hit = cid == first
            chosen = jnp.where(hit, 1.0, chosen)
            zsum = zsum + jnp.exp(m - top0)
            cand = jnp.where(hit, -jnp.inf, cand)
        cnt_lo = jnp.zeros((8, LANE), F32)
        cnt_lo = jnp.where(rid8 == 0, jnp.sum(chosen[0:16], axis=0, keepdims=True), cnt_lo)
        for a in range(1, 8):
            cnt_lo = jnp.where(rid8 == a, jnp.sum(chosen[8 + 8 * a:16 + 8 * a], axis=0, keepdims=True), cnt_lo)
        cnt = jnp.concatenate([cnt_lo, chosen[N_CAND - 8:N_CAND]], axis=0)
        brow = jnp.zeros((PEER_NKEYS, LANE), F32)
        for a in range(PEER_TOPK):
            brow = jnp.where(s1 == v1[a:a + 1], cnt[a:a + 1], brow)
        brow_ref[:, ls] = brow
        crow_ref[:, ls] = jnp.exp(s1 - v1[0:1]) / zsum
        r2_ref[:, ls] = rank2.astype(BF16)
        e2_ref[:, ls] = jnp.exp(s2 - v2[0:1]).astype(BF16)


def _peer_select(qry, sub_keys):
    nt = N_TOK // T_SEL
    kspec = pl.BlockSpec((None, PEER_NKEYS, T_SEL), lambda i, h: (h, 0, i))

    def kshape(dt):
        return jax.ShapeDtypeStruct((PEER_HEADS, PEER_NKEYS, N_TOK), dt)

    return pl.pallas_call(
        _peer_select_kernel,
        grid=(nt, PEER_HEADS),
        in_specs=[pl.BlockSpec((T_SEL, 2 * PEER_NKEYS), lambda i, h: (i, h)),
                  pl.BlockSpec((None, 2, PEER_NKEYS, PEER_NKEYS), lambda i, h: (h, 0, 0, 0))],
        out_specs=[kspec, kspec, kspec, kspec],
        out_shape=[kshape(BF16), kshape(BF16), kshape(F32), kshape(F32)],
        compiler_params=_params("parallel", "arbitrary"),
        name="peer_select",
    )(qry, sub_keys)


E_PAIR = 2 * PEER_NKEYS
BF16_ROWS = 16


def _row_bf16(row):
    return jnp.broadcast_to(row, (BF16_ROWS, LANE)).astype(BF16)


def _gelu(x):
    return 0.5 * x * (1.0 + lax.erf(x * (2.0 ** -0.5)))


def _peer_dense_kernel(h_ref, u_ref, v_ref, brow_ref, crow_ref, r2_ref, e2_ref, x_ref, mod_ref,
                       o_ref, acc_ref, a_scr, p_scr):
    c = pl.program_id(1)

    @pl.when(c == 0)
    def _():
        acc_ref[...] = jnp.zeros_like(acc_ref)

    hb = h_ref[...]
    zero = jnp.zeros((BF16_ROWS, LANE), BF16)
    npair = E_CHUNK // E_PAIR
    for j in range(npair + 1):
        slot = j % 2
        if j < npair:
            a_scr[slot] = _dot_nt(u_ref[j * E_PAIR:(j + 1) * E_PAIR, :], hb)
        if j > 0:
            acc_ref[...] += _dot_tn(p_scr[1 - slot], v_ref[(j - 1) * E_PAIR:j * E_PAIR, :])
        if j == npair:
            break
        for half in range(2):
            n1l = 2 * j + half
            for tc in range(T_PEER // LANE):
                ls = slice(tc * LANE, (tc + 1) * LANE)
                b16 = [_row_bf16(brow_ref[h, n1l:n1l + 1, ls]) for h in range(PEER_HEADS)]
                c16 = [_row_bf16(crow_ref[h, n1l:n1l + 1, ls]) for h in range(PEER_HEADS)]
                e0 = half * PEER_NKEYS
                act = _gelu(a_scr[slot, e0:e0 + PEER_NKEYS, ls]).astype(BF16)
                for rg in range(PEER_NKEYS // BF16_ROWS):
                    rs = slice(rg * BF16_ROWS, (rg + 1) * BF16_ROWS)
                    g = None
                    for h in range(PEER_HEADS):
                        t = jnp.where(r2_ref[h, rs, ls] < b16[h], e2_ref[h, rs, ls], zero) * c16[h]
                        g = t if g is None else g + t
                    p_scr[slot, e0 + rg * BF16_ROWS:e0 + (rg + 1) * BF16_ROWS, ls] = g * act[rs]

    @pl.when(c == pl.num_programs(1) - 1)
    def _():
        o_ref[...] = x_ref[...] + mod_ref[5:6, :] * acc_ref[...]


def _peer_dense(h2, peer_u, peer_v, r2, e2, brow, crow, x, mod_l):
    nt = N_TOK // T_PEER
    nc = PEER_N // E_CHUNK
    n1c = E_CHUNK // PEER_NKEYS
    rowspec = pl.BlockSpec((PEER_HEADS, n1c, T_PEER), lambda i, c: (0, c, i))
    fullspec = pl.BlockSpec((PEER_HEADS, PEER_NKEYS, T_PEER), lambda i, c: (0, 0, i))
    return pl.pallas_call(
        _peer_dense_kernel,
        grid=(nt, nc),
        in_specs=[pl.BlockSpec((T_PEER, D_MODEL), lambda i, c: (i, 0)),
                  pl.BlockSpec((E_CHUNK, D_MODEL), lambda i, c: (c, 0)),
                  pl.BlockSpec((E_CHUNK, D_MODEL), lambda i, c: (c, 0)),
                  rowspec, rowspec, fullspec, fullspec,
                  pl.BlockSpec((T_PEER, D_MODEL), lambda i, c: (i, 0)),
                  pl.BlockSpec((None, 6, D_MODEL), lambda i, c: (_mod_row(i, T_PEER), 0, 0))],
        out_specs=pl.BlockSpec((T_PEER, D_MODEL), lambda i, c: (i, 0)),
        out_shape=jax.ShapeDtypeStruct((N_TOK, D_MODEL), F32),
        scratch_shapes=[pltpu.VMEM((T_PEER, D_MODEL), F32),
                        pltpu.VMEM((2, E_PAIR, T_PEER), F32),
                        pltpu.VMEM((2, E_PAIR, T_PEER), BF16)],
        compiler_params=_params("parallel", "arbitrary"),
        name="peer_dense",
    )(h2, peer_u, peer_v, brow, crow, r2, e2, x, mod_l)


def kernel(x_prompt, x_sample, cache_nat_k, cache_nat_v, cache_diff_k, cache_diff_v, c, c_ctx, w_ada, b_ada, norm1_g, norm2_g, w_in, pool_w, pool_scale, nat_q_g, nat_k_g, nat_rel_bias, conv_w, conv_b, conv_ln_g, conv_ln_b, diff_q_g, diff_k_g, diff_lambda_p, diff_subln_g, w_branch, w_gate, b_gate, w_out, peer_w_query, peer_sub_keys, peer_u, peer_v):
    x = jnp.concatenate([x_prompt.reshape(N_CTX, D_MODEL), x_sample.reshape(N_LAT, D_MODEL)], axis=0)
    cvec = jnp.concatenate([c_ctx[None, :], c, jnp.zeros((8 - 1 - DEC_BATCH, D_MODEL), F32)], axis=0)
    mod = _modulation(cvec, w_ada, b_ada).reshape(DEPTH, 8, 6, D_MODEL)

    gid = jnp.arange(CB) // NAT_HD
    gmat = (gid[:, None] == gid[None, :]).astype(BF16)
    rope = _rope_tables(T_PRE)
    ck_n = cache_nat_k.reshape(DEC_BATCH, DEPTH, PAST_LEN, BRANCH_W)
    cv_n = cache_nat_v.reshape(DEC_BATCH, DEPTH, PAST_LEN, BRANCH_W)
    ck_d = cache_diff_k.reshape(DEC_BATCH, DEPTH, PAST_LEN, BRANCH_W)
    cv_d = cache_diff_v.reshape(DEC_BATCH, DEPTH, PAST_LEN, BRANCH_W)

    states = []
    for l in range(DEPTH):
        gg = jnp.stack([jnp.tile(nat_q_g[l], NAT_HEADS), jnp.tile(nat_k_g[l], NAT_HEADS),
                        jnp.tile(diff_q_g[l], 2 * DIFF_HEADS), jnp.tile(diff_k_g[l], 2 * DIFF_HEADS)])
        gates, zp, zn = _pre(x, mod[l], norm1_g[l].reshape(1, -1), w_gate[l].astype(BF16),
                             b_gate[l].reshape(1, -1), w_in[l].astype(BF16), gg.reshape(len(NORM_ZB), 1, CB),
                             gmat, rope)

        y_pool, y_conv = _seq_mixers(zp, pool_w[l], pool_scale[l], conv_w[l], conv_b[l],
                                     conv_ln_g[l], conv_ln_b[l])
        y_nat_ctx = _nat_ctx(zp, zn)
        y_nat_lat = _nat_lat(zp, zn, ck_n, cv_n, _nat_bias_table(nat_rel_bias[l]), l)
        y_diff_ctx = _diff_ctx(zp, zn, diff_lambda_p[l], diff_subln_g[l], l)
        y_diff_lat = _diff_lat(zp, zn, ck_d, cv_d, diff_lambda_p[l], diff_subln_g[l], l)

        x, h2, qry = _merge(y_pool, y_nat_ctx, y_nat_lat, y_conv, y_diff_ctx, y_diff_lat, gates, x, mod[l],
                            norm2_g[l].reshape(1, -1), w_branch[l].astype(BF16), w_out[l].astype(BF16),
                            peer_w_query[l].astype(BF16))
        r2, e2, brow, crow = _peer_select(qry, peer_sub_keys[l])
        x = _peer_dense(h2, peer_u[l].astype(BF16), peer_v[l].astype(BF16), r2, e2, brow, crow, x, mod[l])

        states.append([zn[:N_CTX, Q_NK * CB:(Q_NK + 1) * CB], zp[:N_CTX, P_NV * CB:(P_NV + 1) * CB],
                       zn[:N_CTX, Q_DK * CB:(Q_DK + 1) * CB], zp[:N_CTX, P_DV * CB:(P_DV + 1) * CB]])

    def stack(idx, shape):
        return jnp.stack([states[l][idx].reshape((BATCH, SEQ) + shape) for l in range(DEPTH)], axis=1)

    return (x[:N_CTX].reshape(BATCH, SEQ, D_MODEL),
            x[N_CTX:].reshape(DEC_BATCH, DEC_SEQ, D_MODEL),
            stack(0, (NAT_HEADS, NAT_HD)),
            stack(1, (NAT_HEADS, NAT_HD)),
            stack(2, (DIFF_HEADS, 2, DIFF_HD)),
            stack(3, (DIFF_HEADS, 2 * DIFF_HD)))
```

```python
import functools
import math

import jax
import jax.numpy as jnp
from jax import lax
from jax.experimental import pallas as pl
from jax.experimental.pallas import tpu as pltpu

F32 = jnp.float32
BF16 = jnp.bfloat16

D_MODEL = 1024
BATCH = 16
SEQ = 256
DEPTH = 2
DEC_BATCH = 4
DEC_SEQ = 2048
PAST_LEN = 256
GRID_W = 64
BRANCH_W = 512
POOL_WINDOWS = (2, 4, 8, 16)
POOL_GROUP = 128
NAT_HEADS = 8
NAT_HD = 64
NAT_WIN_R = 8
NAT_WIN_C = 16
CONV_W = 31
DIFF_HEADS = 4
DIFF_HD = 64
IN_COLS = 4608
GATE_COLS = 4096
PEER_HEADS = 8
PEER_NKEYS = 128
PEER_N = PEER_NKEYS * PEER_NKEYS
PEER_TOPK = 16
ROPE_BASE = 10000.0
EPS = 1e-6
NEG_INF = -1e30
LOG2E = math.log2(math.e)

N_CTX = BATCH * SEQ
N_LAT = DEC_BATCH * DEC_SEQ
N_TOK = N_CTX + N_LAT
GRID_ROWS = DEC_SEQ // GRID_W

CB = 512
N_GATE_CB = GATE_COLS // CB
ZB_POOL, ZB_NQ, ZB_NK, ZB_NV, ZB_CA, ZB_CGATE, ZB_DQ, ZB_DK, ZB_DV = range(9)

V7X_VMEM_LIMIT = 52 * 1024 * 1024

T_PRE = 1024
T_SEQ = 256
HALO = 16
T_MERGE = 256
T_SEL = 512
T_PEER = 512
E_CHUNK = 1024
LANE = 128


def _sigmoid(x):
    return 1.0 / (1.0 + jnp.exp(-x))


def _dot(a, b):
    return jnp.dot(a, b, preferred_element_type=F32)


def _dot_nt(a, b):
    return lax.dot_general(a, b, (((1,), (1,)), ((), ())), preferred_element_type=F32)


def _dot_tn(a, b):
    return lax.dot_general(a, b, (((0,), (0,)), ((), ())), preferred_element_type=F32)


def _split_bf16(a):
    hi = a.astype(BF16)
    lo = (a - hi.astype(F32)).astype(BF16)
    return hi, lo


def _params(*sem):
    return pltpu.CompilerParams(dimension_semantics=sem, vmem_limit_bytes=V7X_VMEM_LIMIT)


def _mod_row(i, tile):
    nct = N_CTX // tile
    per = DEC_SEQ // tile
    return jnp.where(i < nct, 0, 1 + (i - nct) // per)


def _mod_kernel(c_ref, w_ref, b_ref, o_ref):
    c = c_ref[...]
    a = c * _sigmoid(c)
    w = w_ref[0]
    a_hi, a_lo = _split_bf16(a)
    w_hi, w_lo = _split_bf16(w)
    o_ref[0] = _dot(a_hi, w_hi) + _dot(a_lo, w_hi) + _dot(a_hi, w_lo) + b_ref[0]


def _modulation(cvec, w_ada, b_ada):
    tn = 1024
    return pl.pallas_call(
        _mod_kernel,
        grid=(DEPTH, 6 * D_MODEL // tn),
        in_specs=[
            pl.BlockSpec((8, D_MODEL), lambda l, j: (0, 0)),
            pl.BlockSpec((1, D_MODEL, tn), lambda l, j: (l, 0, j)),
            pl.BlockSpec((1, 1, tn), lambda l, j: (l, 0, j)),
        ],
        out_specs=pl.BlockSpec((1, 8, tn), lambda l, j: (l, 0, j)),
        out_shape=jax.ShapeDtypeStruct((DEPTH, 8, 6 * D_MODEL), F32),
        compiler_params=_params("parallel", "parallel"),
        name="modulation",
    )(cvec, w_ada, b_ada.reshape(DEPTH, 1, 6 * D_MODEL))


def _tile4(t):
    return jnp.concatenate([t, t, t, t], axis=1)


def _hnorm_kernel(x_ref, mod_ref, g_ref, h_ref):
    x = x_ref[...]
    ms = jnp.mean(x * x, axis=-1, keepdims=True)
    y = x * lax.rsqrt(ms + EPS) * g_ref[...]
    h_ref[...] = (y * (1.0 + mod_ref[1:2, :]) + mod_ref[0:1, :]).astype(h_ref.dtype)


def _proj_gate_kernel(h_ref, w_ref, b_ref, o_ref):
    o_ref[...] = _sigmoid(_dot(h_ref[...], w_ref[...]) + b_ref[...])


def _proj_plain_kernel(h_ref, w_ref, o_ref):
    o_ref[...] = _dot(h_ref[...], w_ref[...])


def _proj_norm_kernel(h_ref, w_ref, gg_ref, gmat_ref, rc_ref, rp_ref, rm_ref, o_ref):
    acc = _dot(h_ref[...], w_ref[...])
    hi, lo = _split_bf16(acc * acc)
    ss = _dot(hi, gmat_ref[...]) + _dot(lo, gmat_ref[...])
    y = acc * lax.rsqrt(ss * (1.0 / NAT_HD) + EPS) * gg_ref[0]
    o_ref[...] = (y * _tile4(rc_ref[...])
                  + pltpu.roll(y, 16, 1) * _tile4(rp_ref[...])
                  + pltpu.roll(y, CB - 16, 1) * _tile4(rm_ref[...]))


def _rope_tables(tile):
    quarter = DIFF_HD // 4
    t = jnp.arange(DEC_SEQ)
    lane = jnp.arange(LANE)
    d = lane % DIFF_HD
    freqs = ROPE_BASE ** (-jnp.arange(quarter, dtype=F32) / quarter)
    pos = jnp.where(d[None, :] < DIFF_HD // 2, (t // GRID_W)[:, None], (t % GRID_W)[:, None]).astype(F32)
    ang = pos * freqs[d % quarter][None, :]
    cos = jnp.cos(ang)
    sin = jnp.sin(ang)
    second = (d % (2 * quarter)) >= quarter
    s_plus = jnp.where(second[None, :], sin, 0.0)
    s_minus = jnp.where(second[None, :], 0.0, -sin)
    ones = jnp.ones((tile, LANE), F32)
    zeros = jnp.zeros((tile, LANE), F32)
    return (jnp.concatenate([cos, ones], 0), jnp.concatenate([s_plus, zeros], 0),
            jnp.concatenate([s_minus, zeros], 0))


PLAIN_ZB = (ZB_POOL, ZB_NV, ZB_CA, ZB_CGATE, ZB_DV)
NORM_ZB = (ZB_NQ, ZB_NK, ZB_DQ, ZB_DK)
P_POOL, P_NV, P_CA, P_CGATE, P_DV = range(5)
Q_NQ, Q_NK, Q_DQ, Q_DK = range(4)


def _pick(j, values):
    out = values[0]
    for n, v in enumerate(values[1:], start=1):
        out = jnp.where(j == n, v, out)
    return out


def _pre(x, mod_l, norm_g, wg, bg, w_in, gg, gmat, rope):
    nct = N_CTX // T_PRE
    per = DEC_SEQ // T_PRE
    ident_blk = DEC_SEQ // T_PRE
    nt = N_TOK // T_PRE
    h = pl.pallas_call(
        _hnorm_kernel,
        grid=(nt,),
        in_specs=[pl.BlockSpec((T_PRE, D_MODEL), lambda i: (i, 0)),
                  pl.BlockSpec((None, 6, D_MODEL), lambda i: (_mod_row(i, T_PRE), 0, 0)),
                  pl.BlockSpec((1, D_MODEL), lambda i: (0, 0))],
        out_specs=pl.BlockSpec((T_PRE, D_MODEL), lambda i: (i, 0)),
        out_shape=jax.ShapeDtypeStruct((N_TOK, D_MODEL), BF16),
        compiler_params=_params("parallel"),
        name="hnorm",
    )(x, mod_l, norm_g)

    hspec = pl.BlockSpec((T_PRE, D_MODEL), lambda i, j: (i, 0))
    ospec = pl.BlockSpec((T_PRE, CB), lambda i, j: (i, j))

    zp = pl.pallas_call(
        _proj_plain_kernel,
        grid=(nt, len(PLAIN_ZB)),
        in_specs=[hspec, pl.BlockSpec((D_MODEL, CB), lambda i, j: (0, _pick(j, PLAIN_ZB)))],
        out_specs=ospec,
        out_shape=jax.ShapeDtypeStruct((N_TOK, len(PLAIN_ZB) * CB), F32),
        compiler_params=_params("parallel", "arbitrary"),
        name="proj_plain",
    )(h, w_in)

    def rope_idx(i, j):
        use = ((j == Q_DQ) | (j == Q_DK)) & (i >= nct)
        return jnp.where(use, (i - nct) % per, ident_blk), 0

    rope_spec = pl.BlockSpec((T_PRE, LANE), rope_idx)
    zn = pl.pallas_call(
        _proj_norm_kernel,
        grid=(nt, len(NORM_ZB)),
        in_specs=[hspec, pl.BlockSpec((D_MODEL, CB), lambda i, j: (0, _pick(j, NORM_ZB))),
                  pl.BlockSpec((1, 1, CB), lambda i, j: (j, 0, 0)),
                  pl.BlockSpec((CB, CB), lambda i, j: (0, 0)),
                  rope_spec, rope_spec, rope_spec],
        out_specs=ospec,
        out_shape=jax.ShapeDtypeStruct((N_TOK, len(NORM_ZB) * CB), F32),
        compiler_params=_params("parallel", "arbitrary"),
        name="proj_norm",
    )(h, w_in, gg, gmat, *rope)

    gates = pl.pallas_call(
        _proj_gate_kernel,
        grid=(nt, N_GATE_CB),
        in_specs=[hspec, pl.BlockSpec((D_MODEL, CB), lambda i, j: (0, j)),
                  pl.BlockSpec((1, CB), lambda i, j: (0, j))],
        out_specs=ospec,
        out_shape=jax.ShapeDtypeStruct((N_TOK, GATE_COLS), F32),
        compiler_params=_params("parallel", "arbitrary"),
        name="proj_gate",
    )(h, wg, bg)
    return gates, zp, zn


def _seq_kernel(zp_ref, zp_prev, zp_next, za_ref, za_prev, za_next, zb_ref, zb_prev, zb_next,
                pw_ref, ps_ref, cw_ref, cb_ref, lg_ref, lb_ref,
                yp_ref, yc_ref, pbuf, ubuf, ybuf):
    i = pl.program_id(0)
    nct = N_CTX // T_SEQ
    per = DEC_SEQ // T_SEQ
    is_lat = i >= nct
    kk = (i - nct) % per
    has_prev = is_lat & (kk != 0)
    has_next = is_lat & (kk != per - 1)
    pos0 = jnp.where(is_lat, kk * T_SEQ, 0)
    seq_len = jnp.where(is_lat, DEC_SEQ, SEQ)
    fp = jnp.where(has_prev, 1.0, 0.0).astype(F32)
    fn = jnp.where(has_next, 1.0, 0.0).astype(F32)

    pbuf[0:HALO, :] = zp_prev[...] * fp
    pbuf[HALO:HALO + T_SEQ, :] = zp_ref[...]
    pbuf[HALO + T_SEQ:, :] = zp_next[...] * fn
    ubuf[0:HALO, :] = za_prev[...] * _sigmoid(zb_prev[...]) * fp
    ubuf[HALO:HALO + T_SEQ, :] = za_ref[...] * _sigmoid(zb_ref[...])
    ubuf[HALO + T_SEQ:, :] = za_next[...] * _sigmoid(zb_next[...]) * fn

    pos = pos0 + lax.broadcasted_iota(jnp.int32, (T_SEQ, LANE), 0)
    for gi, w in enumerate(POOL_WINDOWS):
        ls = slice(gi * LANE, (gi + 1) * LANE)
        acc = None
        for s in range(-(w // 2), w - w // 2):
            v = pbuf[HALO + s:HALO + s + T_SEQ, ls]
            acc = v if acc is None else acc + v
        lo = jnp.maximum(pos - w // 2, 0)
        hi = jnp.minimum(pos + (w - w // 2), seq_len)
        mean = acc / (hi - lo).astype(F32)
        d = mean - pbuf[HALO:HALO + T_SEQ, ls]
        y = _dot(d.astype(BF16), pw_ref[gi].astype(BF16)) * ps_ref[:, ls]
        yp_ref[:, ls] = y.astype(yp_ref.dtype)

    base = HALO - CONV_W // 2
    for c in range(BRANCH_W // LANE):
        ls = slice(c * LANE, (c + 1) * LANE)
        acc = None
        for k in range(CONV_W):
            v = ubuf[base + k:base + k + T_SEQ, ls] * cw_ref[k:k + 1, ls]
            acc = v if acc is None else acc + v
        ybuf[:, ls] = acc + cb_ref[:, ls]
    y = ybuf[...]
    mu = jnp.mean(y, axis=-1, keepdims=True)
    yc = y - mu
    var = jnp.mean(yc * yc, axis=-1, keepdims=True)
    yn = yc * lax.rsqrt(var + EPS) * lg_ref[...] + lb_ref[...]
    yc_ref[...] = (yn * _sigmoid(yn)).astype(yc_ref.dtype)


def _seq_mixers(zp, pool_w, pool_scale, conv_w, conv_b, ln_g, ln_b):
    nt = N_TOK // T_SEQ
    hb = T_SEQ // HALO
    n_hblk = N_TOK // HALO

    def specs(cb):
        return [
            pl.BlockSpec((T_SEQ, CB), lambda i: (i, cb)),
            pl.BlockSpec((HALO, CB), lambda i: (jnp.maximum(i * hb - 1, 0), cb)),
            pl.BlockSpec((HALO, CB), lambda i: (jnp.minimum((i + 1) * hb, n_hblk - 1), cb)),
        ]

    def full(shape):
        return pl.BlockSpec(shape, lambda i: (0,) * len(shape))

    cw = jnp.concatenate([conv_w, jnp.zeros((1, BRANCH_W), F32)], axis=0)
    out_spec = pl.BlockSpec((T_SEQ, BRANCH_W), lambda i: (i, 0))
    return pl.pallas_call(
        _seq_kernel,
        grid=(nt,),
        in_specs=(specs(P_POOL) + specs(P_CA) + specs(P_CGATE)
                  + [full((4, POOL_GROUP, POOL_GROUP)), full((1, BRANCH_W)), full((CONV_W + 1, BRANCH_W)),
                     full((1, BRANCH_W)), full((1, BRANCH_W)), full((1, BRANCH_W))]),
        out_specs=[out_spec, out_spec],
        out_shape=[jax.ShapeDtypeStruct((N_TOK, BRANCH_W), BF16)] * 2,
        scratch_shapes=[pltpu.VMEM((T_SEQ + 2 * HALO, BRANCH_W), F32),
                        pltpu.VMEM((T_SEQ + 2 * HALO, BRANCH_W), F32),
                        pltpu.VMEM((T_SEQ, BRANCH_W), F32)],
        compiler_params=_params("parallel"),
        name="pool_conv",
    )(zp, zp, zp, zp, zp, zp, zp, zp, zp, pool_w, pool_scale.reshape(1, -1), cw,
      conv_b.reshape(1, -1), ln_g.reshape(1, -1), ln_b.reshape(1, -1))


def _nat_ctx_kernel(q_ref, k_ref, v_ref, o_ref):
    for h in range(NAT_HEADS):
        sl = slice(h * NAT_HD, (h + 1) * NAT_HD)
        q = (q_ref[:, sl] * (NAT_HD ** -0.5 * LOG2E)).astype(BF16)
        s = _dot_nt(q, k_ref[:, sl].astype(BF16))
        m = jnp.max(s, axis=-1, keepdims=True)
        e = jnp.exp2(s - m)
        den = jnp.sum(e, axis=-1, keepdims=True)
        o = _dot(e.astype(BF16), v_ref[:, sl].astype(BF16)) / den
        o_ref[:, sl] = o.astype(o_ref.dtype)


def _nat_ctx(zp, zn):
    def spec(cb):
        return pl.BlockSpec((SEQ, CB), lambda b: (b, cb))

    return pl.pallas_call(
        _nat_ctx_kernel,
        grid=(BATCH,),
        in_specs=[spec(Q_NQ), spec(Q_NK), spec(P_NV)],
        out_specs=pl.BlockSpec((SEQ, BRANCH_W), lambda b: (b, 0)),
        out_shape=jax.ShapeDtypeStruct((N_CTX, BRANCH_W), BF16),
        compiler_params=_params("parallel"),
        name="nat_ctx",
    )(zn, zn, zp)


NAT_QROWS = 4
NAT_KROWS = 12


def _nat_row0(r):
    return jnp.clip(r - NAT_WIN_R // 2, 0, GRID_ROWS - NAT_WIN_R)


def _nat_lat_kernel(q_ref, k_ref, v_ref, ck_ref, cv_ref, bias_ref, o_ref):
    g = pl.program_id(1)
    w0 = jnp.minimum(_nat_row0(g * NAT_QROWS), GRID_ROWS - NAT_KROWS)
    start = pl.multiple_of(w0 * GRID_W, GRID_W)
    nwin = NAT_KROWS * GRID_W
    for h in range(NAT_HEADS):
        sl = slice(h * NAT_HD, (h + 1) * NAT_HD)
        q = (q_ref[:, sl] * (NAT_HD ** -0.5 * LOG2E)).astype(BF16)
        kw = k_ref[pl.ds(start, nwin), sl].astype(BF16)
        vw = v_ref[pl.ds(start, nwin), sl].astype(BF16)
        s_loc = _dot_nt(q, kw) + bias_ref[h]
        s_ctx = _dot_nt(q, ck_ref[:, sl].astype(BF16))
        m = jnp.maximum(jnp.max(s_loc, axis=-1, keepdims=True), jnp.max(s_ctx, axis=-1, keepdims=True))
        e_loc = jnp.exp2(s_loc - m)
        e_ctx = jnp.exp2(s_ctx - m)
        den = jnp.sum(e_loc, axis=-1, keepdims=True) + jnp.sum(e_ctx, axis=-1, keepdims=True)
        o = (_dot(e_loc.astype(BF16), vw) + _dot(e_ctx.astype(BF16), cv_ref[:, sl].astype(BF16))) / den
        o_ref[:, sl] = o.astype(o_ref.dtype)


def _nat_bias_kernel(rb_ref, oh_ref, o_ref):
    x = rb_ref[...]
    hi = x.astype(BF16)
    r1 = x - hi.astype(F32)
    mid = r1.astype(BF16)
    lo = (r1 - mid.astype(F32)).astype(BF16)
    oh = oh_ref[...]
    o_ref[...] = _dot(hi, oh) + _dot(mid, oh) + _dot(lo, oh)


def _nat_bias_table(rel_bias):
    ndr = 2 * NAT_WIN_R - 1
    ndc = 2 * NAT_WIN_C - 1
    q = jnp.arange(GRID_W)
    kc = jnp.arange(GRID_W)
    dcol = jnp.clip(kc[None, :] - q[:, None] + NAT_WIN_C - 1, 0, ndc - 1)
    wstart = jnp.clip(q - NAT_WIN_C // 2, 0, GRID_W - NAT_WIN_C)
    valid = (kc[None, :] >= wstart[:, None]) & (kc[None, :] < wstart[:, None] + NAT_WIN_C)
    d = jnp.arange(LANE)
    onehot = jnp.where(d[:, None, None] == ndc, jnp.logical_not(valid)[None],
                       (d[:, None, None] == dcol[None]) & valid[None])
    onehot = onehot.reshape(LANE, GRID_W * GRID_W).astype(BF16)
    nrow = NAT_HEADS * ndr
    rb = jnp.concatenate([rel_bias.reshape(nrow, ndc).astype(F32) * LOG2E, jnp.full((nrow, 1), NEG_INF, F32),
                          jnp.zeros((nrow, LANE - ndc - 1), F32)], axis=1)
    rb = jnp.concatenate([rb, jnp.zeros((LANE - nrow, LANE), F32)], axis=0)
    tcol = pl.pallas_call(
        _nat_bias_kernel,
        out_shape=jax.ShapeDtypeStruct((LANE, GRID_W * GRID_W), F32),
        compiler_params=pltpu.CompilerParams(vmem_limit_bytes=V7X_VMEM_LIMIT),
        name="nat_bias",
    )(rb, onehot)
    tcol = tcol[:nrow].reshape(NAT_HEADS, ndr, GRID_W, GRID_W)
    neg = jnp.full((NAT_HEADS, GRID_W, GRID_W), NEG_INF, F32)
    tables, variant_of_group, seen = [], [], {}
    for g in range(GRID_ROWS // NAT_QROWS):
        r_first = g * NAT_QROWS
        w0 = min(max(r_first - NAT_WIN_R // 2, 0), GRID_ROWS - NAT_WIN_R, GRID_ROWS - NAT_KROWS)
        rows = [(r_first + j - w0, min(max(r_first + j - NAT_WIN_R // 2, 0), GRID_ROWS - NAT_WIN_R) - w0)
                for j in range(NAT_QROWS)]
        key = tuple(rows)
        if key not in seen:
            seen[key] = len(tables)
            blocks = []
            for rq, rw in rows:
                blocks.append(jnp.concatenate(
                    [tcol[:, i - rq + NAT_WIN_R - 1] if rw <= i < rw + NAT_WIN_R else neg
                     for i in range(NAT_KROWS)], axis=-1))
            tables.append(jnp.concatenate(blocks, axis=1))
        variant_of_group.append(seen[key])
    return jnp.stack(tables, axis=0), tuple(variant_of_group)


def _nat_lat(zp, zn, cache_k, cache_v, bias_tab, variant_of_group, layer):
    tq = NAT_QROWS * GRID_W
    ngroups = GRID_ROWS // NAT_QROWS
    q_blk0 = N_CTX // tq
    kv_blk0 = N_CTX // DEC_SEQ
    cache_spec = pl.BlockSpec((None, None, PAST_LEN, BRANCH_W), lambda b, g: (b, layer, 0, 0))
    return pl.pallas_call(
        _nat_lat_kernel,
        grid=(DEC_BATCH, ngroups),
        in_specs=[
            pl.BlockSpec((tq, CB), lambda b, g: (q_blk0 + b * ngroups + g, Q_NQ)),
            pl.BlockSpec((DEC_SEQ, CB), lambda b, g: (kv_blk0 + b, Q_NK)),
            pl.BlockSpec((DEC_SEQ, CB), lambda b, g: (kv_blk0 + b, P_NV)),
            cache_spec, cache_spec,
            pl.BlockSpec((None, NAT_HEADS, tq, NAT_KROWS * GRID_W),
                         lambda b, g: (_pick(g, variant_of_group), 0, 0, 0)),
        ],
        out_specs=pl.BlockSpec((tq, BRANCH_W), lambda b, g: (b * ngroups + g, 0)),
        out_shape=jax.ShapeDtypeStruct((N_LAT, BRANCH_W), BF16),
        compiler_params=_params("parallel", "arbitrary"),
        name="nat_lat",
    )(zn, zn, zp, cache_k, cache_v, bias_tab)


def _diff_kernel(has_cache, lam_init, *refs):
    if has_cache:
        q_ref, k_ref, v_ref, ck_ref, cv_ref, lamp_ref, g_ref, o_ref = refs
    else:
        q_ref, k_ref, v_ref, lamp_ref, g_ref, o_ref = refs
    lp = lamp_ref[...]
    lam = (jnp.exp(jnp.sum(lp[0:1] * lp[1:2], axis=-1, keepdims=True))
           - jnp.exp(jnp.sum(lp[2:3] * lp[3:4], axis=-1, keepdims=True)) + lam_init)
    hv = 2 * DIFF_HD
    for h in range(DIFF_HEADS):
        vs = slice(h * hv, (h + 1) * hv)
        vb = v_ref[:, vs].astype(BF16)
        if has_cache:
            cvb = cv_ref[:, vs].astype(BF16)
        parts = []
        for i in range(2):
            sl = slice(h * hv + i * DIFF_HD, h * hv + (i + 1) * DIFF_HD)
            q = (q_ref[:, sl] * (DIFF_HD ** -0.5 * LOG2E)).astype(BF16)
            s = _dot_nt(q, k_ref[:, sl].astype(BF16))
            m = jnp.max(s, axis=-1, keepdims=True)
            if has_cache:
                sc = _dot_nt(q, ck_ref[:, sl].astype(BF16))
                m = jnp.maximum(m, jnp.max(sc, axis=-1, keepdims=True))
            e = jnp.exp2(s - m)
            den = jnp.sum(e, axis=-1, keepdims=True)
            o = _dot(e.astype(BF16), vb)
            if has_cache:
                ec = jnp.exp2(sc - m)
                den = den + jnp.sum(ec, axis=-1, keepdims=True)
                o = o + _dot(ec.astype(BF16), cvb)
            parts.append((o, 1.0 / den))
        (o1, r1), (o2, r2) = parts
        o = o1 * r1 - o2 * (lam * r2)
        ms = jnp.mean(o * o, axis=-1, keepdims=True)
        y = o * lax.rsqrt(ms + EPS) * g_ref[...] * (1.0 - lam_init)
        o_ref[:, vs] = y.astype(o_ref.dtype)


def _lam_init(layer):
    return 0.8 - 0.6 * math.exp(-0.3 * layer)


def _diff_ctx(zp, zn, lam_p, subln_g, layer):
    def spec(cb):
        return pl.BlockSpec((SEQ, CB), lambda b: (b, cb))

    return pl.pallas_call(
        functools.partial(_diff_kernel, False, _lam_init(layer)),
        grid=(BATCH,),
        in_specs=[spec(Q_DQ), spec(Q_DK), spec(P_DV),
                  pl.BlockSpec((4, DIFF_HD), lambda b: (0, 0)),
                  pl.BlockSpec((1, 2 * DIFF_HD), lambda b: (0, 0))],
        out_specs=pl.BlockSpec((SEQ, BRANCH_W), lambda b: (b, 0)),
        out_shape=jax.ShapeDtypeStruct((N_CTX, BRANCH_W), BF16),
        compiler_params=_params("parallel"),
        name="diff_ctx",
    )(zn, zn, zp, lam_p, subln_g.reshape(1, -1))


T_DQ = 256


def _diff_lat(zp, zn, cache_k, cache_v, lam_p, subln_g, layer):
    nq = DEC_SEQ // T_DQ
    q_blk0 = N_CTX // T_DQ
    kv_blk0 = N_CTX // DEC_SEQ
    cache_spec = pl.BlockSpec((None, None, PAST_LEN, BRANCH_W), lambda b, t: (b, layer, 0, 0))
    return pl.pallas_call(
        functools.partial(_diff_kernel, True, _lam_init(layer)),
        grid=(DEC_BATCH, nq),
        in_specs=[
            pl.BlockSpec((T_DQ, CB), lambda b, t: (q_blk0 + b * nq + t, Q_DQ)),
            pl.BlockSpec((DEC_SEQ, CB), lambda b, t: (kv_blk0 + b, Q_DK)),
            pl.BlockSpec((DEC_SEQ, CB), lambda b, t: (kv_blk0 + b, P_DV)),
            cache_spec, cache_spec,
            pl.BlockSpec((4, DIFF_HD), lambda b, t: (0, 0)),
            pl.BlockSpec((1, 2 * DIFF_HD), lambda b, t: (0, 0)),
        ],
        out_specs=pl.BlockSpec((T_DQ, BRANCH_W), lambda b, t: (b * nq + t, 0)),
        out_shape=jax.ShapeDtypeStruct((N_LAT, BRANCH_W), BF16),
        compiler_params=_params("parallel", "arbitrary"),
        name="diff_lat",
    )(zn, zn, zp, cache_k, cache_v, lam_p, subln_g.reshape(1, -1))


def _merge_kernel(yp_ref, ync_ref, ynl_ref, yc_ref, ydc_ref, ydl_ref, gate_ref, x_ref, mod_ref, g2_ref,
                  wb_ref, wo_ref, wq_ref, xo_ref, h2_ref, q_ref):
    is_ctx = pl.program_id(0) < N_CTX // T_MERGE
    y_nat = jnp.where(is_ctx, ync_ref[...], ynl_ref[...])
    y_diff = jnp.where(is_ctx, ydc_ref[...], ydl_ref[...])
    merged = None
    for br, y in enumerate((yp_ref[...], y_nat, yc_ref[...], y_diff)):
        t = gate_ref[:, br * D_MODEL:(br + 1) * D_MODEL] * _dot(y, wb_ref[br])
        merged = t if merged is None else merged + t
    out = _dot(merged.astype(BF16), wo_ref[...])
    x = x_ref[...] + mod_ref[2:3, :] * out
    xo_ref[...] = x
    ms = jnp.mean(x * x, axis=-1, keepdims=True)
    h = x * lax.rsqrt(ms + EPS) * g2_ref[...] * (1.0 + mod_ref[4:5, :]) + mod_ref[3:4, :]
    hb = h.astype(BF16)
    h2_ref[...] = hb
    q_ref[...] = _dot(hb, wq_ref[...])


def _merge(y_pool, y_nat_ctx, y_nat_lat, y_conv, y_diff_ctx, y_diff_lat, gates, x, mod_l, norm2_g, wb, wo, wq):
    nct = N_CTX // T_MERGE
    yspec = pl.BlockSpec((T_MERGE, BRANCH_W), lambda i: (i, 0))
    cspec = pl.BlockSpec((T_MERGE, BRANCH_W), lambda i: (jnp.minimum(i, nct - 1), 0))
    lspec = pl.BlockSpec((T_MERGE, BRANCH_W), lambda i: (jnp.maximum(i - nct, 0), 0))
    qcols = wq.shape[1]
    return pl.pallas_call(
        _merge_kernel,
        grid=(N_TOK // T_MERGE,),
        in_specs=[yspec, cspec, lspec, yspec, cspec, lspec,
                  pl.BlockSpec((T_MERGE, GATE_COLS), lambda i: (i, 0)),
                  pl.BlockSpec((T_MERGE, D_MODEL), lambda i: (i, 0)),
                  pl.BlockSpec((None, 6, D_MODEL), lambda i: (_mod_row(i, T_MERGE), 0, 0)),
                  pl.BlockSpec((1, D_MODEL), lambda i: (0, 0)),
                  pl.BlockSpec((4, BRANCH_W, D_MODEL), lambda i: (0, 0, 0)),
                  pl.BlockSpec((D_MODEL, D_MODEL), lambda i: (0, 0)),
                  pl.BlockSpec((D_MODEL, qcols), lambda i: (0, 0))],
        out_specs=[pl.BlockSpec((T_MERGE, D_MODEL), lambda i: (i, 0)),
                   pl.BlockSpec((T_MERGE, D_MODEL), lambda i: (i, 0)),
                   pl.BlockSpec((T_MERGE, qcols), lambda i: (i, 0))],
        out_shape=[jax.ShapeDtypeStruct((N_TOK, D_MODEL), F32),
                   jax.ShapeDtypeStruct((N_TOK, D_MODEL), BF16),
                   jax.ShapeDtypeStruct((N_TOK, qcols), F32)],
        compiler_params=_params("parallel"),
        name="merge",
    )(y_pool, y_nat_ctx, y_nat_lat, y_conv, y_diff_ctx, y_diff_lat, gates, x, mod_l, norm2_g, wb, wo, wq)


NO_RANK = float(PEER_TOPK)


def _top_values_ranks(s, n, want_rank):
    vid = lax.broadcasted_iota(jnp.int32, (n, s.shape[1]), 0)
    vals = jnp.full((n, s.shape[1]), -jnp.inf, F32)
    rank = jnp.full(s.shape, NO_RANK, F32) if want_rank else None
    for k in range(n):
        m = jnp.max(s, axis=0, keepdims=True)
        hit = s == m
        vals = jnp.where(vid == k, m, vals)
        if want_rank:
            rank = jnp.where(hit, float(k), rank)
        s = jnp.where(hit, -jnp.inf, s)
    return vals, rank


N_CAND = 16 + 7 * 8 + 8


def _peer_select_kernel(q_ref, sk_ref, r2_ref, e2_ref, brow_ref, crow_ref):
    q = q_ref[...].astype(BF16)
    half = PEER_NKEYS
    s1_all = _dot_nt(sk_ref[0].astype(BF16), q[:, :half])
    s2_all = _dot_nt(sk_ref[1].astype(BF16), q[:, half:])
    cid = lax.broadcasted_iota(jnp.int32, (N_CAND, LANE), 0)
    rid8 = lax.broadcasted_iota(jnp.int32, (8, LANE), 0)
    for c in range(T_SEL // LANE):
        ls = slice(c * LANE, (c + 1) * LANE)
        s1 = s1_all[:, ls]
        s2 = s2_all[:, ls]
        v1, _ = _top_values_ranks(s1, PEER_TOPK, False)
        v2, rank2 = _top_values_ranks(s2, PEER_TOPK, True)
        cand = jnp.concatenate([v1[0:1] + v2] + [v1[a:a + 1] + v2[0:8] for a in range(1, 8)]
                               + [v1[8:16] + v2[0:1]], axis=0)
        top0 = v1[0:1] + v2[0:1]
        chosen = jnp.zeros((N_CAND, LANE), F32)
        zsum = jnp.zeros((1, LANE), F32)
        for k in range(PEER_TOPK):
            m = jnp.max(cand, axis=0, keepdims=True)
            first = jnp.min(jnp.where(cand == m, cid, N_CAND), axis=0, keepdims=True)
            hit = cid == first
            chosen = jnp.where(hit, 1.0, chosen)
            zsum = zsum + jnp.exp(m - top0)
            cand = jnp.where(hit, -jnp.inf, cand)
        cnt_lo = jnp.zeros((8, LANE), F32)
        cnt_lo = jnp.where(rid8 == 0, jnp.sum(chosen[0:16], axis=0, keepdims=True), cnt_lo)
        for a in range(1, 8):
            cnt_lo = jnp.where(rid8 == a, jnp.sum(chosen[8 + 8 * a:16 + 8 * a], axis=0, keepdims=True), cnt_lo)
        cnt = jnp.concatenate([cnt_lo, chosen[N_CAND - 8:N_CAND]], axis=0)
        brow = jnp.zeros((PEER_NKEYS, LANE), F32)
        for a in range(PEER_TOPK):
            brow = jnp.where(s1 == v1[a:a + 1], cnt[a:a + 1], brow)
        brow_ref[:, ls] = brow
        crow_ref[:, ls] = jnp.exp(s1 - v1[0:1]) / zsum
        r2_ref[:, ls] = rank2.astype(BF16)
        e2_ref[:, ls] = jnp.exp(s2 - v2[0:1]).astype(BF16)


def _peer_select(qry, sub_keys):
    nt = N_TOK // T_SEL
    kspec = pl.BlockSpec((None, PEER_NKEYS, T_SEL), lambda i, h: (h, 0, i))

    def kshape(dt):
        return jax.ShapeDtypeStruct((PEER_HEADS, PEER_NKEYS, N_TOK), dt)

    return pl.pallas_call(
        _peer_select_kernel,
        grid=(nt, PEER_HEADS),
        in_specs=[pl.BlockSpec((T_SEL, 2 * PEER_NKEYS), lambda i, h: (i, h)),
                  pl.BlockSpec((None, 2, PEER_NKEYS, PEER_NKEYS), lambda i, h: (h, 0, 0, 0))],
        out_specs=[kspec, kspec, kspec, kspec],
        out_shape=[kshape(BF16), kshape(BF16), kshape(F32), kshape(F32)],
        compiler_params=_params("parallel", "arbitrary"),
        name="peer_select",
    )(qry, sub_keys)


E_PAIR = 2 * PEER_NKEYS
BF16_ROWS = 16


def _row_bf16(row):
    return jnp.broadcast_to(row, (BF16_ROWS, LANE)).astype(BF16)


def _gelu(x):
    return 0.5 * x * (1.0 + lax.erf(x * (2.0 ** -0.5)))


def _peer_dense_kernel(h_ref, u_ref, v_ref, brow_ref, crow_ref, r2_ref, e2_ref, x_ref, mod_ref,
                       o_ref, acc_ref, a_scr, p_scr):
    c = pl.program_id(1)

    @pl.when(c == 0)
    def _():
        acc_ref[...] = jnp.zeros_like(acc_ref)

    hb = h_ref[...]
    zero = jnp.zeros((BF16_ROWS, LANE), BF16)
    npair = E_CHUNK // E_PAIR
    for j in range(npair + 1):
        slot = j % 2
        if j < npair:
            a_scr[slot] = _dot_nt(u_ref[j * E_PAIR:(j + 1) * E_PAIR, :].astype(BF16), hb)
        if j > 0:
            acc_ref[...] += _dot_tn(p_scr[1 - slot], v_ref[(j - 1) * E_PAIR:j * E_PAIR, :].astype(BF16))
        if j == npair:
            break
        for half in range(2):
            n1l = 2 * j + half
            for tc in range(T_PEER // LANE):
                ls = slice(tc * LANE, (tc + 1) * LANE)
                b16 = [_row_bf16(brow_ref[h, n1l:n1l + 1, ls]) for h in range(PEER_HEADS)]
                c16 = [_row_bf16(crow_ref[h, n1l:n1l + 1, ls]) for h in range(PEER_HEADS)]
                e0 = half * PEER_NKEYS
                act = _gelu(a_scr[slot, e0:e0 + PEER_NKEYS, ls]).astype(BF16)
                for rg in range(PEER_NKEYS // BF16_ROWS):
                    rs = slice(rg * BF16_ROWS, (rg + 1) * BF16_ROWS)
                    g = None
                    for h in range(PEER_HEADS):
                        t = jnp.where(r2_ref[h, rs, ls] < b16[h], e2_ref[h, rs, ls], zero) * c16[h]
                        g = t if g is None else g + t
                    p_scr[slot, e0 + rg * BF16_ROWS:e0 + (rg + 1) * BF16_ROWS, ls] = g * act[rs]

    @pl.when(c == pl.num_programs(1) - 1)
    def _():
        o_ref[...] = x_ref[...] + mod_ref[5:6, :] * acc_ref[...]


def _peer_dense(h2, peer_u, peer_v, r2, e2, brow, crow, x, mod_l):
    nt = N_TOK // T_PEER
    nc = PEER_N // E_CHUNK
    n1c = E_CHUNK // PEER_NKEYS
    rowspec = pl.BlockSpec((PEER_HEADS, n1c, T_PEER), lambda i, c: (0, c, i))
    fullspec = pl.BlockSpec((PEER_HEADS, PEER_NKEYS, T_PEER), lambda i, c: (0, 0, i))
    return pl.pallas_call(
        _peer_dense_kernel,
        grid=(nt, nc),
        in_specs=[pl.BlockSpec((T_PEER, D_MODEL), lambda i, c: (i, 0)),
                  pl.BlockSpec((E_CHUNK, D_MODEL), lambda i, c: (c, 0)),
                  pl.BlockSpec((E_CHUNK, D_MODEL), lambda i, c: (c, 0)),
                  rowspec, rowspec, fullspec, fullspec,
                  pl.BlockSpec((T_PEER, D_MODEL), lambda i, c: (i, 0)),
                  pl.BlockSpec((None, 6, D_MODEL), lambda i, c: (_mod_row(i, T_PEER), 0, 0))],
        out_specs=pl.BlockSpec((T_PEER, D_MODEL), lambda i, c: (i, 0)),
        out_shape=jax.ShapeDtypeStruct((N_TOK, D_MODEL), F32),
        scratch_shapes=[pltpu.VMEM((T_PEER, D_MODEL), F32),
                        pltpu.VMEM((2, E_PAIR, T_PEER), F32),
                        pltpu.VMEM((2, E_PAIR, T_PEER), BF16)],
        compiler_params=_params("parallel", "arbitrary"),
        name="peer_dense",
    )(h2, peer_u, peer_v, brow, crow, r2, e2, x, mod_l)


def kernel(x_prompt, x_sample, cache_nat_k, cache_nat_v, cache_diff_k, cache_diff_v, c, c_ctx, w_ada, b_ada, norm1_g, norm2_g, w_in, pool_w, pool_scale, nat_q_g, nat_k_g, nat_rel_bias, conv_w, conv_b, conv_ln_g, conv_ln_b, diff_q_g, diff_k_g, diff_lambda_p, diff_subln_g, w_branch, w_gate, b_gate, w_out, peer_w_query, peer_sub_keys, peer_u, peer_v):
    x = jnp.concatenate([x_prompt.reshape(N_CTX, D_MODEL), x_sample.reshape(N_LAT, D_MODEL)], axis=0)
    cvec = jnp.concatenate([c_ctx[None, :], c, jnp.zeros((8 - 1 - DEC_BATCH, D_MODEL), F32)], axis=0)
    mod = _modulation(cvec, w_ada, b_ada).reshape(DEPTH, 8, 6, D_MODEL)

    gid = jnp.arange(CB) // NAT_HD
    gmat = (gid[:, None] == gid[None, :]).astype(BF16)
    rope = _rope_tables(T_PRE)
    ck_n = cache_nat_k.reshape(DEC_BATCH, DEPTH, PAST_LEN, BRANCH_W)
    cv_n = cache_nat_v.reshape(DEC_BATCH, DEPTH, PAST_LEN, BRANCH_W)
    ck_d = cache_diff_k.reshape(DEC_BATCH, DEPTH, PAST_LEN, BRANCH_W)
    cv_d = cache_diff_v.reshape(DEC_BATCH, DEPTH, PAST_LEN, BRANCH_W)

    states = []
    for l in range(DEPTH):
        gg = jnp.stack([jnp.tile(nat_q_g[l], NAT_HEADS), jnp.tile(nat_k_g[l], NAT_HEADS),
                        jnp.tile(diff_q_g[l], 2 * DIFF_HEADS), jnp.tile(diff_k_g[l], 2 * DIFF_HEADS)])
        gates, zp, zn = _pre(x, mod[l], norm1_g[l].reshape(1, -1), w_gate[l].astype(BF16),
                             b_gate[l].reshape(1, -1), w_in[l].astype(BF16), gg.reshape(len(NORM_ZB), 1, CB),
                             gmat, rope)

        y_pool, y_conv = _seq_mixers(zp, pool_w[l], pool_scale[l], conv_w[l], conv_b[l],
                                     conv_ln_g[l], conv_ln_b[l])
        y_nat_ctx = _nat_ctx(zp, zn)
        y_nat_lat = _nat_lat(zp, zn, ck_n, cv_n, *_nat_bias_table(nat_rel_bias[l]), l)
        y_diff_ctx = _diff_ctx(zp, zn, diff_lambda_p[l], diff_subln_g[l], l)
        y_diff_lat = _diff_lat(zp, zn, ck_d, cv_d, diff_lambda_p[l], diff_subln_g[l], l)

        x, h2, qry = _merge(y_pool, y_nat_ctx, y_nat_lat, y_conv, y_diff_ctx, y_diff_lat, gates, x, mod[l],
                            norm2_g[l].reshape(1, -1), w_branch[l].astype(BF16), w_out[l].astype(BF16),
                            peer_w_query[l].astype(BF16))
        r2, e2, brow, crow = _peer_select(qry, peer_sub_keys[l])
        x = _peer_dense(h2, peer_u[l], peer_v[l], r2, e2, brow, crow, x, mod[l])

        states.append([zn[:N_CTX, Q_NK * CB:(Q_NK + 1) * CB], zp[:N_CTX, P_NV * CB:(P_NV + 1) * CB],
                       zn[:N_CTX, Q_DK * CB:(Q_DK + 1) * CB], zp[:N_CTX, P_DV * CB:(P_DV + 1) * CB]])

    def stack(idx, shape):
        return jnp.stack([states[l][idx].reshape((BATCH, SEQ) + shape) for l in range(DEPTH)], axis=1)

    return (x[:N_CTX].reshape(BATCH, SEQ, D_MODEL),
            x[N_CTX:].reshape(DEC_BATCH, DEC_SEQ, D_MODEL),
            stack(0, (NAT_HEADS, NAT_HD)),
            stack(1, (NAT_HEADS, NAT_HD)),
            stack(2, (DIFF_HEADS, 2, DIFF_HD)),
            stack(3, (DIFF_HEADS, 2 * DIFF_HD)))
```

```python
import functools
import math

import jax
import jax.numpy as jnp
from jax import lax
from jax.experimental import pallas as pl
from jax.experimental.pallas import tpu as pltpu

F32 = jnp.float32
BF16 = jnp.bfloat16

D_MODEL = 1024
BATCH = 16
SEQ = 256
DEPTH = 2
DEC_BATCH = 4
DEC_SEQ = 2048
PAST_LEN = 256
GRID_W = 64
BRANCH_W = 512
POOL_WINDOWS = (2, 4, 8, 16)
POOL_GROUP = 128
NAT_HEADS = 8
NAT_HD = 64
NAT_WIN_R = 8
NAT_WIN_C = 16
CONV_W = 31
DIFF_HEADS = 4
DIFF_HD = 64
PEER_HEADS = 8
PEER_NKEYS = 128
PEER_N = PEER_NKEYS * PEER_NKEYS
PEER_TOPK = 16
ROPE_BASE = 10000.0
EPS = 1e-6
NEG_INF = -1e30
LOG2E = math.log2(math.e)

N_CTX = BATCH * SEQ
N_LAT = DEC_BATCH * DEC_SEQ
N_TOK = N_CTX + N_LAT
GRID_ROWS = DEC_SEQ // GRID_W

CB = 512
ZB_POOL, ZB_NQ, ZB_NK, ZB_NV, ZB_CA, ZB_CGATE, ZB_DQ, ZB_DK, ZB_DV = range(9)

V7X_VMEM_LIMIT = 52 * 1024 * 1024

T_PRE = 1024
T_SEQ = 256
HALO = 16
T_MERGE = 512
T_SEL = 512
T_PEER = 512
E_CHUNK = 1024
LANE = 128


def _sigmoid(x):
    return 1.0 / (1.0 + jnp.exp(-x))


def _dot(a, b):
    return jnp.dot(a, b, preferred_element_type=F32)


def _dot_nt(a, b):
    return lax.dot_general(a, b, (((1,), (1,)), ((), ())), preferred_element_type=F32)


def _dot_tn(a, b):
    return lax.dot_general(a, b, (((0,), (0,)), ((), ())), preferred_element_type=F32)


def _split_bf16(a):
    hi = a.astype(BF16)
    lo = (a - hi.astype(F32)).astype(BF16)
    return hi, lo


def _params(*sem):
    return pltpu.CompilerParams(dimension_semantics=sem, vmem_limit_bytes=V7X_VMEM_LIMIT)


def _mod_row(i, tile):
    nct = N_CTX // tile
    per = DEC_SEQ // tile
    return jnp.where(i < nct, 0, 1 + (i - nct) // per)


def _mod_kernel(c_ref, w_ref, b_ref, o_ref):
    c = c_ref[...]
    a = c * _sigmoid(c)
    w = w_ref[0]
    a_hi, a_lo = _split_bf16(a)
    w_hi, w_lo = _split_bf16(w)
    o_ref[0] = _dot(a_hi, w_hi) + _dot(a_lo, w_hi) + _dot(a_hi, w_lo) + b_ref[0]


def _modulation(cvec, w_ada, b_ada):
    tn = 1024
    return pl.pallas_call(
        _mod_kernel,
        grid=(DEPTH, 6 * D_MODEL // tn),
        in_specs=[
            pl.BlockSpec((8, D_MODEL), lambda l, j: (0, 0)),
            pl.BlockSpec((1, D_MODEL, tn), lambda l, j: (l, 0, j)),
            pl.BlockSpec((1, 1, tn), lambda l, j: (l, 0, j)),
        ],
        out_specs=pl.BlockSpec((1, 8, tn), lambda l, j: (l, 0, j)),
        out_shape=jax.ShapeDtypeStruct((DEPTH, 8, 6 * D_MODEL), F32),
        compiler_params=_params("parallel", "parallel"),
        name="modulation",
    )(cvec, w_ada, b_ada.reshape(DEPTH, 1, 6 * D_MODEL))


def _tile4(t):
    return jnp.concatenate([t, t, t, t], axis=1)


def _hnorm_kernel(x_ref, mod_ref, g_ref, h_ref):
    x = x_ref[...]
    ms = jnp.mean(x * x, axis=-1, keepdims=True)
    y = x * lax.rsqrt(ms + EPS) * g_ref[...]
    h_ref[...] = (y * (1.0 + mod_ref[1:2, :]) + mod_ref[0:1, :]).astype(h_ref.dtype)


def _proj_plain_kernel(h_ref, w_ref, o_ref):
    o_ref[...] = _dot(h_ref[...], w_ref[...])


def _proj_norm_kernel(h_ref, w_ref, gg_ref, gmat_ref, rc_ref, rp_ref, rm_ref, o_ref):
    acc = _dot(h_ref[...], w_ref[...])
    hi, lo = _split_bf16(acc * acc)
    ss = _dot(hi, gmat_ref[...]) + _dot(lo, gmat_ref[...])
    y = acc * lax.rsqrt(ss * (1.0 / NAT_HD) + EPS) * gg_ref[0]
    o_ref[...] = (y * _tile4(rc_ref[...])
                  + pltpu.roll(y, 16, 1) * _tile4(rp_ref[...])
                  + pltpu.roll(y, CB - 16, 1) * _tile4(rm_ref[...]))


def _rope_tables(tile):
    quarter = DIFF_HD // 4
    t = jnp.arange(DEC_SEQ)
    lane = jnp.arange(LANE)
    d = lane % DIFF_HD
    freqs = ROPE_BASE ** (-jnp.arange(quarter, dtype=F32) / quarter)
    pos = jnp.where(d[None, :] < DIFF_HD // 2, (t // GRID_W)[:, None], (t % GRID_W)[:, None]).astype(F32)
    ang = pos * freqs[d % quarter][None, :]
    cos = jnp.cos(ang)
    sin = jnp.sin(ang)
    second = (d % (2 * quarter)) >= quarter
    s_plus = jnp.where(second[None, :], sin, 0.0)
    s_minus = jnp.where(second[None, :], 0.0, -sin)
    ones = jnp.ones((tile, LANE), F32)
    zeros = jnp.zeros((tile, LANE), F32)
    return (jnp.concatenate([cos, ones], 0), jnp.concatenate([s_plus, zeros], 0),
            jnp.concatenate([s_minus, zeros], 0))


PLAIN_ZB = (ZB_POOL, ZB_NV, ZB_CA, ZB_CGATE, ZB_DV)
NORM_ZB = (ZB_NQ, ZB_NK, ZB_DQ, ZB_DK)
P_POOL, P_NV, P_CA, P_CGATE, P_DV = range(5)
Q_NQ, Q_NK, Q_DQ, Q_DK = range(4)


def _pick(j, values):
    out = values[0]
    for n, v in enumerate(values[1:], start=1):
        out = jnp.where(j == n, v, out)
    return out


def _pre(x, mod_l, norm_g, w_in, gg, gmat, rope):
    nct = N_CTX // T_PRE
    per = DEC_SEQ // T_PRE
    ident_blk = DEC_SEQ // T_PRE
    nt = N_TOK // T_PRE
    h = pl.pallas_call(
        _hnorm_kernel,
        grid=(nt,),
        in_specs=[pl.BlockSpec((T_PRE, D_MODEL), lambda i: (i, 0)),
                  pl.BlockSpec((None, 6, D_MODEL), lambda i: (_mod_row(i, T_PRE), 0, 0)),
                  pl.BlockSpec((1, D_MODEL), lambda i: (0, 0))],
        out_specs=pl.BlockSpec((T_PRE, D_MODEL), lambda i: (i, 0)),
        out_shape=jax.ShapeDtypeStruct((N_TOK, D_MODEL), BF16),
        compiler_params=_params("parallel"),
        name="hnorm",
    )(x, mod_l, norm_g)

    hspec = pl.BlockSpec((T_PRE, D_MODEL), lambda i, j: (i, 0))
    ospec = pl.BlockSpec((T_PRE, CB), lambda i, j: (i, j))

    zp = pl.pallas_call(
        _proj_plain_kernel,
        grid=(nt, len(PLAIN_ZB)),
        in_specs=[hspec, pl.BlockSpec((D_MODEL, CB), lambda i, j: (0, _pick(j, PLAIN_ZB)))],
        out_specs=ospec,
        out_shape=jax.ShapeDtypeStruct((N_TOK, len(PLAIN_ZB) * CB), F32),
        compiler_params=_params("parallel", "arbitrary"),
        name="proj_plain",
    )(h, w_in)

    def rope_idx(i, j):
        use = ((j == Q_DQ) | (j == Q_DK)) & (i >= nct)
        return jnp.where(use, (i - nct) % per, ident_blk), 0

    rope_spec = pl.BlockSpec((T_PRE, LANE), rope_idx)
    zn = pl.pallas_call(
        _proj_norm_kernel,
        grid=(nt, len(NORM_ZB)),
        in_specs=[hspec, pl.BlockSpec((D_MODEL, CB), lambda i, j: (0, _pick(j, NORM_ZB))),
                  pl.BlockSpec((1, 1, CB), lambda i, j: (j, 0, 0)),
                  pl.BlockSpec((CB, CB), lambda i, j: (0, 0)),
                  rope_spec, rope_spec, rope_spec],
        out_specs=ospec,
        out_shape=jax.ShapeDtypeStruct((N_TOK, len(NORM_ZB) * CB), F32),
        compiler_params=_params("parallel", "arbitrary"),
        name="proj_norm",
    )(h, w_in, gg, gmat, *rope)

    return h, zp, zn


def _seq_kernel(zp_ref, zp_prev, zp_next, za_ref, za_prev, za_next, zb_ref, zb_prev, zb_next,
                pw_ref, ps_ref, cw_ref, cb_ref, lg_ref, lb_ref,
                yp_ref, yc_ref, pbuf, ubuf, ybuf):
    i = pl.program_id(0)
    nct = N_CTX // T_SEQ
    per = DEC_SEQ // T_SEQ
    is_lat = i >= nct
    kk = (i - nct) % per
    has_prev = is_lat & (kk != 0)
    has_next = is_lat & (kk != per - 1)
    pos0 = jnp.where(is_lat, kk * T_SEQ, 0)
    seq_len = jnp.where(is_lat, DEC_SEQ, SEQ)
    fp = jnp.where(has_prev, 1.0, 0.0).astype(F32)
    fn = jnp.where(has_next, 1.0, 0.0).astype(F32)

    pbuf[0:HALO, :] = zp_prev[...] * fp
    pbuf[HALO:HALO + T_SEQ, :] = zp_ref[...]
    pbuf[HALO + T_SEQ:, :] = zp_next[...] * fn
    ubuf[0:HALO, :] = za_prev[...] * _sigmoid(zb_prev[...]) * fp
    ubuf[HALO:HALO + T_SEQ, :] = za_ref[...] * _sigmoid(zb_ref[...])
    ubuf[HALO + T_SEQ:, :] = za_next[...] * _sigmoid(zb_next[...]) * fn

    pos = pos0 + lax.broadcasted_iota(jnp.int32, (T_SEQ, LANE), 0)
    for gi, w in enumerate(POOL_WINDOWS):
        ls = slice(gi * LANE, (gi + 1) * LANE)
        acc = None
        for s in range(-(w // 2), w - w // 2):
            v = pbuf[HALO + s:HALO + s + T_SEQ, ls]
            acc = v if acc is None else acc + v
        lo = jnp.maximum(pos - w // 2, 0)
        hi = jnp.minimum(pos + (w - w // 2), seq_len)
        mean = acc / (hi - lo).astype(F32)
        d = mean - pbuf[HALO:HALO + T_SEQ, ls]
        y = _dot(d.astype(BF16), pw_ref[gi].astype(BF16)) * ps_ref[:, ls]
        yp_ref[:, ls] = y.astype(yp_ref.dtype)

    base = HALO - CONV_W // 2
    for c in range(BRANCH_W // LANE):
        ls = slice(c * LANE, (c + 1) * LANE)
        acc = None
        for k in range(CONV_W):
            v = ubuf[base + k:base + k + T_SEQ, ls] * cw_ref[k:k + 1, ls]
            acc = v if acc is None else acc + v
        ybuf[:, ls] = acc + cb_ref[:, ls]
    y = ybuf[...]
    mu = jnp.mean(y, axis=-1, keepdims=True)
    yc = y - mu
    var = jnp.mean(yc * yc, axis=-1, keepdims=True)
    yn = yc * lax.rsqrt(var + EPS) * lg_ref[...] + lb_ref[...]
    yc_ref[...] = (yn * _sigmoid(yn)).astype(yc_ref.dtype)


def _seq_mixers(zp, pool_w, pool_scale, conv_w, conv_b, ln_g, ln_b):
    nt = N_TOK // T_SEQ
    hb = T_SEQ // HALO
    n_hblk = N_TOK // HALO

    def specs(cb):
        return [
            pl.BlockSpec((T_SEQ, CB), lambda i: (i, cb)),
            pl.BlockSpec((HALO, CB), lambda i: (jnp.maximum(i * hb - 1, 0), cb)),
            pl.BlockSpec((HALO, CB), lambda i: (jnp.minimum((i + 1) * hb, n_hblk - 1), cb)),
        ]

    def full(shape):
        return pl.BlockSpec(shape, lambda i: (0,) * len(shape))

    cw = jnp.concatenate([conv_w, jnp.zeros((1, BRANCH_W), F32)], axis=0)
    out_spec = pl.BlockSpec((T_SEQ, BRANCH_W), lambda i: (i, 0))
    return pl.pallas_call(
        _seq_kernel,
        grid=(nt,),
        in_specs=(specs(P_POOL) + specs(P_CA) + specs(P_CGATE)
                  + [full((4, POOL_GROUP, POOL_GROUP)), full((1, BRANCH_W)), full((CONV_W + 1, BRANCH_W)),
                     full((1, BRANCH_W)), full((1, BRANCH_W)), full((1, BRANCH_W))]),
        out_specs=[out_spec, out_spec],
        out_shape=[jax.ShapeDtypeStruct((N_TOK, BRANCH_W), BF16)] * 2,
        scratch_shapes=[pltpu.VMEM((T_SEQ + 2 * HALO, BRANCH_W), F32),
                        pltpu.VMEM((T_SEQ + 2 * HALO, BRANCH_W), F32),
                        pltpu.VMEM((T_SEQ, BRANCH_W), F32)],
        compiler_params=_params("parallel"),
        name="pool_conv",
    )(zp, zp, zp, zp, zp, zp, zp, zp, zp, pool_w, pool_scale.reshape(1, -1), cw,
      conv_b.reshape(1, -1), ln_g.reshape(1, -1), ln_b.reshape(1, -1))


def _nat_ctx_kernel(q_ref, k_ref, v_ref, o_ref):
    for h in range(NAT_HEADS):
        sl = slice(h * NAT_HD, (h + 1) * NAT_HD)
        q = (q_ref[:, sl] * (NAT_HD ** -0.5 * LOG2E)).astype(BF16)
        s = _dot_nt(q, k_ref[:, sl].astype(BF16))
        m = jnp.max(s, axis=-1, keepdims=True)
        e = jnp.exp2(s - m)
        den = jnp.sum(e, axis=-1, keepdims=True)
        o = _dot(e.astype(BF16), v_ref[:, sl].astype(BF16)) / den
        o_ref[:, sl] = o.astype(o_ref.dtype)


def _nat_ctx(zp, zn):
    def spec(cb):
        return pl.BlockSpec((SEQ, CB), lambda b: (b, cb))

    return pl.pallas_call(
        _nat_ctx_kernel,
        grid=(BATCH,),
        in_specs=[spec(Q_NQ), spec(Q_NK), spec(P_NV)],
        out_specs=pl.BlockSpec((SEQ, BRANCH_W), lambda b: (b, 0)),
        out_shape=jax.ShapeDtypeStruct((N_CTX, BRANCH_W), BF16),
        compiler_params=_params("parallel"),
        name="nat_ctx",
    )(zn, zn, zp)


NAT_QROWS = 4
NAT_KROWS = 12


def _nat_row0(r):
    return jnp.clip(r - NAT_WIN_R // 2, 0, GRID_ROWS - NAT_WIN_R)


def _nat_lat_kernel(q_ref, k_ref, v_ref, ck_ref, cv_ref, bias_ref, o_ref):
    g = pl.program_id(1)
    w0 = jnp.minimum(_nat_row0(g * NAT_QROWS), GRID_ROWS - NAT_KROWS)
    start = pl.multiple_of(w0 * GRID_W, GRID_W)
    nwin = NAT_KROWS * GRID_W
    for h in range(NAT_HEADS):
        sl = slice(h * NAT_HD, (h + 1) * NAT_HD)
        q = (q_ref[:, sl] * (NAT_HD ** -0.5 * LOG2E)).astype(BF16)
        kw = k_ref[pl.ds(start, nwin), sl].astype(BF16)
        vw = v_ref[pl.ds(start, nwin), sl].astype(BF16)
        s_loc = _dot_nt(q, kw) + bias_ref[h]
        s_ctx = _dot_nt(q, ck_ref[:, sl].astype(BF16))
        m = jnp.maximum(jnp.max(s_loc, axis=-1, keepdims=True), jnp.max(s_ctx, axis=-1, keepdims=True))
        e_loc = jnp.exp2(s_loc - m)
        e_ctx = jnp.exp2(s_ctx - m)
        den = jnp.sum(e_loc, axis=-1, keepdims=True) + jnp.sum(e_ctx, axis=-1, keepdims=True)
        o = (_dot(e_loc.astype(BF16), vw) + _dot(e_ctx.astype(BF16), cv_ref[:, sl].astype(BF16))) / den
        o_ref[:, sl] = o.astype(o_ref.dtype)


def _nat_bias_kernel(rb_ref, oh_ref, o_ref):
    x = rb_ref[...]
    hi = x.astype(BF16)
    r1 = x - hi.astype(F32)
    mid = r1.astype(BF16)
    lo = (r1 - mid.astype(F32)).astype(BF16)
    oh = oh_ref[...]
    o_ref[...] = _dot(hi, oh) + _dot(mid, oh) + _dot(lo, oh)


def _nat_bias_table(rel_bias):
    ndr = 2 * NAT_WIN_R - 1
    ndc = 2 * NAT_WIN_C - 1
    q = jnp.arange(GRID_W)
    kc = jnp.arange(GRID_W)
    dcol = jnp.clip(kc[None, :] - q[:, None] + NAT_WIN_C - 1, 0, ndc - 1)
    wstart = jnp.clip(q - NAT_WIN_C // 2, 0, GRID_W - NAT_WIN_C)
    valid = (kc[None, :] >= wstart[:, None]) & (kc[None, :] < wstart[:, None] + NAT_WIN_C)
    d = jnp.arange(LANE)
    onehot = jnp.where(d[:, None, None] == ndc, jnp.logical_not(valid)[None],
                       (d[:, None, None] == dcol[None]) & valid[None])
    onehot = onehot.reshape(LANE, GRID_W * GRID_W).astype(BF16)
    nrow = NAT_HEADS * ndr
    rb = jnp.concatenate([rel_bias.reshape(nrow, ndc).astype(F32) * LOG2E, jnp.full((nrow, 1), NEG_INF, F32),
                          jnp.zeros((nrow, LANE - ndc - 1), F32)], axis=1)
    rb = jnp.concatenate([rb, jnp.zeros((LANE - nrow, LANE), F32)], axis=0)
    tcol = pl.pallas_call(
        _nat_bias_kernel,
        out_shape=jax.ShapeDtypeStruct((LANE, GRID_W * GRID_W), F32),
        compiler_params=pltpu.CompilerParams(vmem_limit_bytes=V7X_VMEM_LIMIT),
        name="nat_bias",
    )(rb, onehot)
    tcol = tcol[:nrow].reshape(NAT_HEADS, ndr, GRID_W, GRID_W)
    neg = jnp.full((NAT_HEADS, GRID_W, GRID_W), NEG_INF, F32)
    tables, variant_of_group, seen = [], [], {}
    for g in range(GRID_ROWS // NAT_QROWS):
        r_first = g * NAT_QROWS
        w0 = min(max(r_first - NAT_WIN_R // 2, 0), GRID_ROWS - NAT_WIN_R, GRID_ROWS - NAT_KROWS)
        rows = [(r_first + j - w0, min(max(r_first + j - NAT_WIN_R // 2, 0), GRID_ROWS - NAT_WIN_R) - w0)
                for j in range(NAT_QROWS)]
        key = tuple(rows)
        if key not in seen:
            seen[key] = len(tables)
            blocks = []
            for rq, rw in rows:
                blocks.append(jnp.concatenate(
                    [tcol[:, i - rq + NAT_WIN_R - 1] if rw <= i < rw + NAT_WIN_R else neg
                     for i in range(NAT_KROWS)], axis=-1))
            tables.append(jnp.concatenate(blocks, axis=1))
        variant_of_group.append(seen[key])
    return jnp.stack(tables, axis=0), tuple(variant_of_group)


def _nat_lat(zp, zn, cache_k, cache_v, bias_tab, variant_of_group, layer):
    tq = NAT_QROWS * GRID_W
    ngroups = GRID_ROWS // NAT_QROWS
    q_blk0 = N_CTX // tq
    kv_blk0 = N_CTX // DEC_SEQ
    cache_spec = pl.BlockSpec((None, None, PAST_LEN, BRANCH_W), lambda b, g: (b, layer, 0, 0))
    return pl.pallas_call(
        _nat_lat_kernel,
        grid=(DEC_BATCH, ngroups),
        in_specs=[
            pl.BlockSpec((tq, CB), lambda b, g: (q_blk0 + b * ngroups + g, Q_NQ)),
            pl.BlockSpec((DEC_SEQ, CB), lambda b, g: (kv_blk0 + b, Q_NK)),
            pl.BlockSpec((DEC_SEQ, CB), lambda b, g: (kv_blk0 + b, P_NV)),
            cache_spec, cache_spec,
            pl.BlockSpec((None, NAT_HEADS, tq, NAT_KROWS * GRID_W),
                         lambda b, g: (_pick(g, variant_of_group), 0, 0, 0)),
        ],
        out_specs=pl.BlockSpec((tq, BRANCH_W), lambda b, g: (b * ngroups + g, 0)),
        out_shape=jax.ShapeDtypeStruct((N_LAT, BRANCH_W), BF16),
        compiler_params=_params("parallel", "arbitrary"),
        name="nat_lat",
    )(zn, zn, zp, cache_k, cache_v, bias_tab)


def _diff_kernel(has_cache, lam_init, *refs):
    if has_cache:
        q_ref, k_ref, v_ref, ck_ref, cv_ref, lamp_ref, g_ref, o_ref = refs
    else:
        q_ref, k_ref, v_ref, lamp_ref, g_ref, o_ref = refs
    lp = lamp_ref[...]
    lam = (jnp.exp(jnp.sum(lp[0:1] * lp[1:2], axis=-1, keepdims=True))
           - jnp.exp(jnp.sum(lp[2:3] * lp[3:4], axis=-1, keepdims=True)) + lam_init)
    hv = 2 * DIFF_HD
    for h in range(DIFF_HEADS):
        vs = slice(h * hv, (h + 1) * hv)
        vb = v_ref[:, vs].astype(BF16)
        if has_cache:
            cvb = cv_ref[:, vs].astype(BF16)
        parts = []
        for i in range(2):
            sl = slice(h * hv + i * DIFF_HD, h * hv + (i + 1) * DIFF_HD)
            q = (q_ref[:, sl] * (DIFF_HD ** -0.5 * LOG2E)).astype(BF16)
            s = _dot_nt(q, k_ref[:, sl].astype(BF16))
            m = jnp.max(s, axis=-1, keepdims=True)
            if has_cache:
                sc = _dot_nt(q, ck_ref[:, sl].astype(BF16))
                m = jnp.maximum(m, jnp.max(sc, axis=-1, keepdims=True))
            e = jnp.exp2(s - m)
            den = jnp.sum(e, axis=-1, keepdims=True)
            o = _dot(e.astype(BF16), vb)
            if has_cache:
                ec = jnp.exp2(sc - m)
                den = den + jnp.sum(ec, axis=-1, keepdims=True)
                o = o + _dot(ec.astype(BF16), cvb)
            parts.append((o, 1.0 / den))
        (o1, r1), (o2, r2) = parts
        o = o1 * r1 - o2 * (lam * r2)
        ms = jnp.mean(o * o, axis=-1, keepdims=True)
        y = o * lax.rsqrt(ms + EPS) * g_ref[...] * (1.0 - lam_init)
        o_ref[:, vs] = y.astype(o_ref.dtype)


def _lam_init(layer):
    return 0.8 - 0.6 * math.exp(-0.3 * layer)


def _diff_ctx(zp, zn, lam_p, subln_g, layer):
    def spec(cb):
        return pl.BlockSpec((SEQ, CB), lambda b: (b, cb))

    return pl.pallas_call(
        functools.partial(_diff_kernel, False, _lam_init(layer)),
        grid=(BATCH,),
        in_specs=[spec(Q_DQ), spec(Q_DK), spec(P_DV),
                  pl.BlockSpec((4, DIFF_HD), lambda b: (0, 0)),
                  pl.BlockSpec((1, 2 * DIFF_HD), lambda b: (0, 0))],
        out_specs=pl.BlockSpec((SEQ, BRANCH_W), lambda b: (b, 0)),
        out_shape=jax.ShapeDtypeStruct((N_CTX, BRANCH_W), BF16),
        compiler_params=_params("parallel"),
        name="diff_ctx",
    )(zn, zn, zp, lam_p, subln_g.reshape(1, -1))


T_DQ = 256


def _diff_lat(zp, zn, cache_k, cache_v, lam_p, subln_g, layer):
    nq = DEC_SEQ // T_DQ
    q_blk0 = N_CTX // T_DQ
    kv_blk0 = N_CTX // DEC_SEQ
    cache_spec = pl.BlockSpec((None, None, PAST_LEN, BRANCH_W), lambda b, t: (b, layer, 0, 0))
    return pl.pallas_call(
        functools.partial(_diff_kernel, True, _lam_init(layer)),
        grid=(DEC_BATCH, nq),
        in_specs=[
            pl.BlockSpec((T_DQ, CB), lambda b, t: (q_blk0 + b * nq + t, Q_DQ)),
            pl.BlockSpec((DEC_SEQ, CB), lambda b, t: (kv_blk0 + b, Q_DK)),
            pl.BlockSpec((DEC_SEQ, CB), lambda b, t: (kv_blk0 + b, P_DV)),
            cache_spec, cache_spec,
            pl.BlockSpec((4, DIFF_HD), lambda b, t: (0, 0)),
            pl.BlockSpec((1, 2 * DIFF_HD), lambda b, t: (0, 0)),
        ],
        out_specs=pl.BlockSpec((T_DQ, BRANCH_W), lambda b, t: (b * nq + t, 0)),
        out_shape=jax.ShapeDtypeStruct((N_LAT, BRANCH_W), BF16),
        compiler_params=_params("parallel", "arbitrary"),
        name="diff_lat",
    )(zn, zn, zp, cache_k, cache_v, lam_p, subln_g.reshape(1, -1))


def _merge_kernel(h_ref, wg_ref, bg_ref, yp_ref, ync_ref, ynl_ref, yc_ref, ydc_ref, ydl_ref, x_ref, mod_ref,
                  g2_ref, wb_ref, wo_ref, wq_ref, xo_ref, h2_ref, q_ref, merged_scr):
    br = pl.program_id(1)
    is_ctx = pl.program_id(0) < N_CTX // T_MERGE
    y_nat = jnp.where(is_ctx, ync_ref[...], ynl_ref[...])
    y_diff = jnp.where(is_ctx, ydc_ref[...], ydl_ref[...])
    y = jnp.where(br == 0, yp_ref[...], jnp.where(br == 1, y_nat, jnp.where(br == 2, yc_ref[...], y_diff)))
    gate = _sigmoid(_dot(h_ref[...], wg_ref[...]) + bg_ref[...])
    t = gate * _dot(y, wb_ref[...])

    @pl.when(br == 0)
    def _():
        merged_scr[...] = t

    @pl.when(br > 0)
    def _():
        merged_scr[...] += t

    @pl.when(br == pl.num_programs(1) - 1)
    def _():
        out = _dot(merged_scr[...].astype(BF16), wo_ref[...])
        x = x_ref[...] + mod_ref[2:3, :] * out
        xo_ref[...] = x
        ms = jnp.mean(x * x, axis=-1, keepdims=True)
        h = x * lax.rsqrt(ms + EPS) * g2_ref[...] * (1.0 + mod_ref[4:5, :]) + mod_ref[3:4, :]
        hb = h.astype(BF16)
        h2_ref[...] = hb
        q_ref[...] = _dot(hb, wq_ref[...])


def _merge(h, wg, bg, y_pool, y_nat_ctx, y_nat_lat, y_conv, y_diff_ctx, y_diff_lat, x, mod_l, norm2_g, wb, wo, wq):
    nct = N_CTX // T_MERGE
    nbr = wb.shape[0]
    yspec = pl.BlockSpec((T_MERGE, BRANCH_W), lambda i, br: (i, 0))
    cspec = pl.BlockSpec((T_MERGE, BRANCH_W), lambda i, br: (jnp.minimum(i, nct - 1), 0))
    lspec = pl.BlockSpec((T_MERGE, BRANCH_W), lambda i, br: (jnp.maximum(i - nct, 0), 0))
    tspec = pl.BlockSpec((T_MERGE, D_MODEL), lambda i, br: (i, 0))
    qcols = wq.shape[1]
    return pl.pallas_call(
        _merge_kernel,
        grid=(N_TOK // T_MERGE, nbr),
        in_specs=[tspec,
                  pl.BlockSpec((D_MODEL, D_MODEL), lambda i, br: (0, br)),
                  pl.BlockSpec((1, D_MODEL), lambda i, br: (0, br)),
                  yspec, cspec, lspec, yspec, cspec, lspec,
                  tspec,
                  pl.BlockSpec((None, 6, D_MODEL), lambda i, br: (_mod_row(i, T_MERGE), 0, 0)),
                  pl.BlockSpec((1, D_MODEL), lambda i, br: (0, 0)),
                  pl.BlockSpec((None, BRANCH_W, D_MODEL), lambda i, br: (br, 0, 0)),
                  pl.BlockSpec((D_MODEL, D_MODEL), lambda i, br: (0, 0)),
                  pl.BlockSpec((D_MODEL, qcols), lambda i, br: (0, 0))],
        out_specs=[tspec, tspec, pl.BlockSpec((T_MERGE, qcols), lambda i, br: (i, 0))],
        out_shape=[jax.ShapeDtypeStruct((N_TOK, D_MODEL), F32),
                   jax.ShapeDtypeStruct((N_TOK, D_MODEL), BF16),
                   jax.ShapeDtypeStruct((N_TOK, qcols), F32)],
        scratch_shapes=[pltpu.VMEM((T_MERGE, D_MODEL), F32)],
        compiler_params=_params("parallel", "arbitrary"),
        name="merge",
    )(h, wg, bg, y_pool, y_nat_ctx, y_nat_lat, y_conv, y_diff_ctx, y_diff_lat, x, mod_l, norm2_g, wb, wo, wq)


SUBLANES = 8


def _merge_exchange_pairs(n):
    pairs = []
    t = (n - 1).bit_length()
    p = 1 << (t - 1)
    while p > 0:
        q, r, d = 1 << (t - 1), 0, p
        while d > 0:
            pairs += [(i, i + d) for i in range(n - d) if (i & p) == r]
            d, q, r = q - p, q >> 1, p
        p >>= 1
    return pairs


_SORT16 = _merge_exchange_pairs(PEER_TOPK)


def _top16_sorted(s):
    n = PEER_TOPK
    tiles = [s[j * SUBLANES:(j + 1) * SUBLANES, :] for j in range(PEER_NKEYS // SUBLANES)]
    for i, j in _SORT16:
        tiles[i], tiles[j] = jnp.maximum(tiles[i], tiles[j]), jnp.minimum(tiles[i], tiles[j])
    for shift in (4, 2, 1):
        tiles = [jnp.maximum(tiles[i], pltpu.roll(tiles[n - 1 - i], shift, 0)) for i in range(n)]
        d = n // 2
        while d > 0:
            for i in range(n):
                if (i & d) == 0:
                    tiles[i], tiles[i + d] = jnp.maximum(tiles[i], tiles[i + d]), jnp.minimum(tiles[i], tiles[i + d])
            d //= 2
    return tiles


def _rows_to_sublanes(tiles):
    rid = lax.broadcasted_iota(jnp.int32, (SUBLANES, LANE), 0)
    halves = []
    for base in (0, SUBLANES):
        out = tiles[base]
        for k in range(1, SUBLANES):
            out = jnp.where(rid == k, tiles[base + k], out)
        halves.append(out)
    return jnp.concatenate(halves, axis=0)


N_CAND = 16 + 7 * 8 + 8


def _peer_select_kernel(q_ref, sk_ref, r2_ref, e2_ref, brow_ref, crow_ref):
    q = q_ref[...].astype(BF16)
    half = PEER_NKEYS
    s1_all = _dot_nt(sk_ref[0].astype(BF16), q[:, :half])
    s2_all = _dot_nt(sk_ref[1].astype(BF16), q[:, half:])
    cid = lax.broadcasted_iota(jnp.int32, (N_CAND, LANE), 0)
    rid8 = lax.broadcasted_iota(jnp.int32, (8, LANE), 0)
    ntile = PEER_NKEYS // SUBLANES
    for c in range(T_SEL // LANE):
        ls = slice(c * LANE, (c + 1) * LANE)
        s1 = s1_all[:, ls]
        s2 = s2_all[:, ls]
        t1 = _top16_sorted(s1)
        t2 = _top16_sorted(s2)
        v1 = _rows_to_sublanes(t1)
        v2 = _rows_to_sublanes(t2)
        cand = jnp.concatenate([v1[0:1] + v2] + [v1[a:a + 1] + v2[0:8] for a in range(1, 8)]
                               + [v1[8:16] + v2[0:1]], axis=0)
        top0 = v1[0:1] + v2[0:1]
        chosen = jnp.zeros((N_CAND, LANE), F32)
        zsum = jnp.zeros((1, LANE), F32)
        for k in range(PEER_TOPK):
            m = jnp.max(cand, axis=0, keepdims=True)
            first = jnp.min(jnp.where(cand == m, cid, N_CAND), axis=0, keepdims=True)
            hit = cid == first
            chosen = jnp.where(hit, 1.0, chosen)
            zsum = zsum + jnp.exp(m - top0)
            cand = jnp.where(hit, -jnp.inf, cand)
        cnt_lo = jnp.zeros((8, LANE), F32)
        cnt_lo = jnp.where(rid8 == 0, jnp.sum(chosen[0:16], axis=0, keepdims=True), cnt_lo)
        for a in range(1, 8):
            cnt_lo = jnp.where(rid8 == a, jnp.sum(chosen[8 + 8 * a:16 + 8 * a], axis=0, keepdims=True), cnt_lo)
        cnt = jnp.concatenate([cnt_lo, chosen[N_CAND - 8:N_CAND]], axis=0)
        cnt_rows = [jnp.broadcast_to(cnt[a:a + 1], (SUBLANES, LANE)) for a in range(PEER_TOPK)]
        inv_z = 1.0 / zsum
        for jj in range(ntile // 2):
            ranks, e2s = [], []
            for j in (2 * jj, 2 * jj + 1):
                rows = slice(j * SUBLANES, (j + 1) * SUBLANES)
                d1 = s1[rows]
                d2 = s2[rows]
                brow = jnp.zeros((SUBLANES, LANE), F32)
                rank2 = jnp.zeros((SUBLANES, LANE), F32)
                for a in range(PEER_TOPK):
                    brow = jnp.where(d1 == t1[a], cnt_rows[a], brow)
                    rank2 = jnp.where(t2[a] > d2, float(a + 1), rank2)
                brow_ref[rows, ls] = brow
                crow_ref[rows, ls] = jnp.exp(d1 - t1[0]) * inv_z
                ranks.append(rank2)
                e2s.append(jnp.exp(d2 - t2[0]))
            rows16 = slice(jj * 2 * SUBLANES, (jj + 1) * 2 * SUBLANES)
            r2_ref[rows16, ls] = jnp.concatenate(ranks, axis=0).astype(BF16)
            e2_ref[rows16, ls] = jnp.concatenate(e2s, axis=0).astype(BF16)


def _peer_select(qry, sub_keys):
    nt = N_TOK // T_SEL
    kspec = pl.BlockSpec((None, PEER_NKEYS, T_SEL), lambda i, h: (h, 0, i))

    def kshape(dt):
        return jax.ShapeDtypeStruct((PEER_HEADS, PEER_NKEYS, N_TOK), dt)

    return pl.pallas_call(
        _peer_select_kernel,
        grid=(nt, PEER_HEADS),
        in_specs=[pl.BlockSpec((T_SEL, 2 * PEER_NKEYS), lambda i, h: (i, h)),
                  pl.BlockSpec((None, 2, PEER_NKEYS, PEER_NKEYS), lambda i, h: (h, 0, 0, 0))],
        out_specs=[kspec, kspec, kspec, kspec],
        out_shape=[kshape(BF16), kshape(BF16), kshape(F32), kshape(F32)],
        compiler_params=_params("parallel", "arbitrary"),
        name="peer_select",
    )(qry, sub_keys)


E_PAIR = 2 * PEER_NKEYS
BF16_ROWS = 16


def _row_bf16(row):
    return jnp.broadcast_to(row, (BF16_ROWS, LANE)).astype(BF16)


def _gelu(x):
    return 0.5 * x * (1.0 + lax.erf(x * (2.0 ** -0.5)))


def _peer_dense_kernel(h_ref, u_ref, v_ref, brow_ref, crow_ref, r2_ref, e2_ref, x_ref, mod_ref,
                       o_ref, acc_ref, a_scr, p_scr):
    c = pl.program_id(1)

    @pl.when(c == 0)
    def _():
        acc_ref[...] = jnp.zeros_like(acc_ref)

    hb = h_ref[...]
    zero = jnp.zeros((BF16_ROWS, LANE), BF16)
    npair = E_CHUNK // E_PAIR
    for j in range(npair + 1):
        slot = j % 2
        if j < npair:
            a_scr[slot] = _dot_nt(u_ref[j * E_PAIR:(j + 1) * E_PAIR, :].astype(BF16), hb)
        if j > 0:
            acc_ref[...] += _dot_tn(p_scr[1 - slot], v_ref[(j - 1) * E_PAIR:j * E_PAIR, :].astype(BF16))
        if j == npair:
            break
        for half in range(2):
            n1l = 2 * j + half
            for tc in range(T_PEER // LANE):
                ls = slice(tc * LANE, (tc + 1) * LANE)
                b16 = [_row_bf16(brow_ref[h, n1l:n1l + 1, ls]) for h in range(PEER_HEADS)]
                c16 = [_row_bf16(crow_ref[h, n1l:n1l + 1, ls]) for h in range(PEER_HEADS)]
                e0 = half * PEER_NKEYS
                act = _gelu(a_scr[slot, e0:e0 + PEER_NKEYS, ls]).astype(BF16)
                for rg in range(PEER_NKEYS // BF16_ROWS):
                    rs = slice(rg * BF16_ROWS, (rg + 1) * BF16_ROWS)
                    g = None
                    for h in range(PEER_HEADS):
                        t = jnp.where(r2_ref[h, rs, ls] < b16[h], e2_ref[h, rs, ls], zero) * c16[h]
                        g = t if g is None else g + t
                    p_scr[slot, e0 + rg * BF16_ROWS:e0 + (rg + 1) * BF16_ROWS, ls] = g * act[rs]

    @pl.when(c == pl.num_programs(1) - 1)
    def _():
        o_ref[...] = x_ref[...] + mod_ref[5:6, :] * acc_ref[...]


def _peer_dense(h2, peer_u, peer_v, r2, e2, brow, crow, x, mod_l):
    nt = N_TOK // T_PEER
    nc = PEER_N // E_CHUNK
    n1c = E_CHUNK // PEER_NKEYS
    rowspec = pl.BlockSpec((PEER_HEADS, n1c, T_PEER), lambda i, c: (0, c, i))
    fullspec = pl.BlockSpec((PEER_HEADS, PEER_NKEYS, T_PEER), lambda i, c: (0, 0, i))
    return pl.pallas_call(
        _peer_dense_kernel,
        grid=(nt, nc),
        in_specs=[pl.BlockSpec((T_PEER, D_MODEL), lambda i, c: (i, 0)),
                  pl.BlockSpec((E_CHUNK, D_MODEL), lambda i, c: (c, 0)),
                  pl.BlockSpec((E_CHUNK, D_MODEL), lambda i, c: (c, 0)),
                  rowspec, rowspec, fullspec, fullspec,
                  pl.BlockSpec((T_PEER, D_MODEL), lambda i, c: (i, 0)),
                  pl.BlockSpec((None, 6, D_MODEL), lambda i, c: (_mod_row(i, T_PEER), 0, 0))],
        out_specs=pl.BlockSpec((T_PEER, D_MODEL), lambda i, c: (i, 0)),
        out_shape=jax.ShapeDtypeStruct((N_TOK, D_MODEL), F32),
        scratch_shapes=[pltpu.VMEM((T_PEER, D_MODEL), F32),
                        pltpu.VMEM((2, E_PAIR, T_PEER), F32),
                        pltpu.VMEM((2, E_PAIR, T_PEER), BF16)],
        compiler_params=_params("parallel", "arbitrary"),
        name="peer_dense",
    )(h2, peer_u, peer_v, brow, crow, r2, e2, x, mod_l)


def kernel(x_prompt, x_sample, cache_nat_k, cache_nat_v, cache_diff_k, cache_diff_v, c, c_ctx, w_ada, b_ada, norm1_g, norm2_g, w_in, pool_w, pool_scale, nat_q_g, nat_k_g, nat_rel_bias, conv_w, conv_b, conv_ln_g, conv_ln_b, diff_q_g, diff_k_g, diff_lambda_p, diff_subln_g, w_branch, w_gate, b_gate, w_out, peer_w_query, peer_sub_keys, peer_u, peer_v):
    x = jnp.concatenate([x_prompt.reshape(N_CTX, D_MODEL), x_sample.reshape(N_LAT, D_MODEL)], axis=0)
    cvec = jnp.concatenate([c_ctx[None, :], c, jnp.zeros((8 - 1 - DEC_BATCH, D_MODEL), F32)], axis=0)
    mod = _modulation(cvec, w_ada, b_ada).reshape(DEPTH, 8, 6, D_MODEL)

    gid = jnp.arange(CB) // NAT_HD
    gmat = (gid[:, None] == gid[None, :]).astype(BF16)
    rope = _rope_tables(T_PRE)
    ck_n = cache_nat_k.reshape(DEC_BATCH, DEPTH, PAST_LEN, BRANCH_W)
    cv_n = cache_nat_v.reshape(DEC_BATCH, DEPTH, PAST_LEN, BRANCH_W)
    ck_d = cache_diff_k.reshape(DEC_BATCH, DEPTH, PAST_LEN, BRANCH_W)
    cv_d = cache_diff_v.reshape(DEC_BATCH, DEPTH, PAST_LEN, BRANCH_W)

    states = []
    for l in range(DEPTH):
        gg = jnp.stack([jnp.tile(nat_q_g[l], NAT_HEADS), jnp.tile(nat_k_g[l], NAT_HEADS),
                        jnp.tile(diff_q_g[l], 2 * DIFF_HEADS), jnp.tile(diff_k_g[l], 2 * DIFF_HEADS)])
        h1, zp, zn = _pre(x, mod[l], norm1_g[l].reshape(1, -1), w_in[l].astype(BF16),
                          gg.reshape(len(NORM_ZB), 1, CB), gmat, rope)

        y_pool, y_conv = _seq_mixers(zp, pool_w[l], pool_scale[l], conv_w[l], conv_b[l],
                                     conv_ln_g[l], conv_ln_b[l])
        y_nat_ctx = _nat_ctx(zp, zn)
        y_nat_lat = _nat_lat(zp, zn, ck_n, cv_n, *_nat_bias_table(nat_rel_bias[l]), l)
        y_diff_ctx = _diff_ctx(zp, zn, diff_lambda_p[l], diff_subln_g[l], l)
        y_diff_lat = _diff_lat(zp, zn, ck_d, cv_d, diff_lambda_p[l], diff_subln_g[l], l)

        x, h2, qry = _merge(h1, w_gate[l].astype(BF16), b_gate[l].reshape(1, -1), y_pool, y_nat_ctx, y_nat_lat,
                            y_conv, y_diff_ctx, y_diff_lat, x, mod[l], norm2_g[l].reshape(1, -1),
                            w_branch[l].astype(BF16), w_out[l].astype(BF16), peer_w_query[l].astype(BF16))
        r2, e2, brow, crow = _peer_select(qry, peer_sub_keys[l])
        x = _peer_dense(h2, peer_u[l], peer_v[l], r2, e2, brow, crow, x, mod[l])

        states.append([zn[:N_CTX, Q_NK * CB:(Q_NK + 1) * CB], zp[:N_CTX, P_NV * CB:(P_NV + 1) * CB],
                       zn[:N_CTX, Q_DK * CB:(Q_DK + 1) * CB], zp[:N_CTX, P_DV * CB:(P_DV + 1) * CB]])

    def stack(idx, shape):
        return jnp.stack([states[l][idx].reshape((BATCH, SEQ) + shape) for l in range(DEPTH)], axis=1)

    return (x[:N_CTX].reshape(BATCH, SEQ, D_MODEL),
            x[N_CTX:].reshape(DEC_BATCH, DEC_SEQ, D_MODEL),
            stack(0, (NAT_HEADS, NAT_HD)),
            stack(1, (NAT_HEADS, NAT_HD)),
            stack(2, (DIFF_HEADS, 2, DIFF_HD)),
            stack(3, (DIFF_HEADS, 2 * DIFF_HD)))
```

```python
import functools
import math

import jax
import jax.numpy as jnp
from jax import lax
from jax.experimental import pallas as pl
from jax.experimental.pallas import tpu as pltpu

F32 = jnp.float32
BF16 = jnp.bfloat16

D_MODEL = 1024
BATCH = 16
SEQ = 256
DEPTH = 2
DEC_BATCH = 4
DEC_SEQ = 2048
PAST_LEN = 256
GRID_W = 64
BRANCH_W = 512
POOL_WINDOWS = (2, 4, 8, 16)
POOL_GROUP = 128
NAT_HEADS = 8
NAT_HD = 64
NAT_WIN_R = 8
NAT_WIN_C = 16
CONV_W = 31
DIFF_HEADS = 4
DIFF_HD = 64
PEER_HEADS = 8
PEER_NKEYS = 128
PEER_N = PEER_NKEYS * PEER_NKEYS
PEER_TOPK = 16
ROPE_BASE = 10000.0
EPS = 1e-6
NEG_INF = -1e30
LOG2E = math.log2(math.e)

N_CTX = BATCH * SEQ
N_LAT = DEC_BATCH * DEC_SEQ
N_TOK = N_CTX + N_LAT
GRID_ROWS = DEC_SEQ // GRID_W

CB = 512
ZB_POOL, ZB_NQ, ZB_NK, ZB_NV, ZB_CA, ZB_CGATE, ZB_DQ, ZB_DK, ZB_DV = range(9)

V7X_VMEM_LIMIT = 52 * 1024 * 1024

T_PRE = 1024
T_SEQ = 256
HALO = 16
T_MERGE = 512
T_SEL = 512
T_PEER = 512
E_CHUNK = 1024
LANE = 128


def _sigmoid(x):
    return 1.0 / (1.0 + jnp.exp(-x))


def _dot(a, b):
    return jnp.dot(a, b, preferred_element_type=F32)


def _dot_nt(a, b):
    return lax.dot_general(a, b, (((1,), (1,)), ((), ())), preferred_element_type=F32)


def _dot_tn(a, b):
    return lax.dot_general(a, b, (((0,), (0,)), ((), ())), preferred_element_type=F32)


def _split_bf16(a):
    hi = a.astype(BF16)
    lo = (a - hi.astype(F32)).astype(BF16)
    return hi, lo


def _params(*sem):
    return pltpu.CompilerParams(dimension_semantics=sem, vmem_limit_bytes=V7X_VMEM_LIMIT)


def _mod_row(i, tile):
    nct = N_CTX // tile
    per = DEC_SEQ // tile
    return jnp.where(i < nct, 0, 1 + (i - nct) // per)


def _mod_kernel(c_ref, w_ref, b_ref, o_ref):
    c = c_ref[...]
    a = c * _sigmoid(c)
    w = w_ref[0]
    a_hi, a_lo = _split_bf16(a)
    w_hi, w_lo = _split_bf16(w)
    o_ref[0] = _dot(a_hi, w_hi) + _dot(a_lo, w_hi) + _dot(a_hi, w_lo) + b_ref[0]


def _modulation(cvec, w_ada, b_ada):
    tn = 1024
    return pl.pallas_call(
        _mod_kernel,
        grid=(DEPTH, 6 * D_MODEL // tn),
        in_specs=[
            pl.BlockSpec((8, D_MODEL), lambda l, j: (0, 0)),
            pl.BlockSpec((1, D_MODEL, tn), lambda l, j: (l, 0, j)),
            pl.BlockSpec((1, 1, tn), lambda l, j: (l, 0, j)),
        ],
        out_specs=pl.BlockSpec((1, 8, tn), lambda l, j: (l, 0, j)),
        out_shape=jax.ShapeDtypeStruct((DEPTH, 8, 6 * D_MODEL), F32),
        compiler_params=_params("parallel", "parallel"),
        name="modulation",
    )(cvec, w_ada, b_ada.reshape(DEPTH, 1, 6 * D_MODEL))


def _tile4(t):
    return jnp.concatenate([t, t, t, t], axis=1)


def _hnorm_kernel(x_ref, mod_ref, g_ref, h_ref):
    x = x_ref[...]
    ms = jnp.mean(x * x, axis=-1, keepdims=True)
    y = x * lax.rsqrt(ms + EPS) * g_ref[...]
    h_ref[...] = (y * (1.0 + mod_ref[1:2, :]) + mod_ref[0:1, :]).astype(h_ref.dtype)


def _proj_plain_kernel(h_ref, w_ref, o_ref):
    o_ref[...] = _dot(h_ref[...], w_ref[...])


def _proj_norm_kernel(h_ref, w_ref, gg_ref, gmat_ref, rc_ref, rp_ref, rm_ref, o_ref):
    acc = _dot(h_ref[...], w_ref[...])
    hi, lo = _split_bf16(acc * acc)
    ss = _dot(hi, gmat_ref[...]) + _dot(lo, gmat_ref[...])
    y = acc * lax.rsqrt(ss * (1.0 / NAT_HD) + EPS) * gg_ref[0]
    o_ref[...] = (y * _tile4(rc_ref[...])
                  + pltpu.roll(y, 16, 1) * _tile4(rp_ref[...])
                  + pltpu.roll(y, CB - 16, 1) * _tile4(rm_ref[...]))


def _rope_tables(tile):
    quarter = DIFF_HD // 4
    t = jnp.arange(DEC_SEQ)
    lane = jnp.arange(LANE)
    d = lane % DIFF_HD
    freqs = ROPE_BASE ** (-jnp.arange(quarter, dtype=F32) / quarter)
    pos = jnp.where(d[None, :] < DIFF_HD // 2, (t // GRID_W)[:, None], (t % GRID_W)[:, None]).astype(F32)
    ang = pos * freqs[d % quarter][None, :]
    cos = jnp.cos(ang)
    sin = jnp.sin(ang)
    second = (d % (2 * quarter)) >= quarter
    s_plus = jnp.where(second[None, :], sin, 0.0)
    s_minus = jnp.where(second[None, :], 0.0, -sin)
    ones = jnp.ones((tile, LANE), F32)
    zeros = jnp.zeros((tile, LANE), F32)
    return (jnp.concatenate([cos, ones], 0), jnp.concatenate([s_plus, zeros], 0),
            jnp.concatenate([s_minus, zeros], 0))


PLAIN_ZB = (ZB_POOL, ZB_NV, ZB_CA, ZB_CGATE, ZB_DV)
NORM_ZB = (ZB_NQ, ZB_NK, ZB_DQ, ZB_DK)
P_POOL, P_NV, P_CA, P_CGATE, P_DV = range(5)
Q_NQ, Q_NK, Q_DQ, Q_DK = range(4)


def _pick(j, values):
    out = values[0]
    for n, v in enumerate(values[1:], start=1):
        out = jnp.where(j == n, v, out)
    return out


def _pre(x, mod_l, norm_g, w_in, gg, gmat, rope):
    nct = N_CTX // T_PRE
    per = DEC_SEQ // T_PRE
    ident_blk = DEC_SEQ // T_PRE
    nt = N_TOK // T_PRE
    h = pl.pallas_call(
        _hnorm_kernel,
        grid=(nt,),
        in_specs=[pl.BlockSpec((T_PRE, D_MODEL), lambda i: (i, 0)),
                  pl.BlockSpec((None, 6, D_MODEL), lambda i: (_mod_row(i, T_PRE), 0, 0)),
                  pl.BlockSpec((1, D_MODEL), lambda i: (0, 0))],
        out_specs=pl.BlockSpec((T_PRE, D_MODEL), lambda i: (i, 0)),
        out_shape=jax.ShapeDtypeStruct((N_TOK, D_MODEL), BF16),
        compiler_params=_params("parallel"),
        name="hnorm",
    )(x, mod_l, norm_g)

    hspec = pl.BlockSpec((T_PRE, D_MODEL), lambda i, j: (i, 0))
    ospec = pl.BlockSpec((T_PRE, CB), lambda i, j: (i, j))

    zp = pl.pallas_call(
        _proj_plain_kernel,
        grid=(nt, len(PLAIN_ZB)),
        in_specs=[hspec, pl.BlockSpec((D_MODEL, CB), lambda i, j: (0, _pick(j, PLAIN_ZB)))],
        out_specs=ospec,
        out_shape=jax.ShapeDtypeStruct((N_TOK, len(PLAIN_ZB) * CB), F32),
        compiler_params=_params("parallel", "arbitrary"),
        name="proj_plain",
    )(h, w_in)

    def rope_idx(i, j):
        use = ((j == Q_DQ) | (j == Q_DK)) & (i >= nct)
        return jnp.where(use, (i - nct) % per, ident_blk), 0

    rope_spec = pl.BlockSpec((T_PRE, LANE), rope_idx)
    zn = pl.pallas_call(
        _proj_norm_kernel,
        grid=(nt, len(NORM_ZB)),
        in_specs=[hspec, pl.BlockSpec((D_MODEL, CB), lambda i, j: (0, _pick(j, NORM_ZB))),
                  pl.BlockSpec((1, 1, CB), lambda i, j: (j, 0, 0)),
                  pl.BlockSpec((CB, CB), lambda i, j: (0, 0)),
                  rope_spec, rope_spec, rope_spec],
        out_specs=ospec,
        out_shape=jax.ShapeDtypeStruct((N_TOK, len(NORM_ZB) * CB), F32),
        compiler_params=_params("parallel", "arbitrary"),
        name="proj_norm",
    )(h, w_in, gg, gmat, *rope)

    return h, zp, zn


def _seq_kernel(zp_ref, zp_prev, zp_next, za_ref, za_prev, za_next, zb_ref, zb_prev, zb_next,
                pw_ref, ps_ref, cw_ref, cb_ref, lg_ref, lb_ref,
                yp_ref, yc_ref, pbuf, ubuf, ybuf):
    i = pl.program_id(0)
    nct = N_CTX // T_SEQ
    per = DEC_SEQ // T_SEQ
    is_lat = i >= nct
    kk = (i - nct) % per
    has_prev = is_lat & (kk != 0)
    has_next = is_lat & (kk != per - 1)
    pos0 = jnp.where(is_lat, kk * T_SEQ, 0)
    seq_len = jnp.where(is_lat, DEC_SEQ, SEQ)
    fp = jnp.where(has_prev, 1.0, 0.0).astype(F32)
    fn = jnp.where(has_next, 1.0, 0.0).astype(F32)

    pbuf[0:HALO, :] = zp_prev[...] * fp
    pbuf[HALO:HALO + T_SEQ, :] = zp_ref[...]
    pbuf[HALO + T_SEQ:, :] = zp_next[...] * fn
    ubuf[0:HALO, :] = za_prev[...] * _sigmoid(zb_prev[...]) * fp
    ubuf[HALO:HALO + T_SEQ, :] = za_ref[...] * _sigmoid(zb_ref[...])
    ubuf[HALO + T_SEQ:, :] = za_next[...] * _sigmoid(zb_next[...]) * fn

    pos = pos0 + lax.broadcasted_iota(jnp.int32, (T_SEQ, LANE), 0)
    for gi, w in enumerate(POOL_WINDOWS):
        ls = slice(gi * LANE, (gi + 1) * LANE)
        acc = None
        for s in range(-(w // 2), w - w // 2):
            v = pbuf[HALO + s:HALO + s + T_SEQ, ls]
            acc = v if acc is None else acc + v
        lo = jnp.maximum(pos - w // 2, 0)
        hi = jnp.minimum(pos + (w - w // 2), seq_len)
        mean = acc / (hi - lo).astype(F32)
        d = mean - pbuf[HALO:HALO + T_SEQ, ls]
        y = _dot(d.astype(BF16), pw_ref[gi].astype(BF16)) * ps_ref[:, ls]
        yp_ref[:, ls] = y.astype(yp_ref.dtype)

    base = HALO - CONV_W // 2
    for c in range(BRANCH_W // LANE):
        ls = slice(c * LANE, (c + 1) * LANE)
        acc = None
        for k in range(CONV_W):
            v = ubuf[base + k:base + k + T_SEQ, ls] * cw_ref[k:k + 1, ls]
            acc = v if acc is None else acc + v
        ybuf[:, ls] = acc + cb_ref[:, ls]
    y = ybuf[...]
    mu = jnp.mean(y, axis=-1, keepdims=True)
    yc = y - mu
    var = jnp.mean(yc * yc, axis=-1, keepdims=True)
    yn = yc * lax.rsqrt(var + EPS) * lg_ref[...] + lb_ref[...]
    yc_ref[...] = (yn * _sigmoid(yn)).astype(yc_ref.dtype)


def _seq_mixers(zp, pool_w, pool_scale, conv_w, conv_b, ln_g, ln_b):
    nt = N_TOK // T_SEQ
    hb = T_SEQ // HALO
    n_hblk = N_TOK // HALO

    def specs(cb):
        return [
            pl.BlockSpec((T_SEQ, CB), lambda i: (i, cb)),
            pl.BlockSpec((HALO, CB), lambda i: (jnp.maximum(i * hb - 1, 0), cb)),
            pl.BlockSpec((HALO, CB), lambda i: (jnp.minimum((i + 1) * hb, n_hblk - 1), cb)),
        ]

    def full(shape):
        return pl.BlockSpec(shape, lambda i: (0,) * len(shape))

    cw = jnp.concatenate([conv_w, jnp.zeros((1, BRANCH_W), F32)], axis=0)
    out_spec = pl.BlockSpec((T_SEQ, BRANCH_W), lambda i: (i, 0))
    return pl.pallas_call(
        _seq_kernel,
        grid=(nt,),
        in_specs=(specs(P_POOL) + specs(P_CA) + specs(P_CGATE)
                  + [full((4, POOL_GROUP, POOL_GROUP)), full((1, BRANCH_W)), full((CONV_W + 1, BRANCH_W)),
                     full((1, BRANCH_W)), full((1, BRANCH_W)), full((1, BRANCH_W))]),
        out_specs=[out_spec, out_spec],
        out_shape=[jax.ShapeDtypeStruct((N_TOK, BRANCH_W), BF16)] * 2,
        scratch_shapes=[pltpu.VMEM((T_SEQ + 2 * HALO, BRANCH_W), F32),
                        pltpu.VMEM((T_SEQ + 2 * HALO, BRANCH_W), F32),
                        pltpu.VMEM((T_SEQ, BRANCH_W), F32)],
        compiler_params=_params("parallel"),
        name="pool_conv",
    )(zp, zp, zp, zp, zp, zp, zp, zp, zp, pool_w, pool_scale.reshape(1, -1), cw,
      conv_b.reshape(1, -1), ln_g.reshape(1, -1), ln_b.reshape(1, -1))


def _nat_ctx_kernel(q_ref, k_ref, v_ref, o_ref):
    for h in range(NAT_HEADS):
        sl = slice(h * NAT_HD, (h + 1) * NAT_HD)
        q = (q_ref[:, sl] * (NAT_HD ** -0.5 * LOG2E)).astype(BF16)
        s = _dot_nt(q, k_ref[:, sl].astype(BF16))
        m = jnp.max(s, axis=-1, keepdims=True)
        e = jnp.exp2(s - m)
        den = jnp.sum(e, axis=-1, keepdims=True)
        o = _dot(e.astype(BF16), v_ref[:, sl].astype(BF16)) / den
        o_ref[:, sl] = o.astype(o_ref.dtype)


def _nat_ctx(zp, zn):
    def spec(cb):
        return pl.BlockSpec((SEQ, CB), lambda b: (b, cb))

    return pl.pallas_call(
        _nat_ctx_kernel,
        grid=(BATCH,),
        in_specs=[spec(Q_NQ), spec(Q_NK), spec(P_NV)],
        out_specs=pl.BlockSpec((SEQ, BRANCH_W), lambda b: (b, 0)),
        out_shape=jax.ShapeDtypeStruct((N_CTX, BRANCH_W), BF16),
        compiler_params=_params("parallel"),
        name="nat_ctx",
    )(zn, zn, zp)


NAT_QROWS = 4
NAT_KROWS = 12


def _nat_row0(r):
    return jnp.clip(r - NAT_WIN_R // 2, 0, GRID_ROWS - NAT_WIN_R)


def _nat_lat_kernel(q_ref, k_ref, v_ref, ck_ref, cv_ref, bias_ref, o_ref):
    g = pl.program_id(1)
    w0 = jnp.minimum(_nat_row0(g * NAT_QROWS), GRID_ROWS - NAT_KROWS)
    start = pl.multiple_of(w0 * GRID_W, GRID_W)
    nwin = NAT_KROWS * GRID_W
    for h in range(NAT_HEADS):
        sl = slice(h * NAT_HD, (h + 1) * NAT_HD)
        q = (q_ref[:, sl] * (NAT_HD ** -0.5 * LOG2E)).astype(BF16)
        kw = k_ref[pl.ds(start, nwin), sl].astype(BF16)
        vw = v_ref[pl.ds(start, nwin), sl].astype(BF16)
        s_loc = _dot_nt(q, kw) + bias_ref[h]
        s_ctx = _dot_nt(q, ck_ref[:, sl].astype(BF16))
        m = jnp.maximum(jnp.max(s_loc, axis=-1, keepdims=True), jnp.max(s_ctx, axis=-1, keepdims=True))
        e_loc = jnp.exp2(s_loc - m)
        e_ctx = jnp.exp2(s_ctx - m)
        den = jnp.sum(e_loc, axis=-1, keepdims=True) + jnp.sum(e_ctx, axis=-1, keepdims=True)
        o = (_dot(e_loc.astype(BF16), vw) + _dot(e_ctx.astype(BF16), cv_ref[:, sl].astype(BF16))) / den
        o_ref[:, sl] = o.astype(o_ref.dtype)


def _nat_bias_kernel(rb_ref, oh_ref, o_ref):
    x = rb_ref[...]
    hi = x.astype(BF16)
    r1 = x - hi.astype(F32)
    mid = r1.astype(BF16)
    lo = (r1 - mid.astype(F32)).astype(BF16)
    oh = oh_ref[...]
    o_ref[...] = _dot(hi, oh) + _dot(mid, oh) + _dot(lo, oh)


def _nat_bias_table(rel_bias):
    ndr = 2 * NAT_WIN_R - 1
    ndc = 2 * NAT_WIN_C - 1
    q = jnp.arange(GRID_W)
    kc = jnp.arange(GRID_W)
    dcol = jnp.clip(kc[None, :] - q[:, None] + NAT_WIN_C - 1, 0, ndc - 1)
    wstart = jnp.clip(q - NAT_WIN_C // 2, 0, GRID_W - NAT_WIN_C)
    valid = (kc[None, :] >= wstart[:, None]) & (kc[None, :] < wstart[:, None] + NAT_WIN_C)
    d = jnp.arange(LANE)
    onehot = jnp.where(d[:, None, None] == ndc, jnp.logical_not(valid)[None],
                       (d[:, None, None] == dcol[None]) & valid[None])
    onehot = onehot.reshape(LANE, GRID_W * GRID_W).astype(BF16)
    nrow = NAT_HEADS * ndr
    rb = jnp.concatenate([rel_bias.reshape(nrow, ndc).astype(F32) * LOG2E, jnp.full((nrow, 1), NEG_INF, F32),
                          jnp.zeros((nrow, LANE - ndc - 1), F32)], axis=1)
    rb = jnp.concatenate([rb, jnp.zeros((LANE - nrow, LANE), F32)], axis=0)
    tcol = pl.pallas_call(
        _nat_bias_kernel,
        out_shape=jax.ShapeDtypeStruct((LANE, GRID_W * GRID_W), F32),
        compiler_params=pltpu.CompilerParams(vmem_limit_bytes=V7X_VMEM_LIMIT),
        name="nat_bias",
    )(rb, onehot)
    tcol = tcol[:nrow].reshape(NAT_HEADS, ndr, GRID_W, GRID_W)
    neg = jnp.full((NAT_HEADS, GRID_W, GRID_W), NEG_INF, F32)
    tables, variant_of_group, seen = [], [], {}
    for g in range(GRID_ROWS // NAT_QROWS):
        r_first = g * NAT_QROWS
        w0 = min(max(r_first - NAT_WIN_R // 2, 0), GRID_ROWS - NAT_WIN_R, GRID_ROWS - NAT_KROWS)
        rows = [(r_first + j - w0, min(max(r_first + j - NAT_WIN_R // 2, 0), GRID_ROWS - NAT_WIN_R) - w0)
                for j in range(NAT_QROWS)]
        key = tuple(rows)
        if key not in seen:
            seen[key] = len(tables)
            blocks = []
            for rq, rw in rows:
                blocks.append(jnp.concatenate(
                    [tcol[:, i - rq + NAT_WIN_R - 1] if rw <= i < rw + NAT_WIN_R else neg
                     for i in range(NAT_KROWS)], axis=-1))
            tables.append(jnp.concatenate(blocks, axis=1))
        variant_of_group.append(seen[key])
    return jnp.stack(tables, axis=0), tuple(variant_of_group)


def _nat_lat(zp, zn, cache_k, cache_v, bias_tab, variant_of_group, layer):
    tq = NAT_QROWS * GRID_W
    ngroups = GRID_ROWS // NAT_QROWS
    q_blk0 = N_CTX // tq
    kv_blk0 = N_CTX // DEC_SEQ
    cache_spec = pl.BlockSpec((None, None, PAST_LEN, BRANCH_W), lambda b, g: (b, layer, 0, 0))
    return pl.pallas_call(
        _nat_lat_kernel,
        grid=(DEC_BATCH, ngroups),
        in_specs=[
            pl.BlockSpec((tq, CB), lambda b, g: (q_blk0 + b * ngroups + g, Q_NQ)),
            pl.BlockSpec((DEC_SEQ, CB), lambda b, g: (kv_blk0 + b, Q_NK)),
            pl.BlockSpec((DEC_SEQ, CB), lambda b, g: (kv_blk0 + b, P_NV)),
            cache_spec, cache_spec,
            pl.BlockSpec((None, NAT_HEADS, tq, NAT_KROWS * GRID_W),
                         lambda b, g: (_pick(g, variant_of_group), 0, 0, 0)),
        ],
        out_specs=pl.BlockSpec((tq, BRANCH_W), lambda b, g: (b * ngroups + g, 0)),
        out_shape=jax.ShapeDtypeStruct((N_LAT, BRANCH_W), BF16),
        compiler_params=_params("parallel", "arbitrary"),
        name="nat_lat",
    )(zn, zn, zp, cache_k, cache_v, bias_tab)


def _diff_kernel(has_cache, lam_init, *refs):
    if has_cache:
        q_ref, k_ref, v_ref, ck_ref, cv_ref, lamp_ref, g_ref, o_ref = refs
    else:
        q_ref, k_ref, v_ref, lamp_ref, g_ref, o_ref = refs
    lp = lamp_ref[...]
    lam = (jnp.exp(jnp.sum(lp[0:1] * lp[1:2], axis=-1, keepdims=True))
           - jnp.exp(jnp.sum(lp[2:3] * lp[3:4], axis=-1, keepdims=True)) + lam_init)
    hv = 2 * DIFF_HD
    for h in range(DIFF_HEADS):
        vs = slice(h * hv, (h + 1) * hv)
        vb = v_ref[:, vs].astype(BF16)
        if has_cache:
            cvb = cv_ref[:, vs].astype(BF16)
        parts = []
        for i in range(2):
            sl = slice(h * hv + i * DIFF_HD, h * hv + (i + 1) * DIFF_HD)
            q = (q_ref[:, sl] * (DIFF_HD ** -0.5 * LOG2E)).astype(BF16)
            s = _dot_nt(q, k_ref[:, sl].astype(BF16))
            m = jnp.max(s, axis=-1, keepdims=True)
            if has_cache:
                sc = _dot_nt(q, ck_ref[:, sl].astype(BF16))
                m = jnp.maximum(m, jnp.max(sc, axis=-1, keepdims=True))
            e = jnp.exp2(s - m)
            den = jnp.sum(e, axis=-1, keepdims=True)
            o = _dot(e.astype(BF16), vb)
            if has_cache:
                ec = jnp.exp2(sc - m)
                den = den + jnp.sum(ec, axis=-1, keepdims=True)
                o = o + _dot(ec.astype(BF16), cvb)
            parts.append((o, 1.0 / den))
        (o1, r1), (o2, r2) = parts
        o = o1 * r1 - o2 * (lam * r2)
        ms = jnp.mean(o * o, axis=-1, keepdims=True)
        y = o * lax.rsqrt(ms + EPS) * g_ref[...] * (1.0 - lam_init)
        o_ref[:, vs] = y.astype(o_ref.dtype)


def _lam_init(layer):
    return 0.8 - 0.6 * math.exp(-0.3 * layer)


def _diff_ctx(zp, zn, lam_p, subln_g, layer):
    def spec(cb):
        return pl.BlockSpec((SEQ, CB), lambda b: (b, cb))

    return pl.pallas_call(
        functools.partial(_diff_kernel, False, _lam_init(layer)),
        grid=(BATCH,),
        in_specs=[spec(Q_DQ), spec(Q_DK), spec(P_DV),
                  pl.BlockSpec((4, DIFF_HD), lambda b: (0, 0)),
                  pl.BlockSpec((1, 2 * DIFF_HD), lambda b: (0, 0))],
        out_specs=pl.BlockSpec((SEQ, BRANCH_W), lambda b: (b, 0)),
        out_shape=jax.ShapeDtypeStruct((N_CTX, BRANCH_W), BF16),
        compiler_params=_params("parallel"),
        name="diff_ctx",
    )(zn, zn, zp, lam_p, subln_g.reshape(1, -1))


T_DQ = 256


def _diff_lat(zp, zn, cache_k, cache_v, lam_p, subln_g, layer):
    nq = DEC_SEQ // T_DQ
    q_blk0 = N_CTX // T_DQ
    kv_blk0 = N_CTX // DEC_SEQ
    cache_spec = pl.BlockSpec((None, None, PAST_LEN, BRANCH_W), lambda b, t: (b, layer, 0, 0))
    return pl.pallas_call(
        functools.partial(_diff_kernel, True, _lam_init(layer)),
        grid=(DEC_BATCH, nq),
        in_specs=[
            pl.BlockSpec((T_DQ, CB), lambda b, t: (q_blk0 + b * nq + t, Q_DQ)),
            pl.BlockSpec((DEC_SEQ, CB), lambda b, t: (kv_blk0 + b, Q_DK)),
            pl.BlockSpec((DEC_SEQ, CB), lambda b, t: (kv_blk0 + b, P_DV)),
            cache_spec, cache_spec,
            pl.BlockSpec((4, DIFF_HD), lambda b, t: (0, 0)),
            pl.BlockSpec((1, 2 * DIFF_HD), lambda b, t: (0, 0)),
        ],
        out_specs=pl.BlockSpec((T_DQ, BRANCH_W), lambda b, t: (b * nq + t, 0)),
        out_shape=jax.ShapeDtypeStruct((N_LAT, BRANCH_W), BF16),
        compiler_params=_params("parallel", "arbitrary"),
        name="diff_lat",
    )(zn, zn, zp, cache_k, cache_v, lam_p, subln_g.reshape(1, -1))


def _merge_kernel(h_ref, wg_ref, bg_ref, yp_ref, ync_ref, ynl_ref, yc_ref, ydc_ref, ydl_ref, x_ref, mod_ref,
                  g2_ref, wb_ref, wo_ref, wq_ref, xo_ref, h2_ref, q_ref, merged_scr):
    br = pl.program_id(1)
    is_ctx = pl.program_id(0) < N_CTX // T_MERGE
    y_nat = jnp.where(is_ctx, ync_ref[...], ynl_ref[...])
    y_diff = jnp.where(is_ctx, ydc_ref[...], ydl_ref[...])
    y = jnp.where(br == 0, yp_ref[...], jnp.where(br == 1, y_nat, jnp.where(br == 2, yc_ref[...], y_diff)))
    gate = _sigmoid(_dot(h_ref[...], wg_ref[...]) + bg_ref[...])
    t = gate * _dot(y, wb_ref[...])

    @pl.when(br == 0)
    def _():
        merged_scr[...] = t

    @pl.when(br > 0)
    def _():
        merged_scr[...] += t

    @pl.when(br == pl.num_programs(1) - 1)
    def _():
        out = _dot(merged_scr[...].astype(BF16), wo_ref[...])
        x = x_ref[...] + mod_ref[2:3, :] * out
        xo_ref[...] = x
        ms = jnp.mean(x * x, axis=-1, keepdims=True)
        h = x * lax.rsqrt(ms + EPS) * g2_ref[...] * (1.0 + mod_ref[4:5, :]) + mod_ref[3:4, :]
        hb = h.astype(BF16)
        h2_ref[...] = hb
        q_ref[...] = _dot(hb, wq_ref[...])


def _merge(h, wg, bg, y_pool, y_nat_ctx, y_nat_lat, y_conv, y_diff_ctx, y_diff_lat, x, mod_l, norm2_g, wb, wo, wq):
    nct = N_CTX // T_MERGE
    nbr = wb.shape[0]
    yspec = pl.BlockSpec((T_MERGE, BRANCH_W), lambda i, br: (i, 0))
    cspec = pl.BlockSpec((T_MERGE, BRANCH_W), lambda i, br: (jnp.minimum(i, nct - 1), 0))
    lspec = pl.BlockSpec((T_MERGE, BRANCH_W), lambda i, br: (jnp.maximum(i - nct, 0), 0))
    tspec = pl.BlockSpec((T_MERGE, D_MODEL), lambda i, br: (i, 0))
    qcols = wq.shape[1]
    return pl.pallas_call(
        _merge_kernel,
        grid=(N_TOK // T_MERGE, nbr),
        in_specs=[tspec,
                  pl.BlockSpec((D_MODEL, D_MODEL), lambda i, br: (0, br)),
                  pl.BlockSpec((1, D_MODEL), lambda i, br: (0, br)),
                  yspec, cspec, lspec, yspec, cspec, lspec,
                  tspec,
                  pl.BlockSpec((None, 6, D_MODEL), lambda i, br: (_mod_row(i, T_MERGE), 0, 0)),
                  pl.BlockSpec((1, D_MODEL), lambda i, br: (0, 0)),
                  pl.BlockSpec((None, BRANCH_W, D_MODEL), lambda i, br: (br, 0, 0)),
                  pl.BlockSpec((D_MODEL, D_MODEL), lambda i, br: (0, 0)),
                  pl.BlockSpec((D_MODEL, qcols), lambda i, br: (0, 0))],
        out_specs=[tspec, tspec, pl.BlockSpec((T_MERGE, qcols), lambda i, br: (i, 0))],
        out_shape=[jax.ShapeDtypeStruct((N_TOK, D_MODEL), F32),
                   jax.ShapeDtypeStruct((N_TOK, D_MODEL), BF16),
                   jax.ShapeDtypeStruct((N_TOK, qcols), F32)],
        scratch_shapes=[pltpu.VMEM((T_MERGE, D_MODEL), F32)],
        compiler_params=_params("parallel", "arbitrary"),
        name="merge",
    )(h, wg, bg, y_pool, y_nat_ctx, y_nat_lat, y_conv, y_diff_ctx, y_diff_lat, x, mod_l, norm2_g, wb, wo, wq)


SUBLANES = 8


def _merge_exchange_pairs(n):
    pairs = []
    t = (n - 1).bit_length()
    p = 1 << (t - 1)
    while p > 0:
        q, r, d = 1 << (t - 1), 0, p
        while d > 0:
            pairs += [(i, i + d) for i in range(n - d) if (i & p) == r]
            d, q, r = q - p, q >> 1, p
        p >>= 1
    return pairs


_SORT16 = _merge_exchange_pairs(PEER_TOPK)


def _cmpx(tiles, i, j):
    a, b = tiles[i], tiles[j]
    if b is None:
        return
    if a is None:
        tiles[i], tiles[j] = b, None
        return
    tiles[i], tiles[j] = jnp.maximum(a, b), jnp.minimum(a, b)


def _top16_sorted(s):
    n = PEER_TOPK
    tiles = [s[j * SUBLANES:(j + 1) * SUBLANES, :] for j in range(s.shape[0] // SUBLANES)]
    tiles += [None] * (n - len(tiles))
    for i, j in _SORT16:
        _cmpx(tiles, i, j)
    for shift in (4, 2, 1):
        merged = []
        for i in range(n):
            a, b = tiles[i], tiles[n - 1 - i]
            b = None if b is None else pltpu.roll(b, shift, 0)
            merged.append(b if a is None else a if b is None else jnp.maximum(a, b))
        tiles = merged
        d = n // 2
        while d > 0:
            for i in range(n):
                if (i & d) == 0:
                    _cmpx(tiles, i, i + d)
            d //= 2
    return tiles


def _rows_to_sublanes(tiles):
    rid = lax.broadcasted_iota(jnp.int32, (SUBLANES, LANE), 0)
    halves = []
    for base in (0, SUBLANES):
        out = tiles[base]
        for k in range(1, SUBLANES):
            out = jnp.where(rid == k, tiles[base + k], out)
        halves.append(out)
    return jnp.concatenate(halves, axis=0)


N_CAND = 16 + 7 * 8 + 8


def _peer_select_kernel(q_ref, sk_ref, r2_ref, e2_ref, brow_ref, crow_ref, chosen_scr):
    q = q_ref[...].astype(BF16)
    half = PEER_NKEYS
    s1_all = _dot_nt(sk_ref[0].astype(BF16), q[:, :half])
    s2_all = _dot_nt(sk_ref[1].astype(BF16), q[:, half:])
    cid = lax.broadcasted_iota(jnp.int32, (N_CAND, LANE), 0)
    rid8 = lax.broadcasted_iota(jnp.int32, (8, LANE), 0)
    ntile = PEER_NKEYS // SUBLANES
    for c in range(T_SEL // LANE):
        ls = slice(c * LANE, (c + 1) * LANE)
        s1 = s1_all[:, ls]
        s2 = s2_all[:, ls]
        t1 = _top16_sorted(s1)
        t2 = _top16_sorted(s2)
        v1 = _rows_to_sublanes(t1)
        v2 = _rows_to_sublanes(t2)
        cand = jnp.concatenate([v1[0:1] + v2] + [v1[a:a + 1] + v2[0:8] for a in range(1, 8)]
                               + [v1[8:16] + v2[0:1]], axis=0)
        ctop = _top16_sorted(cand)
        zsum = jnp.zeros((1, LANE), F32)
        for k in range(PEER_TOPK):
            zsum = zsum + jnp.exp(ctop[k][0:1] - ctop[0][0:1])
        fast = jnp.where(cand >= ctop[PEER_TOPK - 1][0:1], 1.0, 0.0)
        chosen_scr[...] = fast
        n_fast = jnp.sum(fast, axis=0, keepdims=True)
        tied = jnp.sum(jnp.where(n_fast != float(PEER_TOPK), 1.0, 0.0)) > 0.0

        @pl.when(tied)
        def _():
            rest = cand
            walk = jnp.zeros((N_CAND, LANE), F32)
            for k in range(PEER_TOPK):
                m = jnp.max(rest, axis=0, keepdims=True)
                first = jnp.min(jnp.where(rest == m, cid, N_CAND), axis=0, keepdims=True)
                hit = cid == first
                walk = jnp.where(hit, 1.0, walk)
                rest = jnp.where(hit, -jnp.inf, rest)
            chosen_scr[...] = walk

        chosen = chosen_scr[...]
        cnt_lo = jnp.zeros((8, LANE), F32)
        cnt_lo = jnp.where(rid8 == 0, jnp.sum(chosen[0:16], axis=0, keepdims=True), cnt_lo)
        for a in range(1, 8):
            cnt_lo = jnp.where(rid8 == a, jnp.sum(chosen[8 + 8 * a:16 + 8 * a], axis=0, keepdims=True), cnt_lo)
        cnt = jnp.concatenate([cnt_lo, chosen[N_CAND - 8:N_CAND]], axis=0)
        cnt_rows = [jnp.broadcast_to(cnt[a:a + 1], (SUBLANES, LANE)) for a in range(PEER_TOPK)]
        inv_z = 1.0 / zsum
        for jj in range(ntile // 2):
            ranks, e2s = [], []
            for j in (2 * jj, 2 * jj + 1):
                rows = slice(j * SUBLANES, (j + 1) * SUBLANES)
                d1 = s1[rows]
                d2 = s2[rows]
                brow = jnp.zeros((SUBLANES, LANE), F32)
                rank2 = jnp.zeros((SUBLANES, LANE), F32)
                for a in range(PEER_TOPK):
                    brow = jnp.where(d1 == t1[a], cnt_rows[a], brow)
                    rank2 = jnp.where(t2[a] > d2, float(a + 1), rank2)
                brow_ref[rows, ls] = brow
                crow_ref[rows, ls] = jnp.exp(d1 - t1[0]) * inv_z
                ranks.append(rank2)
                e2s.append(jnp.exp(d2 - t2[0]))
            rows16 = slice(jj * 2 * SUBLANES, (jj + 1) * 2 * SUBLANES)
            r2_ref[rows16, ls] = jnp.concatenate(ranks, axis=0).astype(BF16)
            e2_ref[rows16, ls] = jnp.concatenate(e2s, axis=0).astype(BF16)


def _peer_select(qry, sub_keys):
    nt = N_TOK // T_SEL
    kspec = pl.BlockSpec((None, PEER_NKEYS, T_SEL), lambda i, h: (h, 0, i))

    def kshape(dt):
        return jax.ShapeDtypeStruct((PEER_HEADS, PEER_NKEYS, N_TOK), dt)

    return pl.pallas_call(
        _peer_select_kernel,
        grid=(nt, PEER_HEADS),
        in_specs=[pl.BlockSpec((T_SEL, 2 * PEER_NKEYS), lambda i, h: (i, h)),
                  pl.BlockSpec((None, 2, PEER_NKEYS, PEER_NKEYS), lambda i, h: (h, 0, 0, 0))],
        out_specs=[kspec, kspec, kspec, kspec],
        out_shape=[kshape(BF16), kshape(BF16), kshape(F32), kshape(F32)],
        scratch_shapes=[pltpu.VMEM((N_CAND, LANE), F32)],
        compiler_params=_params("parallel", "arbitrary"),
        name="peer_select",
    )(qry, sub_keys)


E_PAIR = 2 * PEER_NKEYS
BF16_ROWS = 16


def _row_bf16(row):
    return jnp.broadcast_to(row, (BF16_ROWS, LANE)).astype(BF16)


def _gelu(x):
    return 0.5 * x * (1.0 + lax.erf(x * (2.0 ** -0.5)))


def _peer_dense_kernel(h_ref, u_ref, v_ref, brow_ref, crow_ref, r2_ref, e2_ref, x_ref, mod_ref,
                       o_ref, acc_ref, a_scr, p_scr):
    c = pl.program_id(1)

    @pl.when(c == 0)
    def _():
        acc_ref[...] = jnp.zeros_like(acc_ref)

    hb = h_ref[...]
    zero = jnp.zeros((BF16_ROWS, LANE), BF16)
    npair = E_CHUNK // E_PAIR
    for j in range(npair + 1):
        slot = j % 2
        if j < npair:
            a_scr[slot] = _dot_nt(u_ref[j * E_PAIR:(j + 1) * E_PAIR, :].astype(BF16), hb)
        if j > 0:
            acc_ref[...] += _dot_tn(p_scr[1 - slot], v_ref[(j - 1) * E_PAIR:j * E_PAIR, :].astype(BF16))
        if j == npair:
            break
        for half in range(2):
            n1l = 2 * j + half
            for tc in range(T_PEER // LANE):
                ls = slice(tc * LANE, (tc + 1) * LANE)
                b16 = [_row_bf16(brow_ref[h, n1l:n1l + 1, ls]) for h in range(PEER_HEADS)]
                c16 = [_row_bf16(crow_ref[h, n1l:n1l + 1, ls]) for h in range(PEER_HEADS)]
                e0 = half * PEER_NKEYS
                act = _gelu(a_scr[slot, e0:e0 + PEER_NKEYS, ls]).astype(BF16)
                for rg in range(PEER_NKEYS // BF16_ROWS):
                    rs = slice(rg * BF16_ROWS, (rg + 1) * BF16_ROWS)
                    g = None
                    for h in range(PEER_HEADS):
                        t = jnp.where(r2_ref[h, rs, ls] < b16[h], e2_ref[h, rs, ls], zero) * c16[h]
                        g = t if g is None else g + t
                    p_scr[slot, e0 + rg * BF16_ROWS:e0 + (rg + 1) * BF16_ROWS, ls] = g * act[rs]

    @pl.when(c == pl.num_programs(1) - 1)
    def _():
        o_ref[...] = x_ref[...] + mod_ref[5:6, :] * acc_ref[...]


def _peer_dense(h2, peer_u, peer_v, layer, r2, e2, brow, crow, x, mod_l):
    nt = N_TOK // T_PEER
    nc = PEER_N // E_CHUNK
    n1c = E_CHUNK // PEER_NKEYS
    rowspec = pl.BlockSpec((PEER_HEADS, n1c, T_PEER), lambda i, c: (0, c, i))
    fullspec = pl.BlockSpec((PEER_HEADS, PEER_NKEYS, T_PEER), lambda i, c: (0, 0, i))
    return pl.pallas_call(
        _peer_dense_kernel,
        grid=(nt, nc),
        in_specs=[pl.BlockSpec((T_PEER, D_MODEL), lambda i, c: (i, 0)),
                  pl.BlockSpec((None, E_CHUNK, D_MODEL), lambda i, c: (layer, c, 0)),
                  pl.BlockSpec((None, E_CHUNK, D_MODEL), lambda i, c: (layer, c, 0)),
                  rowspec, rowspec, fullspec, fullspec,
                  pl.BlockSpec((T_PEER, D_MODEL), lambda i, c: (i, 0)),
                  pl.BlockSpec((None, 6, D_MODEL), lambda i, c: (_mod_row(i, T_PEER), 0, 0))],
        out_specs=pl.BlockSpec((T_PEER, D_MODEL), lambda i, c: (i, 0)),
        out_shape=jax.ShapeDtypeStruct((N_TOK, D_MODEL), F32),
        scratch_shapes=[pltpu.VMEM((T_PEER, D_MODEL), F32),
                        pltpu.VMEM((2, E_PAIR, T_PEER), F32),
                        pltpu.VMEM((2, E_PAIR, T_PEER), BF16)],
        compiler_params=_params("parallel", "arbitrary"),
        name="peer_dense",
    )(h2, peer_u, peer_v, brow, crow, r2, e2, x, mod_l)


def kernel(x_prompt, x_sample, cache_nat_k, cache_nat_v, cache_diff_k, cache_diff_v, c, c_ctx, w_ada, b_ada, norm1_g, norm2_g, w_in, pool_w, pool_scale, nat_q_g, nat_k_g, nat_rel_bias, conv_w, conv_b, conv_ln_g, conv_ln_b, diff_q_g, diff_k_g, diff_lambda_p, diff_subln_g, w_branch, w_gate, b_gate, w_out, peer_w_query, peer_sub_keys, peer_u, peer_v):
    x = jnp.concatenate([x_prompt.reshape(N_CTX, D_MODEL), x_sample.reshape(N_LAT, D_MODEL)], axis=0)
    cvec = jnp.concatenate([c_ctx[None, :], c, jnp.zeros((8 - 1 - DEC_BATCH, D_MODEL), F32)], axis=0)
    mod = _modulation(cvec, w_ada, b_ada).reshape(DEPTH, 8, 6, D_MODEL)

    gid = jnp.arange(CB) // NAT_HD
    gmat = (gid[:, None] == gid[None, :]).astype(BF16)
    rope = _rope_tables(T_PRE)
    ck_n = cache_nat_k.reshape(DEC_BATCH, DEPTH, PAST_LEN, BRANCH_W)
    cv_n = cache_nat_v.reshape(DEC_BATCH, DEPTH, PAST_LEN, BRANCH_W)
    ck_d = cache_diff_k.reshape(DEC_BATCH, DEPTH, PAST_LEN, BRANCH_W)
    cv_d = cache_diff_v.reshape(DEC_BATCH, DEPTH, PAST_LEN, BRANCH_W)

    states = []
    for l in range(DEPTH):
        gg = jnp.stack([jnp.tile(nat_q_g[l], NAT_HEADS), jnp.tile(nat_k_g[l], NAT_HEADS),
                        jnp.tile(diff_q_g[l], 2 * DIFF_HEADS), jnp.tile(diff_k_g[l], 2 * DIFF_HEADS)])
        h1, zp, zn = _pre(x, mod[l], norm1_g[l].reshape(1, -1), w_in[l].astype(BF16),
                          gg.reshape(len(NORM_ZB), 1, CB), gmat, rope)

        y_pool, y_conv = _seq_mixers(zp, pool_w[l], pool_scale[l], conv_w[l], conv_b[l],
                                     conv_ln_g[l], conv_ln_b[l])
        y_nat_ctx = _nat_ctx(zp, zn)
        y_nat_lat = _nat_lat(zp, zn, ck_n, cv_n, *_nat_bias_table(nat_rel_bias[l]), l)
        y_diff_ctx = _diff_ctx(zp, zn, diff_lambda_p[l], diff_subln_g[l], l)
        y_diff_lat = _diff_lat(zp, zn, ck_d, cv_d, diff_lambda_p[l], diff_subln_g[l], l)

        x, h2, qry = _merge(h1, w_gate[l].astype(BF16), b_gate[l].reshape(1, -1), y_pool, y_nat_ctx, y_nat_lat,
                            y_conv, y_diff_ctx, y_diff_lat, x, mod[l], norm2_g[l].reshape(1, -1),
                            w_branch[l].astype(BF16), w_out[l].astype(BF16), peer_w_query[l].astype(BF16))
        r2, e2, brow, crow = _peer_select(qry, peer_sub_keys[l])
        x = _peer_dense(h2, peer_u, peer_v, l, r2, e2, brow, crow, x, mod[l])

        states.append([zn[:N_CTX, Q_NK * CB:(Q_NK + 1) * CB], zp[:N_CTX, P_NV * CB:(P_NV + 1) * CB],
                       zn[:N_CTX, Q_DK * CB:(Q_DK + 1) * CB], zp[:N_CTX, P_DV * CB:(P_DV + 1) * CB]])

    def stack(idx, shape):
        return jnp.stack([states[l][idx].reshape((BATCH, SEQ) + shape) for l in range(DEPTH)], axis=1)

    return (x[:N_CTX].reshape(BATCH, SEQ, D_MODEL),
            x[N_CTX:].reshape(DEC_BATCH, DEC_SEQ, D_MODEL),
            stack(0, (NAT_HEADS, NAT_HD)),
            stack(1, (NAT_HEADS, NAT_HD)),
            stack(2, (DIFF_HEADS, 2, DIFF_HD)),
            stack(3, (DIFF_HEADS, 2 * DIFF_HD)))
```

```python
import functools
import math

import jax
import jax.numpy as jnp
from jax import lax
from jax.experimental import pallas as pl
from jax.experimental.pallas import tpu as pltpu

F32 = jnp.float32
BF16 = jnp.bfloat16

D_MODEL = 1024
BATCH = 16
SEQ = 256
DEPTH = 2
DEC_BATCH = 4
DEC_SEQ = 2048
PAST_LEN = 256
GRID_W = 64
BRANCH_W = 512
POOL_WINDOWS = (2, 4, 8, 16)
POOL_GROUP = 128
NAT_HEADS = 8
NAT_HD = 64
NAT_WIN_R = 8
NAT_WIN_C = 16
CONV_W = 31
DIFF_HEADS = 4
DIFF_HD = 64
PEER_HEADS = 8
PEER_NKEYS = 128
PEER_N = PEER_NKEYS * PEER_NKEYS
PEER_TOPK = 16
ROPE_BASE = 10000.0
EPS = 1e-6
NEG_INF = -1e30
LOG2E = math.log2(math.e)

N_CTX = BATCH * SEQ
N_LAT = DEC_BATCH * DEC_SEQ
N_TOK = N_CTX + N_LAT
GRID_ROWS = DEC_SEQ // GRID_W

CB = 512
ZB_POOL, ZB_NQ, ZB_NK, ZB_NV, ZB_CA, ZB_CGATE, ZB_DQ, ZB_DK, ZB_DV = range(9)

V7X_VMEM_LIMIT = 52 * 1024 * 1024

T_PRE = 1024
T_SEQ = 256
HALO = 16
T_MERGE = 512
T_SEL = 512
T_PEER = 512
E_CHUNK = 1024
LANE = 128


def _sigmoid(x):
    return 1.0 / (1.0 + jnp.exp(-x))


def _dot(a, b):
    return jnp.dot(a, b, preferred_element_type=F32)


def _dot_nt(a, b):
    return lax.dot_general(a, b, (((1,), (1,)), ((), ())), preferred_element_type=F32)


def _dot_tn(a, b):
    return lax.dot_general(a, b, (((0,), (0,)), ((), ())), preferred_element_type=F32)


def _split_bf16(a):
    hi = a.astype(BF16)
    lo = (a - hi.astype(F32)).astype(BF16)
    return hi, lo


def _params(*sem):
    return pltpu.CompilerParams(dimension_semantics=sem, vmem_limit_bytes=V7X_VMEM_LIMIT)


def _mod_row(i, tile):
    nct = N_CTX // tile
    per = DEC_SEQ // tile
    return jnp.where(i < nct, 0, 1 + (i - nct) // per)


def _mod_kernel(c_ref, w_ref, b_ref, o_ref):
    c = c_ref[...]
    a = c * _sigmoid(c)
    w = w_ref[0]
    a_hi, a_lo = _split_bf16(a)
    w_hi, w_lo = _split_bf16(w)
    o_ref[0] = _dot(a_hi, w_hi) + _dot(a_lo, w_hi) + _dot(a_hi, w_lo) + b_ref[0]


def _modulation(cvec, w_ada, b_ada):
    tn = 1024
    return pl.pallas_call(
        _mod_kernel,
        grid=(DEPTH, 6 * D_MODEL // tn),
        in_specs=[
            pl.BlockSpec((8, D_MODEL), lambda l, j: (0, 0)),
            pl.BlockSpec((1, D_MODEL, tn), lambda l, j: (l, 0, j)),
            pl.BlockSpec((1, 1, tn), lambda l, j: (l, 0, j)),
        ],
        out_specs=pl.BlockSpec((1, 8, tn), lambda l, j: (l, 0, j)),
        out_shape=jax.ShapeDtypeStruct((DEPTH, 8, 6 * D_MODEL), F32),
        compiler_params=_params("parallel", "parallel"),
        name="modulation",
    )(cvec, w_ada, b_ada.reshape(DEPTH, 1, 6 * D_MODEL))


def _tile4(t):
    return jnp.concatenate([t, t, t, t], axis=1)


def _hnorm_kernel(x_ref, mod_ref, g_ref, h_ref):
    x = x_ref[...]
    ms = jnp.mean(x * x, axis=-1, keepdims=True)
    y = x * lax.rsqrt(ms + EPS) * g_ref[...]
    h_ref[...] = (y * (1.0 + mod_ref[1:2, :]) + mod_ref[0:1, :]).astype(h_ref.dtype)


def _proj_plain_kernel(h_ref, w_ref, o_ref):
    o_ref[...] = _dot(h_ref[...], w_ref[...])


def _proj_norm_kernel(h_ref, w_ref, gg_ref, gmat_ref, rc_ref, rp_ref, rm_ref, o_ref):
    acc = _dot(h_ref[...], w_ref[...])
    hi, lo = _split_bf16(acc * acc)
    ss = _dot(hi, gmat_ref[...]) + _dot(lo, gmat_ref[...])
    y = acc * lax.rsqrt(ss * (1.0 / NAT_HD) + EPS) * gg_ref[0]
    o_ref[...] = (y * _tile4(rc_ref[...])
                  + pltpu.roll(y, 16, 1) * _tile4(rp_ref[...])
                  + pltpu.roll(y, CB - 16, 1) * _tile4(rm_ref[...]))


def _rope_tables(tile):
    quarter = DIFF_HD // 4
    t = jnp.arange(DEC_SEQ)
    lane = jnp.arange(LANE)
    d = lane % DIFF_HD
    freqs = ROPE_BASE ** (-jnp.arange(quarter, dtype=F32) / quarter)
    pos = jnp.where(d[None, :] < DIFF_HD // 2, (t // GRID_W)[:, None], (t % GRID_W)[:, None]).astype(F32)
    ang = pos * freqs[d % quarter][None, :]
    cos = jnp.cos(ang)
    sin = jnp.sin(ang)
    second = (d % (2 * quarter)) >= quarter
    s_plus = jnp.where(second[None, :], sin, 0.0)
    s_minus = jnp.where(second[None, :], 0.0, -sin)
    ones = jnp.ones((tile, LANE), F32)
    zeros = jnp.zeros((tile, LANE), F32)
    return (jnp.concatenate([cos, ones], 0), jnp.concatenate([s_plus, zeros], 0),
            jnp.concatenate([s_minus, zeros], 0))


PLAIN_ZB = (ZB_POOL, ZB_NV, ZB_CA, ZB_CGATE, ZB_DV)
NORM_ZB = (ZB_NQ, ZB_NK, ZB_DQ, ZB_DK)
P_POOL, P_NV, P_CA, P_CGATE, P_DV = range(5)
Q_NQ, Q_NK, Q_DQ, Q_DK = range(4)


def _pick(j, values):
    out = values[0]
    for n, v in enumerate(values[1:], start=1):
        out = jnp.where(j == n, v, out)
    return out


def _pre(x, mod_l, norm_g, w_in, gg, gmat, rope):
    nct = N_CTX // T_PRE
    per = DEC_SEQ // T_PRE
    ident_blk = DEC_SEQ // T_PRE
    nt = N_TOK // T_PRE
    h = pl.pallas_call(
        _hnorm_kernel,
        grid=(nt,),
        in_specs=[pl.BlockSpec((T_PRE, D_MODEL), lambda i: (i, 0)),
                  pl.BlockSpec((None, 6, D_MODEL), lambda i: (_mod_row(i, T_PRE), 0, 0)),
                  pl.BlockSpec((1, D_MODEL), lambda i: (0, 0))],
        out_specs=pl.BlockSpec((T_PRE, D_MODEL), lambda i: (i, 0)),
        out_shape=jax.ShapeDtypeStruct((N_TOK, D_MODEL), BF16),
        compiler_params=_params("parallel"),
        name="hnorm",
    )(x, mod_l, norm_g)

    hspec = pl.BlockSpec((T_PRE, D_MODEL), lambda i, j: (i, 0))
    ospec = pl.BlockSpec((T_PRE, CB), lambda i, j: (i, j))

    zp = pl.pallas_call(
        _proj_plain_kernel,
        grid=(nt, len(PLAIN_ZB)),
        in_specs=[hspec, pl.BlockSpec((D_MODEL, CB), lambda i, j: (0, _pick(j, PLAIN_ZB)))],
        out_specs=ospec,
        out_shape=jax.ShapeDtypeStruct((N_TOK, len(PLAIN_ZB) * CB), F32),
        compiler_params=_params("parallel", "arbitrary"),
        name="proj_plain",
    )(h, w_in)

    def rope_idx(i, j):
        use = ((j == Q_DQ) | (j == Q_DK)) & (i >= nct)
        return jnp.where(use, (i - nct) % per, ident_blk), 0

    rope_spec = pl.BlockSpec((T_PRE, LANE), rope_idx)
    zn = pl.pallas_call(
        _proj_norm_kernel,
        grid=(nt, len(NORM_ZB)),
        in_specs=[hspec, pl.BlockSpec((D_MODEL, CB), lambda i, j: (0, _pick(j, NORM_ZB))),
                  pl.BlockSpec((1, 1, CB), lambda i, j: (j, 0, 0)),
                  pl.BlockSpec((CB, CB), lambda i, j: (0, 0)),
                  rope_spec, rope_spec, rope_spec],
        out_specs=ospec,
        out_shape=jax.ShapeDtypeStruct((N_TOK, len(NORM_ZB) * CB), F32),
        compiler_params=_params("parallel", "arbitrary"),
        name="proj_norm",
    )(h, w_in, gg, gmat, *rope)

    return h, zp, zn


def _seq_kernel(zp_ref, zp_prev, zp_next, za_ref, za_prev, za_next, zb_ref, zb_prev, zb_next,
                pw_ref, ps_ref, cw_ref, cb_ref, lg_ref, lb_ref,
                yp_ref, yc_ref, pbuf, ubuf, ybuf, shbuf):
    i = pl.program_id(0)
    nct = N_CTX // T_SEQ
    per = DEC_SEQ // T_SEQ
    is_lat = i >= nct
    kk = (i - nct) % per
    has_prev = is_lat & (kk != 0)
    has_next = is_lat & (kk != per - 1)
    pos0 = jnp.where(is_lat, kk * T_SEQ, 0)
    seq_len = jnp.where(is_lat, DEC_SEQ, SEQ)
    fp = jnp.where(has_prev, 1.0, 0.0).astype(F32)
    fn = jnp.where(has_next, 1.0, 0.0).astype(F32)

    pbuf[0:HALO, :] = zp_prev[...] * fp
    pbuf[HALO:HALO + T_SEQ, :] = zp_ref[...]
    pbuf[HALO + T_SEQ:, :] = zp_next[...] * fn
    ubuf[0:HALO, :] = za_prev[...] * _sigmoid(zb_prev[...]) * fp
    ubuf[HALO:HALO + T_SEQ, :] = za_ref[...] * _sigmoid(zb_ref[...])
    ubuf[HALO + T_SEQ:, :] = za_next[...] * _sigmoid(zb_next[...]) * fn

    pos = pos0 + lax.broadcasted_iota(jnp.int32, (T_SEQ, LANE), 0)
    for gi, w in enumerate(POOL_WINDOWS):
        ls = slice(gi * LANE, (gi + 1) * LANE)
        acc = None
        for s in range(-(w // 2), w - w // 2):
            v = pbuf[HALO + s:HALO + s + T_SEQ, ls]
            acc = v if acc is None else acc + v
        lo = jnp.maximum(pos - w // 2, 0)
        hi = jnp.minimum(pos + (w - w // 2), seq_len)
        mean = acc / (hi - lo).astype(F32)
        d = mean - pbuf[HALO:HALO + T_SEQ, ls]
        y = _dot(d.astype(BF16), pw_ref[gi].astype(BF16)) * ps_ref[:, ls]
        yp_ref[:, ls] = y.astype(yp_ref.dtype)

    base = HALO - CONV_W // 2
    span = shbuf.shape[0]
    for b in range(8):
        shbuf[...] = ubuf[base + b:base + b + span, :]
        for c in range(BRANCH_W // LANE):
            ls = slice(c * LANE, (c + 1) * LANE)
            acc = cb_ref[:, ls] if b == 0 else ybuf[:, ls]
            for k in range(b, CONV_W, 8):
                acc = acc + shbuf[k - b:k - b + T_SEQ, ls] * cw_ref[k:k + 1, ls]
            ybuf[:, ls] = acc
    y = ybuf[...]
    mu = jnp.mean(y, axis=-1, keepdims=True)
    yc = y - mu
    var = jnp.mean(yc * yc, axis=-1, keepdims=True)
    yn = yc * lax.rsqrt(var + EPS) * lg_ref[...] + lb_ref[...]
    yc_ref[...] = (yn * _sigmoid(yn)).astype(yc_ref.dtype)


def _seq_mixers(zp, pool_w, pool_scale, conv_w, conv_b, ln_g, ln_b):
    nt = N_TOK // T_SEQ
    hb = T_SEQ // HALO
    n_hblk = N_TOK // HALO

    def specs(cb):
        return [
            pl.BlockSpec((T_SEQ, CB), lambda i: (i, cb)),
            pl.BlockSpec((HALO, CB), lambda i: (jnp.maximum(i * hb - 1, 0), cb)),
            pl.BlockSpec((HALO, CB), lambda i: (jnp.minimum((i + 1) * hb, n_hblk - 1), cb)),
        ]

    def full(shape):
        return pl.BlockSpec(shape, lambda i: (0,) * len(shape))

    cw = jnp.concatenate([conv_w, jnp.zeros((1, BRANCH_W), F32)], axis=0)
    out_spec = pl.BlockSpec((T_SEQ, BRANCH_W), lambda i: (i, 0))
    return pl.pallas_call(
        _seq_kernel,
        grid=(nt,),
        in_specs=(specs(P_POOL) + specs(P_CA) + specs(P_CGATE)
                  + [full((4, POOL_GROUP, POOL_GROUP)), full((1, BRANCH_W)), full((CONV_W + 1, BRANCH_W)),
                     full((1, BRANCH_W)), full((1, BRANCH_W)), full((1, BRANCH_W))]),
        out_specs=[out_spec, out_spec],
        out_shape=[jax.ShapeDtypeStruct((N_TOK, BRANCH_W), BF16)] * 2,
        scratch_shapes=[pltpu.VMEM((T_SEQ + 2 * HALO, BRANCH_W), F32),
                        pltpu.VMEM((T_SEQ + 2 * HALO, BRANCH_W), F32),
                        pltpu.VMEM((T_SEQ, BRANCH_W), F32),
                        pltpu.VMEM((T_SEQ + 8 * ((CONV_W - 1) // 8), BRANCH_W), F32)],
        compiler_params=_params("parallel"),
        name="pool_conv",
    )(zp, zp, zp, zp, zp, zp, zp, zp, zp, pool_w, pool_scale.reshape(1, -1), cw,
      conv_b.reshape(1, -1), ln_g.reshape(1, -1), ln_b.reshape(1, -1))


def _nat_ctx_kernel(q_ref, k_ref, v_ref, o_ref):
    for h in range(NAT_HEADS):
        sl = slice(h * NAT_HD, (h + 1) * NAT_HD)
        q = (q_ref[:, sl] * (NAT_HD ** -0.5 * LOG2E)).astype(BF16)
        s = _dot_nt(q, k_ref[:, sl].astype(BF16))
        m = jnp.max(s, axis=-1, keepdims=True)
        e = jnp.exp2(s - m)
        den = jnp.sum(e, axis=-1, keepdims=True)
        o = _dot(e.astype(BF16), v_ref[:, sl].astype(BF16)) / den
        o_ref[:, sl] = o.astype(o_ref.dtype)


def _nat_ctx(zp, zn):
    def spec(cb):
        return pl.BlockSpec((SEQ, CB), lambda b: (b, cb))

    return pl.pallas_call(
        _nat_ctx_kernel,
        grid=(BATCH,),
        in_specs=[spec(Q_NQ), spec(Q_NK), spec(P_NV)],
        out_specs=pl.BlockSpec((SEQ, BRANCH_W), lambda b: (b, 0)),
        out_shape=jax.ShapeDtypeStruct((N_CTX, BRANCH_W), BF16),
        compiler_params=_params("parallel"),
        name="nat_ctx",
    )(zn, zn, zp)


NAT_QROWS = 4
NAT_KROWS = 12


def _nat_row0(r):
    return jnp.clip(r - NAT_WIN_R // 2, 0, GRID_ROWS - NAT_WIN_R)


def _nat_lat_kernel(q_ref, k_ref, v_ref, ck_ref, cv_ref, bias_ref, o_ref):
    g = pl.program_id(1)
    w0 = jnp.minimum(_nat_row0(g * NAT_QROWS), GRID_ROWS - NAT_KROWS)
    start = pl.multiple_of(w0 * GRID_W, GRID_W)
    nwin = NAT_KROWS * GRID_W
    for h in range(NAT_HEADS):
        sl = slice(h * NAT_HD, (h + 1) * NAT_HD)
        q = (q_ref[:, sl] * (NAT_HD ** -0.5 * LOG2E)).astype(BF16)
        kw = k_ref[pl.ds(start, nwin), sl].astype(BF16)
        vw = v_ref[pl.ds(start, nwin), sl].astype(BF16)
        s_loc = _dot_nt(q, kw) + bias_ref[h]
        s_ctx = _dot_nt(q, ck_ref[:, sl].astype(BF16))
        m = jnp.maximum(jnp.max(s_loc, axis=-1, keepdims=True), jnp.max(s_ctx, axis=-1, keepdims=True))
        e_loc = jnp.exp2(s_loc - m)
        e_ctx = jnp.exp2(s_ctx - m)
        den = jnp.sum(e_loc, axis=-1, keepdims=True) + jnp.sum(e_ctx, axis=-1, keepdims=True)
        o = (_dot(e_loc.astype(BF16), vw) + _dot(e_ctx.astype(BF16), cv_ref[:, sl].astype(BF16))) / den
        o_ref[:, sl] = o.astype(o_ref.dtype)


def _nat_bias_kernel(rb_ref, oh_ref, o_ref):
    x = rb_ref[...]
    hi = x.astype(BF16)
    r1 = x - hi.astype(F32)
    mid = r1.astype(BF16)
    lo = (r1 - mid.astype(F32)).astype(BF16)
    oh = oh_ref[...]
    o_ref[...] = _dot(hi, oh) + _dot(mid, oh) + _dot(lo, oh)


def _nat_bias_table(rel_bias):
    ndr = 2 * NAT_WIN_R - 1
    ndc = 2 * NAT_WIN_C - 1
    q = jnp.arange(GRID_W)
    kc = jnp.arange(GRID_W)
    dcol = jnp.clip(kc[None, :] - q[:, None] + NAT_WIN_C - 1, 0, ndc - 1)
    wstart = jnp.clip(q - NAT_WIN_C // 2, 0, GRID_W - NAT_WIN_C)
    valid = (kc[None, :] >= wstart[:, None]) & (kc[None, :] < wstart[:, None] + NAT_WIN_C)
    d = jnp.arange(LANE)
    onehot = jnp.where(d[:, None, None] == ndc, jnp.logical_not(valid)[None],
                       (d[:, None, None] == dcol[None]) & valid[None])
    onehot = onehot.reshape(LANE, GRID_W * GRID_W).astype(BF16)
    nrow = NAT_HEADS * ndr
    rb = jnp.concatenate([rel_bias.reshape(nrow, ndc).astype(F32) * LOG2E, jnp.full((nrow, 1), NEG_INF, F32),
                          jnp.zeros((nrow, LANE - ndc - 1), F32)], axis=1)
    rb = jnp.concatenate([rb, jnp.zeros((LANE - nrow, LANE), F32)], axis=0)
    tcol = pl.pallas_call(
        _nat_bias_kernel,
        out_shape=jax.ShapeDtypeStruct((LANE, GRID_W * GRID_W), F32),
        compiler_params=pltpu.CompilerParams(vmem_limit_bytes=V7X_VMEM_LIMIT),
        name="nat_bias",
    )(rb, onehot)
    tcol = tcol[:nrow].reshape(NAT_HEADS, ndr, GRID_W, GRID_W)
    neg = jnp.full((NAT_HEADS, GRID_W, GRID_W), NEG_INF, F32)
    tables, variant_of_group, seen = [], [], {}
    for g in range(GRID_ROWS // NAT_QROWS):
        r_first = g * NAT_QROWS
        w0 = min(max(r_first - NAT_WIN_R // 2, 0), GRID_ROWS - NAT_WIN_R, GRID_ROWS - NAT_KROWS)
        rows = [(r_first + j - w0, min(max(r_first + j - NAT_WIN_R // 2, 0), GRID_ROWS - NAT_WIN_R) - w0)
                for j in range(NAT_QROWS)]
        key = tuple(rows)
        if key not in seen:
            seen[key] = len(tables)
            blocks = []
            for rq, rw in rows:
                blocks.append(jnp.concatenate(
                    [tcol[:, i - rq + NAT_WIN_R - 1] if rw <= i < rw + NAT_WIN_R else neg
                     for i in range(NAT_KROWS)], axis=-1))
            tables.append(jnp.concatenate(blocks, axis=1))
        variant_of_group.append(seen[key])
    return jnp.stack(tables, axis=0), tuple(variant_of_group)


def _nat_lat(zp, zn, cache_k, cache_v, bias_tab, variant_of_group, layer):
    tq = NAT_QROWS * GRID_W
    ngroups = GRID_ROWS // NAT_QROWS
    q_blk0 = N_CTX // tq
    kv_blk0 = N_CTX // DEC_SEQ
    cache_spec = pl.BlockSpec((None, None, PAST_LEN, BRANCH_W), lambda b, g: (b, layer, 0, 0))
    return pl.pallas_call(
        _nat_lat_kernel,
        grid=(DEC_BATCH, ngroups),
        in_specs=[
            pl.BlockSpec((tq, CB), lambda b, g: (q_blk0 + b * ngroups + g, Q_NQ)),
            pl.BlockSpec((DEC_SEQ, CB), lambda b, g: (kv_blk0 + b, Q_NK)),
            pl.BlockSpec((DEC_SEQ, CB), lambda b, g: (kv_blk0 + b, P_NV)),
            cache_spec, cache_spec,
            pl.BlockSpec((None, NAT_HEADS, tq, NAT_KROWS * GRID_W),
                         lambda b, g: (_pick(g, variant_of_group), 0, 0, 0)),
        ],
        out_specs=pl.BlockSpec((tq, BRANCH_W), lambda b, g: (b * ngroups + g, 0)),
        out_shape=jax.ShapeDtypeStruct((N_LAT, BRANCH_W), BF16),
        compiler_params=_params("parallel", "arbitrary"),
        name="nat_lat",
    )(zn, zn, zp, cache_k, cache_v, bias_tab)


def _diff_kernel(has_cache, lam_init, *refs):
    if has_cache:
        q_ref, k_ref, v_ref, ck_ref, cv_ref, lamp_ref, g_ref, o_ref = refs
    else:
        q_ref, k_ref, v_ref, lamp_ref, g_ref, o_ref = refs
    lp = lamp_ref[...]
    lam = (jnp.exp(jnp.sum(lp[0:1] * lp[1:2], axis=-1, keepdims=True))
           - jnp.exp(jnp.sum(lp[2:3] * lp[3:4], axis=-1, keepdims=True)) + lam_init)
    hv = 2 * DIFF_HD
    for h in range(DIFF_HEADS):
        vs = slice(h * hv, (h + 1) * hv)
        vb = v_ref[:, vs].astype(BF16)
        if has_cache:
            cvb = cv_ref[:, vs].astype(BF16)
        parts = []
        for i in range(2):
            sl = slice(h * hv + i * DIFF_HD, h * hv + (i + 1) * DIFF_HD)
            q = (q_ref[:, sl] * (DIFF_HD ** -0.5 * LOG2E)).astype(BF16)
            s = _dot_nt(q, k_ref[:, sl].astype(BF16))
            m = jnp.max(s, axis=-1, keepdims=True)
            if has_cache:
                sc = _dot_nt(q, ck_ref[:, sl].astype(BF16))
                m = jnp.maximum(m, jnp.max(sc, axis=-1, keepdims=True))
            e = jnp.exp2(s - m)
            den = jnp.sum(e, axis=-1, keepdims=True)
            o = _dot(e.astype(BF16), vb)
            if has_cache:
                ec = jnp.exp2(sc - m)
                den = den + jnp.sum(ec, axis=-1, keepdims=True)
                o = o + _dot(ec.astype(BF16), cvb)
            parts.append((o, 1.0 / den))
        (o1, r1), (o2, r2) = parts
        o = o1 * r1 - o2 * (lam * r2)
        ms = jnp.mean(o * o, axis=-1, keepdims=True)
        y = o * lax.rsqrt(ms + EPS) * g_ref[...] * (1.0 - lam_init)
        o_ref[:, vs] = y.astype(o_ref.dtype)


def _lam_init(layer):
    return 0.8 - 0.6 * math.exp(-0.3 * layer)


def _diff_ctx(zp, zn, lam_p, subln_g, layer):
    def spec(cb):
        return pl.BlockSpec((SEQ, CB), lambda b: (b, cb))

    return pl.pallas_call(
        functools.partial(_diff_kernel, False, _lam_init(layer)),
        grid=(BATCH,),
        in_specs=[spec(Q_DQ), spec(Q_DK), spec(P_DV),
                  pl.BlockSpec((4, DIFF_HD), lambda b: (0, 0)),
                  pl.BlockSpec((1, 2 * DIFF_HD), lambda b: (0, 0))],
        out_specs=pl.BlockSpec((SEQ, BRANCH_W), lambda b: (b, 0)),
        out_shape=jax.ShapeDtypeStruct((N_CTX, BRANCH_W), BF16),
        compiler_params=_params("parallel"),
        name="diff_ctx",
    )(zn, zn, zp, lam_p, subln_g.reshape(1, -1))


T_DQ = 512


def _diff_lat(zp, zn, cache_k, cache_v, lam_p, subln_g, layer):
    nq = DEC_SEQ // T_DQ
    q_blk0 = N_CTX // T_DQ
    kv_blk0 = N_CTX // DEC_SEQ
    cache_spec = pl.BlockSpec((None, None, PAST_LEN, BRANCH_W), lambda b, t: (b, layer, 0, 0))
    return pl.pallas_call(
        functools.partial(_diff_kernel, True, _lam_init(layer)),
        grid=(DEC_BATCH, nq),
        in_specs=[
            pl.BlockSpec((T_DQ, CB), lambda b, t: (q_blk0 + b * nq + t, Q_DQ)),
            pl.BlockSpec((DEC_SEQ, CB), lambda b, t: (kv_blk0 + b, Q_DK)),
            pl.BlockSpec((DEC_SEQ, CB), lambda b, t: (kv_blk0 + b, P_DV)),
            cache_spec, cache_spec,
            pl.BlockSpec((4, DIFF_HD), lambda b, t: (0, 0)),
            pl.BlockSpec((1, 2 * DIFF_HD), lambda b, t: (0, 0)),
        ],
        out_specs=pl.BlockSpec((T_DQ, BRANCH_W), lambda b, t: (b * nq + t, 0)),
        out_shape=jax.ShapeDtypeStruct((N_LAT, BRANCH_W), BF16),
        compiler_params=_params("parallel", "arbitrary"),
        name="diff_lat",
    )(zn, zn, zp, cache_k, cache_v, lam_p, subln_g.reshape(1, -1))


def _merge_kernel(h_ref, wg_ref, bg_ref, yp_ref, ync_ref, ynl_ref, yc_ref, ydc_ref, ydl_ref, x_ref, mod_ref,
                  g2_ref, wb_ref, wo_ref, wq_ref, xo_ref, h2_ref, q_ref, merged_scr):
    br = pl.program_id(1)
    is_ctx = pl.program_id(0) < N_CTX // T_MERGE
    y_nat = jnp.where(is_ctx, ync_ref[...], ynl_ref[...])
    y_diff = jnp.where(is_ctx, ydc_ref[...], ydl_ref[...])
    y = jnp.where(br == 0, yp_ref[...], jnp.where(br == 1, y_nat, jnp.where(br == 2, yc_ref[...], y_diff)))
    gate = _sigmoid(_dot(h_ref[...], wg_ref[...]) + bg_ref[...])
    t = gate * _dot(y, wb_ref[...])

    @pl.when(br == 0)
    def _():
        merged_scr[...] = t

    @pl.when(br > 0)
    def _():
        merged_scr[...] += t

    @pl.when(br == pl.num_programs(1) - 1)
    def _():
        out = _dot(merged_scr[...].astype(BF16), wo_ref[...])
        x = x_ref[...] + mod_ref[2:3, :] * out
        xo_ref[...] = x
        ms = jnp.mean(x * x, axis=-1, keepdims=True)
        h = x * lax.rsqrt(ms + EPS) * g2_ref[...] * (1.0 + mod_ref[4:5, :]) + mod_ref[3:4, :]
        hb = h.astype(BF16)
        h2_ref[...] = hb
        q_ref[...] = _dot(hb, wq_ref[...])


def _merge(h, wg, bg, y_pool, y_nat_ctx, y_nat_lat, y_conv, y_diff_ctx, y_diff_lat, x, mod_l, norm2_g, wb, wo, wq):
    nct = N_CTX // T_MERGE
    nbr = wb.shape[0]
    yspec = pl.BlockSpec((T_MERGE, BRANCH_W), lambda i, br: (i, 0))
    cspec = pl.BlockSpec((T_MERGE, BRANCH_W), lambda i, br: (jnp.minimum(i, nct - 1), 0))
    lspec = pl.BlockSpec((T_MERGE, BRANCH_W), lambda i, br: (jnp.maximum(i - nct, 0), 0))
    tspec = pl.BlockSpec((T_MERGE, D_MODEL), lambda i, br: (i, 0))
    qcols = wq.shape[1]
    return pl.pallas_call(
        _merge_kernel,
        grid=(N_TOK // T_MERGE, nbr),
        in_specs=[tspec,
                  pl.BlockSpec((D_MODEL, D_MODEL), lambda i, br: (0, br)),
                  pl.BlockSpec((1, D_MODEL), lambda i, br: (0, br)),
                  yspec, cspec, lspec, yspec, cspec, lspec,
                  tspec,
                  pl.BlockSpec((None, 6, D_MODEL), lambda i, br: (_mod_row(i, T_MERGE), 0, 0)),
                  pl.BlockSpec((1, D_MODEL), lambda i, br: (0, 0)),
                  pl.BlockSpec((None, BRANCH_W, D_MODEL), lambda i, br: (br, 0, 0)),
                  pl.BlockSpec((D_MODEL, D_MODEL), lambda i, br: (0, 0)),
                  pl.BlockSpec((D_MODEL, qcols), lambda i, br: (0, 0))],
        out_specs=[tspec, tspec, pl.BlockSpec((T_MERGE, qcols), lambda i, br: (i, 0))],
        out_shape=[jax.ShapeDtypeStruct((N_TOK, D_MODEL), F32),
                   jax.ShapeDtypeStruct((N_TOK, D_MODEL), BF16),
                   jax.ShapeDtypeStruct((N_TOK, qcols), F32)],
        scratch_shapes=[pltpu.VMEM((T_MERGE, D_MODEL), F32)],
        compiler_params=_params("parallel", "arbitrary"),
        name="merge",
    )(h, wg, bg, y_pool, y_nat_ctx, y_nat_lat, y_conv, y_diff_ctx, y_diff_lat, x, mod_l, norm2_g, wb, wo, wq)


SUBLANES = 8


def _merge_exchange_pairs(n):
    pairs = []
    t = (n - 1).bit_length()
    p = 1 << (t - 1)
    while p > 0:
        q, r, d = 1 << (t - 1), 0, p
        while d > 0:
            pairs += [(i, i + d) for i in range(n - d) if (i & p) == r]
            d, q, r = q - p, q >> 1, p
        p >>= 1
    return pairs


_SORT16 = _merge_exchange_pairs(PEER_TOPK)


def _cmpx(tiles, i, j):
    a, b = tiles[i], tiles[j]
    if b is None:
        return
    if a is None:
        tiles[i], tiles[j] = b, None
        return
    tiles[i], tiles[j] = jnp.maximum(a, b), jnp.minimum(a, b)


def _top16_sorted(s):
    n = PEER_TOPK
    tiles = [s[j * SUBLANES:(j + 1) * SUBLANES, :] for j in range(s.shape[0] // SUBLANES)]
    tiles += [None] * (n - len(tiles))
    for i, j in _SORT16:
        _cmpx(tiles, i, j)
    for shift in (4, 2, 1):
        merged = []
        for i in range(n):
            a, b = tiles[i], tiles[n - 1 - i]
            b = None if b is None else pltpu.roll(b, shift, 0)
            merged.append(b if a is None else a if b is None else jnp.maximum(a, b))
        tiles = merged
        d = n // 2
        while d > 0:
            for i in range(n):
                if (i & d) == 0:
                    _cmpx(tiles, i, i + d)
            d //= 2
    return tiles


def _rows_to_sublanes(tiles):
    rid = lax.broadcasted_iota(jnp.int32, (SUBLANES, LANE), 0)
    halves = []
    for base in (0, SUBLANES):
        out = tiles[base]
        for k in range(1, SUBLANES):
            out = jnp.where(rid == k, tiles[base + k], out)
        halves.append(out)
    return jnp.concatenate(halves, axis=0)


N_CAND = 16 + 7 * 8 + 8


def _peer_select_kernel(q_ref, sk_ref, r2_ref, e2_ref, brow_ref, crow_ref, chosen_scr):
    q = q_ref[...].astype(BF16)
    half = PEER_NKEYS
    s1_all = _dot_nt(sk_ref[0].astype(BF16), q[:, :half])
    s2_all = _dot_nt(sk_ref[1].astype(BF16), q[:, half:])
    cid = lax.broadcasted_iota(jnp.int32, (N_CAND, LANE), 0)
    rid8 = lax.broadcasted_iota(jnp.int32, (8, LANE), 0)
    ntile = PEER_NKEYS // SUBLANES
    for c in range(T_SEL // LANE):
        ls = slice(c * LANE, (c + 1) * LANE)
        s1 = s1_all[:, ls]
        s2 = s2_all[:, ls]
        t1 = _top16_sorted(s1)
        t2 = _top16_sorted(s2)
        v1 = _rows_to_sublanes(t1)
        v2 = _rows_to_sublanes(t2)
        cand = jnp.concatenate([v1[0:1] + v2] + [v1[a:a + 1] + v2[0:8] for a in range(1, 8)]
                               + [v1[8:16] + v2[0:1]], axis=0)
        ctop = _top16_sorted(cand)
        zsum = jnp.zeros((1, LANE), F32)
        for k in range(PEER_TOPK):
            zsum = zsum + jnp.exp(ctop[k][0:1] - ctop[0][0:1])
        fast = jnp.where(cand >= ctop[PEER_TOPK - 1][0:1], 1.0, 0.0)
        chosen_scr[...] = fast
        n_fast = jnp.sum(fast, axis=0, keepdims=True)
        tied = jnp.sum(jnp.where(n_fast != float(PEER_TOPK), 1.0, 0.0)) > 0.0

        @pl.when(tied)
        def _():
            rest = cand
            walk = jnp.zeros((N_CAND, LANE), F32)
            for k in range(PEER_TOPK):
                m = jnp.max(rest, axis=0, keepdims=True)
                first = jnp.min(jnp.where(rest == m, cid, N_CAND), axis=0, keepdims=True)
                hit = cid == first
                walk = jnp.where(hit, 1.0, walk)
                rest = jnp.where(hit, -jnp.inf, rest)
            chosen_scr[...] = walk

        chosen = chosen_scr[...]
        cnt_lo = jnp.zeros((8, LANE), F32)
        cnt_lo = jnp.where(rid8 == 0, jnp.sum(chosen[0:16], axis=0, keepdims=True), cnt_lo)
        for a in range(1, 8):
            cnt_lo = jnp.where(rid8 == a, jnp.sum(chosen[8 + 8 * a:16 + 8 * a], axis=0, keepdims=True), cnt_lo)
        cnt = jnp.concatenate([cnt_lo, chosen[N_CAND - 8:N_CAND]], axis=0)
        cnt_rows = [jnp.broadcast_to(cnt[a:a + 1], (SUBLANES, LANE)) for a in range(PEER_TOPK)]
        inv_z = 1.0 / zsum
        for jj in range(ntile // 2):
            ranks, e2s = [], []
            for j in (2 * jj, 2 * jj + 1):
                rows = slice(j * SUBLANES, (j + 1) * SUBLANES)
                d1 = s1[rows]
                d2 = s2[rows]
                brow = jnp.zeros((SUBLANES, LANE), F32)
                rank2 = jnp.zeros((SUBLANES, LANE), F32)
                for a in range(PEER_TOPK):
                    brow = jnp.where(d1 == t1[a], cnt_rows[a], brow)
                    rank2 = jnp.where(t2[a] > d2, float(a + 1), rank2)
                brow_ref[rows, ls] = brow
                crow_ref[rows, ls] = jnp.exp(d1 - t1[0]) * inv_z
                ranks.append(rank2)
                e2s.append(jnp.exp(d2 - t2[0]))
            rows16 = slice(jj * 2 * SUBLANES, (jj + 1) * 2 * SUBLANES)
            r2_ref[rows16, ls] = jnp.concatenate(ranks, axis=0).astype(BF16)
            e2_ref[rows16, ls] = jnp.concatenate(e2s, axis=0).astype(BF16)


def _peer_select(qry, sub_keys):
    nt = N_TOK // T_SEL
    kspec = pl.BlockSpec((None, PEER_NKEYS, T_SEL), lambda i, h: (h, 0, i))

    def kshape(dt):
        return jax.ShapeDtypeStruct((PEER_HEADS, PEER_NKEYS, N_TOK), dt)

    return pl.pallas_call(
        _peer_select_kernel,
        grid=(nt, PEER_HEADS),
        in_specs=[pl.BlockSpec((T_SEL, 2 * PEER_NKEYS), lambda i, h: (i, h)),
                  pl.BlockSpec((None, 2, PEER_NKEYS, PEER_NKEYS), lambda i, h: (h, 0, 0, 0))],
        out_specs=[kspec, kspec, kspec, kspec],
        out_shape=[kshape(BF16), kshape(BF16), kshape(F32), kshape(F32)],
        scratch_shapes=[pltpu.VMEM((N_CAND, LANE), F32)],
        compiler_params=_params("parallel", "arbitrary"),
        name="peer_select",
    )(qry, sub_keys)


E_PAIR = 2 * PEER_NKEYS
BF16_ROWS = 16


def _row_bf16(row):
    return jnp.broadcast_to(row, (BF16_ROWS, LANE)).astype(BF16)


def _gelu(x):
    return 0.5 * x * (1.0 + lax.erf(x * (2.0 ** -0.5)))


def _peer_dense_kernel(h_ref, u_ref, v_ref, brow_ref, crow_ref, r2_ref, e2_ref, x_ref, mod_ref,
                       o_ref, acc_ref, a_scr, p_scr):
    c = pl.program_id(1)

    @pl.when(c == 0)
    def _():
        acc_ref[...] = jnp.zeros_like(acc_ref)

    hb = h_ref[...]
    zero = jnp.zeros((BF16_ROWS, LANE), BF16)
    npair = E_CHUNK // E_PAIR
    for j in range(npair + 1):
        slot = j % 2
        if j < npair:
            a_scr[slot] = _dot_nt(u_ref[j * E_PAIR:(j + 1) * E_PAIR, :].astype(BF16), hb)
        if j > 0:
            acc_ref[...] += _dot_tn(p_scr[1 - slot], v_ref[(j - 1) * E_PAIR:j * E_PAIR, :].astype(BF16))
        if j == npair:
            break
        for half in range(2):
            n1l = 2 * j + half
            for tc in range(T_PEER // LANE):
                ls = slice(tc * LANE, (tc + 1) * LANE)
                b16 = [_row_bf16(brow_ref[h, n1l:n1l + 1, ls]) for h in range(PEER_HEADS)]
                c16 = [_row_bf16(crow_ref[h, n1l:n1l + 1, ls]) for h in range(PEER_HEADS)]
                e0 = half * PEER_NKEYS
                act = _gelu(a_scr[slot, e0:e0 + PEER_NKEYS, ls]).astype(BF16)
                for rg in range(PEER_NKEYS // BF16_ROWS):
                    rs = slice(rg * BF16_ROWS, (rg + 1) * BF16_ROWS)
                    g = None
                    for h in range(PEER_HEADS):
                        t = jnp.where(r2_ref[h, rs, ls] < b16[h], e2_ref[h, rs, ls], zero) * c16[h]
                        g = t if g is None else g + t
                    p_scr[slot, e0 + rg * BF16_ROWS:e0 + (rg + 1) * BF16_ROWS, ls] = g * act[rs]

    @pl.when(c == pl.num_programs(1) - 1)
    def _():
        o_ref[...] = x_ref[...] + mod_ref[5:6, :] * acc_ref[...]


def _peer_dense(h2, peer_u, peer_v, layer, r2, e2, brow, crow, x, mod_l):
    nt = N_TOK // T_PEER
    nc = PEER_N // E_CHUNK
    n1c = E_CHUNK // PEER_NKEYS
    rowspec = pl.BlockSpec((PEER_HEADS, n1c, T_PEER), lambda i, c: (0, c, i))
    fullspec = pl.BlockSpec((PEER_HEADS, PEER_NKEYS, T_PEER), lambda i, c: (0, 0, i))
    return pl.pallas_call(
        _peer_dense_kernel,
        grid=(nt, nc),
        in_specs=[pl.BlockSpec((T_PEER, D_MODEL), lambda i, c: (i, 0)),
                  pl.BlockSpec((None, E_CHUNK, D_MODEL), lambda i, c: (layer, c, 0)),
                  pl.BlockSpec((None, E_CHUNK, D_MODEL), lambda i, c: (layer, c, 0)),
                  rowspec, rowspec, fullspec, fullspec,
                  pl.BlockSpec((T_PEER, D_MODEL), lambda i, c: (i, 0)),
                  pl.BlockSpec((None, 6, D_MODEL), lambda i, c: (_mod_row(i, T_PEER), 0, 0))],
        out_specs=pl.BlockSpec((T_PEER, D_MODEL), lambda i, c: (i, 0)),
        out_shape=jax.ShapeDtypeStruct((N_TOK, D_MODEL), F32),
        scratch_shapes=[pltpu.VMEM((T_PEER, D_MODEL), F32),
                        pltpu.VMEM((2, E_PAIR, T_PEER), F32),
                        pltpu.VMEM((2, E_PAIR, T_PEER), BF16)],
        compiler_params=_params("parallel", "arbitrary"),
        name="peer_dense",
    )(h2, peer_u, peer_v, brow, crow, r2, e2, x, mod_l)


def kernel(x_prompt, x_sample, cache_nat_k, cache_nat_v, cache_diff_k, cache_diff_v, c, c_ctx, w_ada, b_ada, norm1_g, norm2_g, w_in, pool_w, pool_scale, nat_q_g, nat_k_g, nat_rel_bias, conv_w, conv_b, conv_ln_g, conv_ln_b, diff_q_g, diff_k_g, diff_lambda_p, diff_subln_g, w_branch, w_gate, b_gate, w_out, peer_w_query, peer_sub_keys, peer_u, peer_v):
    x = jnp.concatenate([x_prompt.reshape(N_CTX, D_MODEL), x_sample.reshape(N_LAT, D_MODEL)], axis=0)
    cvec = jnp.concatenate([c_ctx[None, :], c, jnp.zeros((8 - 1 - DEC_BATCH, D_MODEL), F32)], axis=0)
    mod = _modulation(cvec, w_ada, b_ada).reshape(DEPTH, 8, 6, D_MODEL)

    gid = jnp.arange(CB) // NAT_HD
    gmat = (gid[:, None] == gid[None, :]).astype(BF16)
    rope = _rope_tables(T_PRE)
    ck_n = cache_nat_k.reshape(DEC_BATCH, DEPTH, PAST_LEN, BRANCH_W)
    cv_n = cache_nat_v.reshape(DEC_BATCH, DEPTH, PAST_LEN, BRANCH_W)
    ck_d = cache_diff_k.reshape(DEC_BATCH, DEPTH, PAST_LEN, BRANCH_W)
    cv_d = cache_diff_v.reshape(DEC_BATCH, DEPTH, PAST_LEN, BRANCH_W)

    states = []
    for l in range(DEPTH):
        gg = jnp.stack([jnp.tile(nat_q_g[l], NAT_HEADS), jnp.tile(nat_k_g[l], NAT_HEADS),
                        jnp.tile(diff_q_g[l], 2 * DIFF_HEADS), jnp.tile(diff_k_g[l], 2 * DIFF_HEADS)])
        h1, zp, zn = _pre(x, mod[l], norm1_g[l].reshape(1, -1), w_in[l].astype(BF16),
                          gg.reshape(len(NORM_ZB), 1, CB), gmat, rope)

        y_pool, y_conv = _seq_mixers(zp, pool_w[l], pool_scale[l], conv_w[l], conv_b[l],
                                     conv_ln_g[l], conv_ln_b[l])
        y_nat_ctx = _nat_ctx(zp, zn)
        y_nat_lat = _nat_lat(zp, zn, ck_n, cv_n, *_nat_bias_table(nat_rel_bias[l]), l)
        y_diff_ctx = _diff_ctx(zp, zn, diff_lambda_p[l], diff_subln_g[l], l)
        y_diff_lat = _diff_lat(zp, zn, ck_d, cv_d, diff_lambda_p[l], diff_subln_g[l], l)

        x, h2, qry = _merge(h1, w_gate[l].astype(BF16), b_gate[l].reshape(1, -1), y_pool, y_nat_ctx, y_nat_lat,
                            y_conv, y_diff_ctx, y_diff_lat, x, mod[l], norm2_g[l].reshape(1, -1),
                            w_branch[l].astype(BF16), w_out[l].astype(BF16), peer_w_query[l].astype(BF16))
        r2, e2, brow, crow = _peer_select(qry, peer_sub_keys[l])
        x = _peer_dense(h2, peer_u, peer_v, l, r2, e2, brow, crow, x, mod[l])

        states.append([zn[:N_CTX, Q_NK * CB:(Q_NK + 1) * CB], zp[:N_CTX, P_NV * CB:(P_NV + 1) * CB],
                       zn[:N_CTX, Q_DK * CB:(Q_DK + 1) * CB], zp[:N_CTX, P_DV * CB:(P_DV + 1) * CB]])

    def stack(idx, shape):
        return jnp.stack([states[l][idx].reshape((BATCH, SEQ) + shape) for l in range(DEPTH)], axis=1)

    return (x[:N_CTX].reshape(BATCH, SEQ, D_MODEL),
            x[N_CTX:].reshape(DEC_BATCH, DEC_SEQ, D_MODEL),
            stack(0, (NAT_HEADS, NAT_HD)),
            stack(1, (NAT_HEADS, NAT_HD)),
            stack(2, (DIFF_HEADS, 2, DIFF_HD)),
            stack(3, (DIFF_HEADS, 2 * DIFF_HD)))
```

```python
import functools
import math

import jax
import jax.numpy as jnp
from jax import lax
from jax.experimental import pallas as pl
from jax.experimental.pallas import tpu as pltpu

F32 = jnp.float32
BF16 = jnp.bfloat16

D_MODEL = 1024
BATCH = 16
SEQ = 256
DEPTH = 2
DEC_BATCH = 4
DEC_SEQ = 2048
PAST_LEN = 256
GRID_W = 64
BRANCH_W = 512
POOL_WINDOWS = (2, 4, 8, 16)
POOL_GROUP = 128
NAT_HEADS = 8
NAT_HD = 64
NAT_WIN_R = 8
NAT_WIN_C = 16
CONV_W = 31
DIFF_HEADS = 4
DIFF_HD = 64
PEER_HEADS = 8
PEER_NKEYS = 128
PEER_N = PEER_NKEYS * PEER_NKEYS
PEER_TOPK = 16
ROPE_BASE = 10000.0
EPS = 1e-6
NEG_INF = -1e30
LOG2E = math.log2(math.e)

N_CTX = BATCH * SEQ
N_LAT = DEC_BATCH * DEC_SEQ
N_TOK = N_CTX + N_LAT
GRID_ROWS = DEC_SEQ // GRID_W

CB = 512
ZB_POOL, ZB_NQ, ZB_NK, ZB_NV, ZB_CA, ZB_CGATE, ZB_DQ, ZB_DK, ZB_DV = range(9)

V7X_VMEM_LIMIT = 52 * 1024 * 1024

T_PRE = 2048
T_SEQ = 256
HALO = 16
T_MERGE = 512
T_SEL = 1024
T_PEER = 512
E_CHUNK = 1024
LANE = 128


def _sigmoid(x):
    return 1.0 / (1.0 + jnp.exp(-x))


def _dot(a, b):
    return jnp.dot(a, b, preferred_element_type=F32)


def _dot_nt(a, b):
    return lax.dot_general(a, b, (((1,), (1,)), ((), ())), preferred_element_type=F32)


def _dot_tn(a, b):
    return lax.dot_general(a, b, (((0,), (0,)), ((), ())), preferred_element_type=F32)


def _split_bf16(a):
    hi = a.astype(BF16)
    lo = (a - hi.astype(F32)).astype(BF16)
    return hi, lo


def _params(*sem):
    return pltpu.CompilerParams(dimension_semantics=sem, vmem_limit_bytes=V7X_VMEM_LIMIT)


def _mod_row(i, tile):
    nct = N_CTX // tile
    per = DEC_SEQ // tile
    return jnp.where(i < nct, 0, 1 + (i - nct) // per)


def _mod_kernel(c_ref, w_ref, b_ref, o_ref):
    c = c_ref[...]
    a = c * _sigmoid(c)
    w = w_ref[0]
    a_hi, a_lo = _split_bf16(a)
    w_hi, w_lo = _split_bf16(w)
    o_ref[0] = _dot(a_hi, w_hi) + _dot(a_lo, w_hi) + _dot(a_hi, w_lo) + b_ref[0]


def _modulation(cvec, w_ada, b_ada):
    tn = 1024
    return pl.pallas_call(
        _mod_kernel,
        grid=(DEPTH, 6 * D_MODEL // tn),
        in_specs=[
            pl.BlockSpec((8, D_MODEL), lambda l, j: (0, 0)),
            pl.BlockSpec((1, D_MODEL, tn), lambda l, j: (l, 0, j)),
            pl.BlockSpec((1, 1, tn), lambda l, j: (l, 0, j)),
        ],
        out_specs=pl.BlockSpec((1, 8, tn), lambda l, j: (l, 0, j)),
        out_shape=jax.ShapeDtypeStruct((DEPTH, 8, 6 * D_MODEL), F32),
        compiler_params=_params("parallel", "parallel"),
        name="modulation",
    )(cvec, w_ada, b_ada.reshape(DEPTH, 1, 6 * D_MODEL))


def _tile4(t):
    return jnp.concatenate([t, t, t, t], axis=1)


def _hnorm_kernel(x_ref, mod_ref, g_ref, h_ref):
    x = x_ref[...]
    ms = jnp.mean(x * x, axis=-1, keepdims=True)
    y = x * lax.rsqrt(ms + EPS) * g_ref[...]
    h_ref[...] = (y * (1.0 + mod_ref[1:2, :]) + mod_ref[0:1, :]).astype(h_ref.dtype)


def _proj_plain_kernel(h_ref, w_ref, o_ref):
    o_ref[...] = _dot(h_ref[...], w_ref[...])


def _proj_norm_kernel(h_ref, w_ref, gg_ref, gmat_ref, rc_ref, rp_ref, rm_ref, o_ref):
    acc = _dot(h_ref[...], w_ref[...])
    hi, lo = _split_bf16(acc * acc)
    ss = _dot(hi, gmat_ref[...]) + _dot(lo, gmat_ref[...])
    y = acc * lax.rsqrt(ss * (1.0 / NAT_HD) + EPS) * gg_ref[0]
    o_ref[...] = (y * _tile4(rc_ref[...])
                  + pltpu.roll(y, 16, 1) * _tile4(rp_ref[...])
                  + pltpu.roll(y, CB - 16, 1) * _tile4(rm_ref[...]))


def _rope_tables(tile):
    quarter = DIFF_HD // 4
    t = jnp.arange(DEC_SEQ)
    lane = jnp.arange(LANE)
    d = lane % DIFF_HD
    freqs = ROPE_BASE ** (-jnp.arange(quarter, dtype=F32) / quarter)
    pos = jnp.where(d[None, :] < DIFF_HD // 2, (t // GRID_W)[:, None], (t % GRID_W)[:, None]).astype(F32)
    ang = pos * freqs[d % quarter][None, :]
    cos = jnp.cos(ang)
    sin = jnp.sin(ang)
    second = (d % (2 * quarter)) >= quarter
    s_plus = jnp.where(second[None, :], sin, 0.0)
    s_minus = jnp.where(second[None, :], 0.0, -sin)
    ones = jnp.ones((tile, LANE), F32)
    zeros = jnp.zeros((tile, LANE), F32)
    return (jnp.concatenate([cos, ones], 0), jnp.concatenate([s_plus, zeros], 0),
            jnp.concatenate([s_minus, zeros], 0))


PLAIN_ZB = (ZB_POOL, ZB_NV, ZB_CA, ZB_CGATE, ZB_DV)
NORM_ZB = (ZB_NQ, ZB_NK, ZB_DQ, ZB_DK)
P_POOL, P_NV, P_CA, P_CGATE, P_DV = range(5)
Q_NQ, Q_NK, Q_DQ, Q_DK = range(4)


def _pick(j, values):
    out = values[0]
    for n, v in enumerate(values[1:], start=1):
        out = jnp.where(j == n, v, out)
    return out


def _pre(x, mod_l, norm_g, w_in, gg, gmat, rope):
    nct = N_CTX // T_PRE
    per = DEC_SEQ // T_PRE
    ident_blk = DEC_SEQ // T_PRE
    nt = N_TOK // T_PRE
    h = pl.pallas_call(
        _hnorm_kernel,
        grid=(nt,),
        in_specs=[pl.BlockSpec((T_PRE, D_MODEL), lambda i: (i, 0)),
                  pl.BlockSpec((None, 6, D_MODEL), lambda i: (_mod_row(i, T_PRE), 0, 0)),
                  pl.BlockSpec((1, D_MODEL), lambda i: (0, 0))],
        out_specs=pl.BlockSpec((T_PRE, D_MODEL), lambda i: (i, 0)),
        out_shape=jax.ShapeDtypeStruct((N_TOK, D_MODEL), BF16),
        compiler_params=_params("parallel"),
        name="hnorm",
    )(x, mod_l, norm_g)

    hspec = pl.BlockSpec((T_PRE, D_MODEL), lambda i, j: (i, 0))
    ospec = pl.BlockSpec((T_PRE, CB), lambda i, j: (i, j))

    zp = pl.pallas_call(
        _proj_plain_kernel,
        grid=(nt, len(PLAIN_ZB)),
        in_specs=[hspec, pl.BlockSpec((D_MODEL, CB), lambda i, j: (0, _pick(j, PLAIN_ZB)))],
        out_specs=ospec,
        out_shape=jax.ShapeDtypeStruct((N_TOK, len(PLAIN_ZB) * CB), F32),
        compiler_params=_params("parallel", "arbitrary"),
        name="proj_plain",
    )(h, w_in)

    def rope_idx(i, j):
        use = ((j == Q_DQ) | (j == Q_DK)) & (i >= nct)
        return jnp.where(use, (i - nct) % per, ident_blk), 0

    rope_spec = pl.BlockSpec((T_PRE, LANE), rope_idx)
    zn = pl.pallas_call(
        _proj_norm_kernel,
        grid=(nt, len(NORM_ZB)),
        in_specs=[hspec, pl.BlockSpec((D_MODEL, CB), lambda i, j: (0, _pick(j, NORM_ZB))),
                  pl.BlockSpec((1, 1, CB), lambda i, j: (j, 0, 0)),
                  pl.BlockSpec((CB, CB), lambda i, j: (0, 0)),
                  rope_spec, rope_spec, rope_spec],
        out_specs=ospec,
        out_shape=jax.ShapeDtypeStruct((N_TOK, len(NORM_ZB) * CB), F32),
        compiler_params=_params("parallel", "arbitrary"),
        name="proj_norm",
    )(h, w_in, gg, gmat, *rope)

    return h, zp, zn


def _seq_kernel(zp_ref, zp_prev, zp_next, za_ref, za_prev, za_next, zb_ref, zb_prev, zb_next,
                pw_ref, ps_ref, cw_ref, cb_ref, lg_ref, lb_ref,
                yp_ref, yc_ref, pbuf, ubuf, ybuf, shbuf):
    i = pl.program_id(0)
    nct = N_CTX // T_SEQ
    per = DEC_SEQ // T_SEQ
    is_lat = i >= nct
    kk = (i - nct) % per
    has_prev = is_lat & (kk != 0)
    has_next = is_lat & (kk != per - 1)
    pos0 = jnp.where(is_lat, kk * T_SEQ, 0)
    seq_len = jnp.where(is_lat, DEC_SEQ, SEQ)
    fp = jnp.where(has_prev, 1.0, 0.0).astype(F32)
    fn = jnp.where(has_next, 1.0, 0.0).astype(F32)

    pbuf[0:HALO, :] = zp_prev[...] * fp
    pbuf[HALO:HALO + T_SEQ, :] = zp_ref[...]
    pbuf[HALO + T_SEQ:, :] = zp_next[...] * fn
    ubuf[0:HALO, :] = za_prev[...] * _sigmoid(zb_prev[...]) * fp
    ubuf[HALO:HALO + T_SEQ, :] = za_ref[...] * _sigmoid(zb_ref[...])
    ubuf[HALO + T_SEQ:, :] = za_next[...] * _sigmoid(zb_next[...]) * fn

    pos = pos0 + lax.broadcasted_iota(jnp.int32, (T_SEQ, LANE), 0)
    for gi, w in enumerate(POOL_WINDOWS):
        ls = slice(gi * LANE, (gi + 1) * LANE)
        acc = None
        for s in range(-(w // 2), w - w // 2):
            v = pbuf[HALO + s:HALO + s + T_SEQ, ls]
            acc = v if acc is None else acc + v
        lo = jnp.maximum(pos - w // 2, 0)
        hi = jnp.minimum(pos + (w - w // 2), seq_len)
        mean = acc / (hi - lo).astype(F32)
        d = mean - pbuf[HALO:HALO + T_SEQ, ls]
        y = _dot(d.astype(BF16), pw_ref[gi].astype(BF16)) * ps_ref[:, ls]
        yp_ref[:, ls] = y.astype(yp_ref.dtype)

    base = HALO - CONV_W // 2
    span = shbuf.shape[0]
    for b in range(8):
        shbuf[...] = ubuf[base + b:base + b + span, :]
        for c in range(BRANCH_W // LANE):
            ls = slice(c * LANE, (c + 1) * LANE)
            acc = cb_ref[:, ls] if b == 0 else ybuf[:, ls]
            for k in range(b, CONV_W, 8):
                acc = acc + shbuf[k - b:k - b + T_SEQ, ls] * cw_ref[k:k + 1, ls]
            ybuf[:, ls] = acc
    y = ybuf[...]
    mu = jnp.mean(y, axis=-1, keepdims=True)
    yc = y - mu
    var = jnp.mean(yc * yc, axis=-1, keepdims=True)
    yn = yc * lax.rsqrt(var + EPS) * lg_ref[...] + lb_ref[...]
    yc_ref[...] = (yn * _sigmoid(yn)).astype(yc_ref.dtype)


def _seq_mixers(zp, pool_w, pool_scale, conv_w, conv_b, ln_g, ln_b):
    nt = N_TOK // T_SEQ
    hb = T_SEQ // HALO
    n_hblk = N_TOK // HALO

    def specs(cb):
        return [
            pl.BlockSpec((T_SEQ, CB), lambda i: (i, cb)),
            pl.BlockSpec((HALO, CB), lambda i: (jnp.maximum(i * hb - 1, 0), cb)),
            pl.BlockSpec((HALO, CB), lambda i: (jnp.minimum((i + 1) * hb, n_hblk - 1), cb)),
        ]

    def full(shape):
        return pl.BlockSpec(shape, lambda i: (0,) * len(shape))

    cw = jnp.concatenate([conv_w, jnp.zeros((1, BRANCH_W), F32)], axis=0)
    out_spec = pl.BlockSpec((T_SEQ, BRANCH_W), lambda i: (i, 0))
    return pl.pallas_call(
        _seq_kernel,
        grid=(nt,),
        in_specs=(specs(P_POOL) + specs(P_CA) + specs(P_CGATE)
                  + [full((4, POOL_GROUP, POOL_GROUP)), full((1, BRANCH_W)), full((CONV_W + 1, BRANCH_W)),
                     full((1, BRANCH_W)), full((1, BRANCH_W)), full((1, BRANCH_W))]),
        out_specs=[out_spec, out_spec],
        out_shape=[jax.ShapeDtypeStruct((N_TOK, BRANCH_W), BF16)] * 2,
        scratch_shapes=[pltpu.VMEM((T_SEQ + 2 * HALO, BRANCH_W), F32),
                        pltpu.VMEM((T_SEQ + 2 * HALO, BRANCH_W), F32),
                        pltpu.VMEM((T_SEQ, BRANCH_W), F32),
                        pltpu.VMEM((T_SEQ + 8 * ((CONV_W - 1) // 8), BRANCH_W), F32)],
        compiler_params=_params("parallel"),
        name="pool_conv",
    )(zp, zp, zp, zp, zp, zp, zp, zp, zp, pool_w, pool_scale.reshape(1, -1), cw,
      conv_b.reshape(1, -1), ln_g.reshape(1, -1), ln_b.reshape(1, -1))


def _nat_ctx_kernel(q_ref, k_ref, v_ref, o_ref):
    for h in range(NAT_HEADS):
        sl = slice(h * NAT_HD, (h + 1) * NAT_HD)
        q = (q_ref[:, sl] * (NAT_HD ** -0.5 * LOG2E)).astype(BF16)
        s = _dot_nt(q, k_ref[:, sl].astype(BF16))
        m = jnp.max(s, axis=-1, keepdims=True)
        e = jnp.exp2(s - m)
        den = jnp.sum(e, axis=-1, keepdims=True)
        o = _dot(e.astype(BF16), v_ref[:, sl].astype(BF16)) / den
        o_ref[:, sl] = o.astype(o_ref.dtype)


def _nat_ctx(zp, zn):
    def spec(cb):
        return pl.BlockSpec((SEQ, CB), lambda b: (b, cb))

    return pl.pallas_call(
        _nat_ctx_kernel,
        grid=(BATCH,),
        in_specs=[spec(Q_NQ), spec(Q_NK), spec(P_NV)],
        out_specs=pl.BlockSpec((SEQ, BRANCH_W), lambda b: (b, 0)),
        out_shape=jax.ShapeDtypeStruct((N_CTX, BRANCH_W), BF16),
        compiler_params=_params("parallel"),
        name="nat_ctx",
    )(zn, zn, zp)


NAT_QROWS = 4
NAT_KROWS = 12


def _nat_row0(r):
    return jnp.clip(r - NAT_WIN_R // 2, 0, GRID_ROWS - NAT_WIN_R)


def _nat_lat_kernel(q_ref, k_ref, v_ref, ck_ref, cv_ref, bias_ref, o_ref):
    g = pl.program_id(1)
    w0 = jnp.minimum(_nat_row0(g * NAT_QROWS), GRID_ROWS - NAT_KROWS)
    start = pl.multiple_of(w0 * GRID_W, GRID_W)
    nwin = NAT_KROWS * GRID_W
    for h in range(NAT_HEADS):
        sl = slice(h * NAT_HD, (h + 1) * NAT_HD)
        q = (q_ref[:, sl] * (NAT_HD ** -0.5 * LOG2E)).astype(BF16)
        kw = k_ref[pl.ds(start, nwin), sl].astype(BF16)
        vw = v_ref[pl.ds(start, nwin), sl].astype(BF16)
        s_loc = _dot_nt(q, kw) + bias_ref[h]
        s_ctx = _dot_nt(q, ck_ref[:, sl].astype(BF16))
        m = jnp.maximum(jnp.max(s_loc, axis=-1, keepdims=True), jnp.max(s_ctx, axis=-1, keepdims=True))
        e_loc = jnp.exp2(s_loc - m)
        e_ctx = jnp.exp2(s_ctx - m)
        den = jnp.sum(e_loc, axis=-1, keepdims=True) + jnp.sum(e_ctx, axis=-1, keepdims=True)
        o = (_dot(e_loc.astype(BF16), vw) + _dot(e_ctx.astype(BF16), cv_ref[:, sl].astype(BF16))) / den
        o_ref[:, sl] = o.astype(o_ref.dtype)


def _nat_bias_kernel(rb_ref, oh_ref, o_ref):
    x = rb_ref[...]
    hi = x.astype(BF16)
    r1 = x - hi.astype(F32)
    mid = r1.astype(BF16)
    lo = (r1 - mid.astype(F32)).astype(BF16)
    oh = oh_ref[...]
    o_ref[...] = _dot(hi, oh) + _dot(mid, oh) + _dot(lo, oh)


def _nat_bias_table(rel_bias):
    ndr = 2 * NAT_WIN_R - 1
    ndc = 2 * NAT_WIN_C - 1
    q = jnp.arange(GRID_W)
    kc = jnp.arange(GRID_W)
    dcol = jnp.clip(kc[None, :] - q[:, None] + NAT_WIN_C - 1, 0, ndc - 1)
    wstart = jnp.clip(q - NAT_WIN_C // 2, 0, GRID_W - NAT_WIN_C)
    valid = (kc[None, :] >= wstart[:, None]) & (kc[None, :] < wstart[:, None] + NAT_WIN_C)
    d = jnp.arange(LANE)
    onehot = jnp.where(d[:, None, None] == ndc, jnp.logical_not(valid)[None],
                       (d[:, None, None] == dcol[None]) & valid[None])
    onehot = onehot.reshape(LANE, GRID_W * GRID_W).astype(BF16)
    nrow = NAT_HEADS * ndr
    rb = jnp.concatenate([rel_bias.reshape(nrow, ndc).astype(F32) * LOG2E, jnp.full((nrow, 1), NEG_INF, F32),
                          jnp.zeros((nrow, LANE - ndc - 1), F32)], axis=1)
    rb = jnp.concatenate([rb, jnp.zeros((LANE - nrow, LANE), F32)], axis=0)
    tcol = pl.pallas_call(
        _nat_bias_kernel,
        out_shape=jax.ShapeDtypeStruct((LANE, GRID_W * GRID_W), F32),
        compiler_params=pltpu.CompilerParams(vmem_limit_bytes=V7X_VMEM_LIMIT),
        name="nat_bias",
    )(rb, onehot)
    tcol = tcol[:nrow].reshape(NAT_HEADS, ndr, GRID_W, GRID_W)
    neg = jnp.full((NAT_HEADS, GRID_W, GRID_W), NEG_INF, F32)
    tables, variant_of_group, seen = [], [], {}
    for g in range(GRID_ROWS // NAT_QROWS):
        r_first = g * NAT_QROWS
        w0 = min(max(r_first - NAT_WIN_R // 2, 0), GRID_ROWS - NAT_WIN_R, GRID_ROWS - NAT_KROWS)
        rows = [(r_first + j - w0, min(max(r_first + j - NAT_WIN_R // 2, 0), GRID_ROWS - NAT_WIN_R) - w0)
                for j in range(NAT_QROWS)]
        key = tuple(rows)
        if key not in seen:
            seen[key] = len(tables)
            blocks = []
            for rq, rw in rows:
                blocks.append(jnp.concatenate(
                    [tcol[:, i - rq + NAT_WIN_R - 1] if rw <= i < rw + NAT_WIN_R else neg
                     for i in range(NAT_KROWS)], axis=-1))
            tables.append(jnp.concatenate(blocks, axis=1))
        variant_of_group.append(seen[key])
    return jnp.stack(tables, axis=0), tuple(variant_of_group)


def _nat_lat(zp, zn, cache_k, cache_v, bias_tab, variant_of_group, layer):
    tq = NAT_QROWS * GRID_W
    ngroups = GRID_ROWS // NAT_QROWS
    q_blk0 = N_CTX // tq
    kv_blk0 = N_CTX // DEC_SEQ
    cache_spec = pl.BlockSpec((None, None, PAST_LEN, BRANCH_W), lambda b, g: (b, layer, 0, 0))
    return pl.pallas_call(
        _nat_lat_kernel,
        grid=(DEC_BATCH, ngroups),
        in_specs=[
            pl.BlockSpec((tq, CB), lambda b, g: (q_blk0 + b * ngroups + g, Q_NQ)),
            pl.BlockSpec((DEC_SEQ, CB), lambda b, g: (kv_blk0 + b, Q_NK)),
            pl.BlockSpec((DEC_SEQ, CB), lambda b, g: (kv_blk0 + b, P_NV)),
            cache_spec, cache_spec,
            pl.BlockSpec((None, NAT_HEADS, tq, NAT_KROWS * GRID_W),
                         lambda b, g: (_pick(g, variant_of_group), 0, 0, 0)),
        ],
        out_specs=pl.BlockSpec((tq, BRANCH_W), lambda b, g: (b * ngroups + g, 0)),
        out_shape=jax.ShapeDtypeStruct((N_LAT, BRANCH_W), BF16),
        compiler_params=_params("parallel", "arbitrary"),
        name="nat_lat",
    )(zn, zn, zp, cache_k, cache_v, bias_tab)


def _diff_kernel(has_cache, lam_init, *refs):
    if has_cache:
        q_ref, k_ref, v_ref, ck_ref, cv_ref, lamp_ref, g_ref, o_ref = refs
    else:
        q_ref, k_ref, v_ref, lamp_ref, g_ref, o_ref = refs
    lp = lamp_ref[...]
    lam = (jnp.exp(jnp.sum(lp[0:1] * lp[1:2], axis=-1, keepdims=True))
           - jnp.exp(jnp.sum(lp[2:3] * lp[3:4], axis=-1, keepdims=True)) + lam_init)
    hv = 2 * DIFF_HD
    for h in range(DIFF_HEADS):
        vs = slice(h * hv, (h + 1) * hv)
        vb = v_ref[:, vs].astype(BF16)
        if has_cache:
            cvb = cv_ref[:, vs].astype(BF16)
        parts = []
        for i in range(2):
            sl = slice(h * hv + i * DIFF_HD, h * hv + (i + 1) * DIFF_HD)
            q = (q_ref[:, sl] * (DIFF_HD ** -0.5 * LOG2E)).astype(BF16)
            s = _dot_nt(q, k_ref[:, sl].astype(BF16))
            m = jnp.max(s, axis=-1, keepdims=True)
            if has_cache:
                sc = _dot_nt(q, ck_ref[:, sl].astype(BF16))
                m = jnp.maximum(m, jnp.max(sc, axis=-1, keepdims=True))
            e = jnp.exp2(s - m)
            den = jnp.sum(e, axis=-1, keepdims=True)
            o = _dot(e.astype(BF16), vb)
            if has_cache:
                ec = jnp.exp2(sc - m)
                den = den + jnp.sum(ec, axis=-1, keepdims=True)
                o = o + _dot(ec.astype(BF16), cvb)
            parts.append((o, 1.0 / den))
        (o1, r1), (o2, r2) = parts
        o = o1 * r1 - o2 * (lam * r2)
        ms = jnp.mean(o * o, axis=-1, keepdims=True)
        y = o * lax.rsqrt(ms + EPS) * g_ref[...] * (1.0 - lam_init)
        o_ref[:, vs] = y.astype(o_ref.dtype)


def _lam_init(layer):
    return 0.8 - 0.6 * math.exp(-0.3 * layer)


def _diff_ctx(zp, zn, lam_p, subln_g, layer):
    def spec(cb):
        return pl.BlockSpec((SEQ, CB), lambda b: (b, cb))

    return pl.pallas_call(
        functools.partial(_diff_kernel, False, _lam_init(layer)),
        grid=(BATCH,),
        in_specs=[spec(Q_DQ), spec(Q_DK), spec(P_DV),
                  pl.BlockSpec((4, DIFF_HD), lambda b: (0, 0)),
                  pl.BlockSpec((1, 2 * DIFF_HD), lambda b: (0, 0))],
        out_specs=pl.BlockSpec((SEQ, BRANCH_W), lambda b: (b, 0)),
        out_shape=jax.ShapeDtypeStruct((N_CTX, BRANCH_W), BF16),
        compiler_params=_params("parallel"),
        name="diff_ctx",
    )(zn, zn, zp, lam_p, subln_g.reshape(1, -1))


T_DQ = 512


def _diff_lat(zp, zn, cache_k, cache_v, lam_p, subln_g, layer):
    nq = DEC_SEQ // T_DQ
    q_blk0 = N_CTX // T_DQ
    kv_blk0 = N_CTX // DEC_SEQ
    cache_spec = pl.BlockSpec((None, None, PAST_LEN, BRANCH_W), lambda b, t: (b, layer, 0, 0))
    return pl.pallas_call(
        functools.partial(_diff_kernel, True, _lam_init(layer)),
        grid=(DEC_BATCH, nq),
        in_specs=[
            pl.BlockSpec((T_DQ, CB), lambda b, t: (q_blk0 + b * nq + t, Q_DQ)),
            pl.BlockSpec((DEC_SEQ, CB), lambda b, t: (kv_blk0 + b, Q_DK)),
            pl.BlockSpec((DEC_SEQ, CB), lambda b, t: (kv_blk0 + b, P_DV)),
            cache_spec, cache_spec,
            pl.BlockSpec((4, DIFF_HD), lambda b, t: (0, 0)),
            pl.BlockSpec((1, 2 * DIFF_HD), lambda b, t: (0, 0)),
        ],
        out_specs=pl.BlockSpec((T_DQ, BRANCH_W), lambda b, t: (b * nq + t, 0)),
        out_shape=jax.ShapeDtypeStruct((N_LAT, BRANCH_W), BF16),
        compiler_params=_params("parallel", "arbitrary"),
        name="diff_lat",
    )(zn, zn, zp, cache_k, cache_v, lam_p, subln_g.reshape(1, -1))


def _merge_kernel(h_ref, wg_ref, bg_ref, yp_ref, ync_ref, ynl_ref, yc_ref, ydc_ref, ydl_ref, x_ref, mod_ref,
                  g2_ref, wb_ref, wo_ref, wq_ref, xo_ref, h2_ref, q_ref, merged_scr):
    br = pl.program_id(1)
    is_ctx = pl.program_id(0) < N_CTX // T_MERGE
    y_nat = jnp.where(is_ctx, ync_ref[...], ynl_ref[...])
    y_diff = jnp.where(is_ctx, ydc_ref[...], ydl_ref[...])
    y = jnp.where(br == 0, yp_ref[...], jnp.where(br == 1, y_nat, jnp.where(br == 2, yc_ref[...], y_diff)))
    gate = _sigmoid(_dot(h_ref[...], wg_ref[...]) + bg_ref[...])
    t = gate * _dot(y, wb_ref[...])

    @pl.when(br == 0)
    def _():
        merged_scr[...] = t

    @pl.when(br > 0)
    def _():
        merged_scr[...] += t

    @pl.when(br == pl.num_programs(1) - 1)
    def _():
        out = _dot(merged_scr[...].astype(BF16), wo_ref[...])
        x = x_ref[...] + mod_ref[2:3, :] * out
        xo_ref[...] = x
        ms = jnp.mean(x * x, axis=-1, keepdims=True)
        h = x * lax.rsqrt(ms + EPS) * g2_ref[...] * (1.0 + mod_ref[4:5, :]) + mod_ref[3:4, :]
        hb = h.astype(BF16)
        h2_ref[...] = hb
        q_ref[...] = _dot(hb, wq_ref[...])


def _merge(h, wg, bg, y_pool, y_nat_ctx, y_nat_lat, y_conv, y_diff_ctx, y_diff_lat, x, mod_l, norm2_g, wb, wo, wq):
    nct = N_CTX // T_MERGE
    nbr = wb.shape[0]
    yspec = pl.BlockSpec((T_MERGE, BRANCH_W), lambda i, br: (i, 0))
    cspec = pl.BlockSpec((T_MERGE, BRANCH_W), lambda i, br: (jnp.minimum(i, nct - 1), 0))
    lspec = pl.BlockSpec((T_MERGE, BRANCH_W), lambda i, br: (jnp.maximum(i - nct, 0), 0))
    tspec = pl.BlockSpec((T_MERGE, D_MODEL), lambda i, br: (i, 0))
    qcols = wq.shape[1]
    return pl.pallas_call(
        _merge_kernel,
        grid=(N_TOK // T_MERGE, nbr),
        in_specs=[tspec,
                  pl.BlockSpec((D_MODEL, D_MODEL), lambda i, br: (0, br)),
                  pl.BlockSpec((1, D_MODEL), lambda i, br: (0, br)),
                  yspec, cspec, lspec, yspec, cspec, lspec,
                  tspec,
                  pl.BlockSpec((None, 6, D_MODEL), lambda i, br: (_mod_row(i, T_MERGE), 0, 0)),
                  pl.BlockSpec((1, D_MODEL), lambda i, br: (0, 0)),
                  pl.BlockSpec((None, BRANCH_W, D_MODEL), lambda i, br: (br, 0, 0)),
                  pl.BlockSpec((D_MODEL, D_MODEL), lambda i, br: (0, 0)),
                  pl.BlockSpec((D_MODEL, qcols), lambda i, br: (0, 0))],
        out_specs=[tspec, tspec, pl.BlockSpec((T_MERGE, qcols), lambda i, br: (i, 0))],
        out_shape=[jax.ShapeDtypeStruct((N_TOK, D_MODEL), F32),
                   jax.ShapeDtypeStruct((N_TOK, D_MODEL), BF16),
                   jax.ShapeDtypeStruct((N_TOK, qcols), F32)],
        scratch_shapes=[pltpu.VMEM((T_MERGE, D_MODEL), F32)],
        compiler_params=_params("parallel", "arbitrary"),
        name="merge",
    )(h, wg, bg, y_pool, y_nat_ctx, y_nat_lat, y_conv, y_diff_ctx, y_diff_lat, x, mod_l, norm2_g, wb, wo, wq)


SUBLANES = 8


def _merge_exchange_pairs(n):
    pairs = []
    t = (n - 1).bit_length()
    p = 1 << (t - 1)
    while p > 0:
        q, r, d = 1 << (t - 1), 0, p
        while d > 0:
            pairs += [(i, i + d) for i in range(n - d) if (i & p) == r]
            d, q, r = q - p, q >> 1, p
        p >>= 1
    return pairs


_SORT16 = _merge_exchange_pairs(PEER_TOPK)


def _cmpx(tiles, i, j):
    a, b = tiles[i], tiles[j]
    if b is None:
        return
    if a is None:
        tiles[i], tiles[j] = b, None
        return
    tiles[i], tiles[j] = jnp.maximum(a, b), jnp.minimum(a, b)


def _top16_sorted(s):
    n = PEER_TOPK
    tiles = [s[j * SUBLANES:(j + 1) * SUBLANES, :] for j in range(s.shape[0] // SUBLANES)]
    tiles += [None] * (n - len(tiles))
    for i, j in _SORT16:
        _cmpx(tiles, i, j)
    for shift in (4, 2, 1):
        merged = []
        for i in range(n):
            a, b = tiles[i], tiles[n - 1 - i]
            b = None if b is None else pltpu.roll(b, shift, 0)
            merged.append(b if a is None else a if b is None else jnp.maximum(a, b))
        tiles = merged
        d = n // 2
        while d > 0:
            for i in range(n):
                if (i & d) == 0:
                    _cmpx(tiles, i, i + d)
            d //= 2
    return tiles


def _rows_to_sublanes(tiles):
    rid = lax.broadcasted_iota(jnp.int32, (SUBLANES, LANE), 0)
    halves = []
    for base in (0, SUBLANES):
        out = tiles[base]
        for k in range(1, SUBLANES):
            out = jnp.where(rid == k, tiles[base + k], out)
        halves.append(out)
    return jnp.concatenate(halves, axis=0)


N_CAND = 16 + 7 * 8 + 8


def _peer_select_kernel(q_ref, sk_ref, r2_ref, e2_ref, brow_ref, crow_ref, chosen_scr):
    q = q_ref[...].astype(BF16)
    half = PEER_NKEYS
    s1_all = _dot_nt(sk_ref[0].astype(BF16), q[:, :half])
    s2_all = _dot_nt(sk_ref[1].astype(BF16), q[:, half:])
    cid = lax.broadcasted_iota(jnp.int32, (N_CAND, LANE), 0)
    rid8 = lax.broadcasted_iota(jnp.int32, (8, LANE), 0)
    ntile = PEER_NKEYS // SUBLANES
    for c in range(T_SEL // LANE):
        ls = slice(c * LANE, (c + 1) * LANE)
        s1 = s1_all[:, ls]
        s2 = s2_all[:, ls]
        t1 = _top16_sorted(s1)
        t2 = _top16_sorted(s2)
        v1 = _rows_to_sublanes(t1)
        v2 = _rows_to_sublanes(t2)
        cand = jnp.concatenate([v1[0:1] + v2] + [v1[a:a + 1] + v2[0:8] for a in range(1, 8)]
                               + [v1[8:16] + v2[0:1]], axis=0)
        ctop = _top16_sorted(cand)
        zsum = jnp.zeros((1, LANE), F32)
        for k in range(PEER_TOPK):
            zsum = zsum + jnp.exp(ctop[k][0:1] - ctop[0][0:1])
        fast = jnp.where(cand >= ctop[PEER_TOPK - 1][0:1], 1.0, 0.0)
        chosen_scr[...] = fast
        n_fast = jnp.sum(fast, axis=0, keepdims=True)
        tied = jnp.sum(jnp.where(n_fast != float(PEER_TOPK), 1.0, 0.0)) > 0.0

        @pl.when(tied)
        def _():
            rest = cand
            walk = jnp.zeros((N_CAND, LANE), F32)
            for k in range(PEER_TOPK):
                m = jnp.max(rest, axis=0, keepdims=True)
                first = jnp.min(jnp.where(rest == m, cid, N_CAND), axis=0, keepdims=True)
                hit = cid == first
                walk = jnp.where(hit, 1.0, walk)
                rest = jnp.where(hit, -jnp.inf, rest)
            chosen_scr[...] = walk

        chosen = chosen_scr[...]
        cnt_lo = jnp.zeros((8, LANE), F32)
        cnt_lo = jnp.where(rid8 == 0, jnp.sum(chosen[0:16], axis=0, keepdims=True), cnt_lo)
        for a in range(1, 8):
            cnt_lo = jnp.where(rid8 == a, jnp.sum(chosen[8 + 8 * a:16 + 8 * a], axis=0, keepdims=True), cnt_lo)
        cnt = jnp.concatenate([cnt_lo, chosen[N_CAND - 8:N_CAND]], axis=0)
        cnt_rows = [jnp.broadcast_to(cnt[a:a + 1], (SUBLANES, LANE)) for a in range(PEER_TOPK)]
        inv_z = 1.0 / zsum
        for jj in range(ntile // 2):
            ranks, e2s = [], []
            for j in (2 * jj, 2 * jj + 1):
                rows = slice(j * SUBLANES, (j + 1) * SUBLANES)
                d1 = s1[rows]
                d2 = s2[rows]
                brow = jnp.zeros((SUBLANES, LANE), F32)
                rank2 = jnp.zeros((SUBLANES, LANE), F32)
                for a in range(PEER_TOPK):
                    brow = jnp.where(d1 == t1[a], cnt_rows[a], brow)
                    rank2 = jnp.where(t2[a] > d2, float(a + 1), rank2)
                brow_ref[rows, ls] = brow
                crow_ref[rows, ls] = jnp.exp(d1 - t1[0]) * inv_z
                ranks.append(rank2)
                e2s.append(jnp.exp(d2 - t2[0]))
            rows16 = slice(jj * 2 * SUBLANES, (jj + 1) * 2 * SUBLANES)
            r2_ref[rows16, ls] = jnp.concatenate(ranks, axis=0).astype(BF16)
            e2_ref[rows16, ls] = jnp.concatenate(e2s, axis=0).astype(BF16)


def _peer_select(qry, sub_keys):
    nt = N_TOK // T_SEL
    kspec = pl.BlockSpec((None, PEER_NKEYS, T_SEL), lambda i, h: (h, 0, i))

    def kshape(dt):
        return jax.ShapeDtypeStruct((PEER_HEADS, PEER_NKEYS, N_TOK), dt)

    return pl.pallas_call(
        _peer_select_kernel,
        grid=(nt, PEER_HEADS),
        in_specs=[pl.BlockSpec((T_SEL, 2 * PEER_NKEYS), lambda i, h: (i, h)),
                  pl.BlockSpec((None, 2, PEER_NKEYS, PEER_NKEYS), lambda i, h: (h, 0, 0, 0))],
        out_specs=[kspec, kspec, kspec, kspec],
        out_shape=[kshape(BF16), kshape(BF16), kshape(F32), kshape(F32)],
        scratch_shapes=[pltpu.VMEM((N_CAND, LANE), F32)],
        compiler_params=_params("parallel", "arbitrary"),
        name="peer_select",
    )(qry, sub_keys)


E_PAIR = 2 * PEER_NKEYS
BF16_ROWS = 16


def _row_bf16(row):
    return jnp.broadcast_to(row, (BF16_ROWS, LANE)).astype(BF16)


def _gelu(x):
    return 0.5 * x * (1.0 + lax.erf(x * (2.0 ** -0.5)))


def _peer_dense_kernel(h_ref, u_ref, v_ref, brow_ref, crow_ref, r2_ref, e2_ref, x_ref, mod_ref,
                       o_ref, acc_ref, a_scr, p_scr):
    c = pl.program_id(1)

    @pl.when(c == 0)
    def _():
        acc_ref[...] = jnp.zeros_like(acc_ref)

    hb = h_ref[...]
    zero = jnp.zeros((BF16_ROWS, LANE), BF16)
    npair = E_CHUNK // E_PAIR
    for j in range(npair + 1):
        slot = j % 2
        if j < npair:
            a_scr[slot] = _dot_nt(u_ref[j * E_PAIR:(j + 1) * E_PAIR, :].astype(BF16), hb)
        if j > 0:
            acc_ref[...] += _dot_tn(p_scr[1 - slot], v_ref[(j - 1) * E_PAIR:j * E_PAIR, :].astype(BF16))
        if j == npair:
            break
        for half in range(2):
            n1l = 2 * j + half
            for tc in range(T_PEER // LANE):
                ls = slice(tc * LANE, (tc + 1) * LANE)
                b16 = [_row_bf16(brow_ref[h, n1l:n1l + 1, ls]) for h in range(PEER_HEADS)]
                c16 = [_row_bf16(crow_ref[h, n1l:n1l + 1, ls]) for h in range(PEER_HEADS)]
                e0 = half * PEER_NKEYS
                act = _gelu(a_scr[slot, e0:e0 + PEER_NKEYS, ls]).astype(BF16)
                for rg in range(PEER_NKEYS // BF16_ROWS):
                    rs = slice(rg * BF16_ROWS, (rg + 1) * BF16_ROWS)
                    g = None
                    for h in range(PEER_HEADS):
                        t = jnp.where(r2_ref[h, rs, ls] < b16[h], e2_ref[h, rs, ls], zero) * c16[h]
                        g = t if g is None else g + t
                    p_scr[slot, e0 + rg * BF16_ROWS:e0 + (rg + 1) * BF16_ROWS, ls] = g * act[rs]

    @pl.when(c == pl.num_programs(1) - 1)
    def _():
        o_ref[...] = x_ref[...] + mod_ref[5:6, :] * acc_ref[...]


def _peer_dense(h2, peer_u, peer_v, layer, r2, e2, brow, crow, x, mod_l):
    nt = N_TOK // T_PEER
    nc = PEER_N // E_CHUNK
    n1c = E_CHUNK // PEER_NKEYS
    rowspec = pl.BlockSpec((PEER_HEADS, n1c, T_PEER), lambda i, c: (0, c, i))
    fullspec = pl.BlockSpec((PEER_HEADS, PEER_NKEYS, T_PEER), lambda i, c: (0, 0, i))
    return pl.pallas_call(
        _peer_dense_kernel,
        grid=(nt, nc),
        in_specs=[pl.BlockSpec((T_PEER, D_MODEL), lambda i, c: (i, 0)),
                  pl.BlockSpec((None, E_CHUNK, D_MODEL), lambda i, c: (layer, c, 0)),
                  pl.BlockSpec((None, E_CHUNK, D_MODEL), lambda i, c: (layer, c, 0)),
                  rowspec, rowspec, fullspec, fullspec,
                  pl.BlockSpec((T_PEER, D_MODEL), lambda i, c: (i, 0)),
                  pl.BlockSpec((None, 6, D_MODEL), lambda i, c: (_mod_row(i, T_PEER), 0, 0))],
        out_specs=pl.BlockSpec((T_PEER, D_MODEL), lambda i, c: (i, 0)),
        out_shape=jax.ShapeDtypeStruct((N_TOK, D_MODEL), F32),
        scratch_shapes=[pltpu.VMEM((T_PEER, D_MODEL), F32),
                        pltpu.VMEM((2, E_PAIR, T_PEER), F32),
                        pltpu.VMEM((2, E_PAIR, T_PEER), BF16)],
        compiler_params=_params("parallel", "arbitrary"),
        name="peer_dense",
    )(h2, peer_u, peer_v, brow, crow, r2, e2, x, mod_l)


def kernel(x_prompt, x_sample, cache_nat_k, cache_nat_v, cache_diff_k, cache_diff_v, c, c_ctx, w_ada, b_ada, norm1_g, norm2_g, w_in, pool_w, pool_scale, nat_q_g, nat_k_g, nat_rel_bias, conv_w, conv_b, conv_ln_g, conv_ln_b, diff_q_g, diff_k_g, diff_lambda_p, diff_subln_g, w_branch, w_gate, b_gate, w_out, peer_w_query, peer_sub_keys, peer_u, peer_v):
    x = jnp.concatenate([x_prompt.reshape(N_CTX, D_MODEL), x_sample.reshape(N_LAT, D_MODEL)], axis=0)
    cvec = jnp.concatenate([c_ctx[None, :], c, jnp.zeros((8 - 1 - DEC_BATCH, D_MODEL), F32)], axis=0)
    mod = _modulation(cvec, w_ada, b_ada).reshape(DEPTH, 8, 6, D_MODEL)

    gid = jnp.arange(CB) // NAT_HD
    gmat = (gid[:, None] == gid[None, :]).astype(BF16)
    rope = _rope_tables(T_PRE)
    ck_n = cache_nat_k.reshape(DEC_BATCH, DEPTH, PAST_LEN, BRANCH_W)
    cv_n = cache_nat_v.reshape(DEC_BATCH, DEPTH, PAST_LEN, BRANCH_W)
    ck_d = cache_diff_k.reshape(DEC_BATCH, DEPTH, PAST_LEN, BRANCH_W)
    cv_d = cache_diff_v.reshape(DEC_BATCH, DEPTH, PAST_LEN, BRANCH_W)

    states = []
    for l in range(DEPTH):
        gg = jnp.stack([jnp.tile(nat_q_g[l], NAT_HEADS), jnp.tile(nat_k_g[l], NAT_HEADS),
                        jnp.tile(diff_q_g[l], 2 * DIFF_HEADS), jnp.tile(diff_k_g[l], 2 * DIFF_HEADS)])
        h1, zp, zn = _pre(x, mod[l], norm1_g[l].reshape(1, -1), w_in[l].astype(BF16),
                          gg.reshape(len(NORM_ZB), 1, CB), gmat, rope)

        y_pool, y_conv = _seq_mixers(zp, pool_w[l], pool_scale[l], conv_w[l], conv_b[l],
                                     conv_ln_g[l], conv_ln_b[l])
        y_nat_ctx = _nat_ctx(zp, zn)
        y_nat_lat = _nat_lat(zp, zn, ck_n, cv_n, *_nat_bias_table(nat_rel_bias[l]), l)
        y_diff_ctx = _diff_ctx(zp, zn, diff_lambda_p[l], diff_subln_g[l], l)
        y_diff_lat = _diff_lat(zp, zn, ck_d, cv_d, diff_lambda_p[l], diff_subln_g[l], l)

        x, h2, qry = _merge(h1, w_gate[l].astype(BF16), b_gate[l].reshape(1, -1), y_pool, y_nat_ctx, y_nat_lat,
                            y_conv, y_diff_ctx, y_diff_lat, x, mod[l], norm2_g[l].reshape(1, -1),
                            w_branch[l].astype(BF16), w_out[l].astype(BF16), peer_w_query[l].astype(BF16))
        r2, e2, brow, crow = _peer_select(qry, peer_sub_keys[l])
        x = _peer_dense(h2, peer_u, peer_v, l, r2, e2, brow, crow, x, mod[l])

        states.append([zn[:N_CTX, Q_NK * CB:(Q_NK + 1) * CB], zp[:N_CTX, P_NV * CB:(P_NV + 1) * CB],
                       zn[:N_CTX, Q_DK * CB:(Q_DK + 1) * CB], zp[:N_CTX, P_DV * CB:(P_DV + 1) * CB]])

    def stack(idx, shape):
        return jnp.stack([states[l][idx].reshape((BATCH, SEQ) + shape) for l in range(DEPTH)], axis=1)

    return (x[:N_CTX].reshape(BATCH, SEQ, D_MODEL),
            x[N_CTX:].reshape(DEC_BATCH, DEC_SEQ, D_MODEL),
            stack(0, (NAT_HEADS, NAT_HD)),
            stack(1, (NAT_HEADS, NAT_HD)),
            stack(2, (DIFF_HEADS, 2, DIFF_HD)),
            stack(3, (DIFF_HEADS, 2 * DIFF_HD)))
```

```python
import functools
import math

import jax
import jax.numpy as jnp
from jax import lax
from jax.experimental import pallas as pl
from jax.experimental.pallas import tpu as pltpu

F32 = jnp.float32
BF16 = jnp.bfloat16

D_MODEL = 1024
BATCH = 16
SEQ = 256
DEPTH = 2
DEC_BATCH = 4
DEC_SEQ = 2048
PAST_LEN = 256
GRID_W = 64
BRANCH_W = 512
POOL_WINDOWS = (2, 4, 8, 16)
POOL_GROUP = 128
NAT_HEADS = 8
NAT_HD = 64
NAT_WIN_R = 8
NAT_WIN_C = 16
CONV_W = 31
DIFF_HEADS = 4
DIFF_HD = 64
PEER_HEADS = 8
PEER_NKEYS = 128
PEER_N = PEER_NKEYS * PEER_NKEYS
PEER_TOPK = 16
ROPE_BASE = 10000.0
EPS = 1e-6
NEG_INF = -1e30
LOG2E = math.log2(math.e)

N_CTX = BATCH * SEQ
N_LAT = DEC_BATCH * DEC_SEQ
N_TOK = N_CTX + N_LAT
GRID_ROWS = DEC_SEQ // GRID_W

CB = 512
ZB_POOL, ZB_NQ, ZB_NK, ZB_NV, ZB_CA, ZB_CGATE, ZB_DQ, ZB_DK, ZB_DV = range(9)

V7X_VMEM_LIMIT = 52 * 1024 * 1024

T_PRE = 2048
T_SEQ = 256
HALO = 16
T_MERGE = 512
T_SEL = 1024
T_PEER = 512
E_CHUNK = 1024
LANE = 128


def _sigmoid(x):
    return 1.0 / (1.0 + jnp.exp(-x))


def _dot(a, b):
    return jnp.dot(a, b, preferred_element_type=F32)


def _dot_nt(a, b):
    return lax.dot_general(a, b, (((1,), (1,)), ((), ())), preferred_element_type=F32)


def _dot_tn(a, b):
    return lax.dot_general(a, b, (((0,), (0,)), ((), ())), preferred_element_type=F32)


def _split_bf16(a):
    hi = a.astype(BF16)
    lo = (a - hi.astype(F32)).astype(BF16)
    return hi, lo


def _params(*sem):
    return pltpu.CompilerParams(dimension_semantics=sem, vmem_limit_bytes=V7X_VMEM_LIMIT)


def _mod_row(i, tile):
    nct = N_CTX // tile
    per = DEC_SEQ // tile
    return jnp.where(i < nct, 0, 1 + (i - nct) // per)


def _mod_kernel(c_ref, w_ref, b_ref, o_ref):
    c = c_ref[...]
    a = c * _sigmoid(c)
    w = w_ref[0]
    a_hi, a_lo = _split_bf16(a)
    w_hi, w_lo = _split_bf16(w)
    o_ref[0] = _dot(a_hi, w_hi) + _dot(a_lo, w_hi) + _dot(a_hi, w_lo) + b_ref[0]


def _modulation(cvec, w_ada, b_ada):
    tn = 1024
    return pl.pallas_call(
        _mod_kernel,
        grid=(DEPTH, 6 * D_MODEL // tn),
        in_specs=[
            pl.BlockSpec((8, D_MODEL), lambda l, j: (0, 0)),
            pl.BlockSpec((1, D_MODEL, tn), lambda l, j: (l, 0, j)),
            pl.BlockSpec((1, 1, tn), lambda l, j: (l, 0, j)),
        ],
        out_specs=pl.BlockSpec((1, 8, tn), lambda l, j: (l, 0, j)),
        out_shape=jax.ShapeDtypeStruct((DEPTH, 8, 6 * D_MODEL), F32),
        compiler_params=_params("parallel", "parallel"),
        name="modulation",
    )(cvec, w_ada, b_ada.reshape(DEPTH, 1, 6 * D_MODEL))


def _tile4(t):
    return jnp.concatenate([t, t, t, t], axis=1)


def _hnorm_kernel(x_ref, mod_ref, g_ref, h_ref):
    x = x_ref[...]
    ms = jnp.mean(x * x, axis=-1, keepdims=True)
    y = x * lax.rsqrt(ms + EPS) * g_ref[...]
    h_ref[...] = (y * (1.0 + mod_ref[1:2, :]) + mod_ref[0:1, :]).astype(h_ref.dtype)


def _proj_plain_kernel(h_ref, w_ref, o_ref):
    o_ref[...] = _dot(h_ref[...], w_ref[...])


def _proj_norm_kernel(h_ref, w_ref, gg_ref, gmat_ref, rc_ref, rp_ref, rm_ref, o_ref):
    acc = _dot(h_ref[...], w_ref[...])
    hi, lo = _split_bf16(acc * acc)
    ss = _dot(hi, gmat_ref[...]) + _dot(lo, gmat_ref[...])
    y = acc * lax.rsqrt(ss * (1.0 / NAT_HD) + EPS) * gg_ref[0]
    o_ref[...] = (y * _tile4(rc_ref[...])
                  + pltpu.roll(y, 16, 1) * _tile4(rp_ref[...])
                  + pltpu.roll(y, CB - 16, 1) * _tile4(rm_ref[...]))


def _rope_tables(tile):
    quarter = DIFF_HD // 4
    t = jnp.arange(DEC_SEQ)
    lane = jnp.arange(LANE)
    d = lane % DIFF_HD
    freqs = ROPE_BASE ** (-jnp.arange(quarter, dtype=F32) / quarter)
    pos = jnp.where(d[None, :] < DIFF_HD // 2, (t // GRID_W)[:, None], (t % GRID_W)[:, None]).astype(F32)
    ang = pos * freqs[d % quarter][None, :]
    cos = jnp.cos(ang)
    sin = jnp.sin(ang)
    second = (d % (2 * quarter)) >= quarter
    s_plus = jnp.where(second[None, :], sin, 0.0)
    s_minus = jnp.where(second[None, :], 0.0, -sin)
    ones = jnp.ones((tile, LANE), F32)
    zeros = jnp.zeros((tile, LANE), F32)
    return (jnp.concatenate([cos, ones], 0), jnp.concatenate([s_plus, zeros], 0),
            jnp.concatenate([s_minus, zeros], 0))


PLAIN_ZB = (ZB_POOL, ZB_NV, ZB_CA, ZB_CGATE, ZB_DV)
NORM_ZB = (ZB_NQ, ZB_NK, ZB_DQ, ZB_DK)
P_POOL, P_NV, P_CA, P_CGATE, P_DV = range(5)
Q_NQ, Q_NK, Q_DQ, Q_DK = range(4)


def _pick(j, values):
    out = values[0]
    for n, v in enumerate(values[1:], start=1):
        out = jnp.where(j == n, v, out)
    return out


def _pre(x, mod_l, norm_g, w_in, gg, gmat, rope):
    nct = N_CTX // T_PRE
    per = DEC_SEQ // T_PRE
    ident_blk = DEC_SEQ // T_PRE
    nt = N_TOK // T_PRE
    h = pl.pallas_call(
        _hnorm_kernel,
        grid=(nt,),
        in_specs=[pl.BlockSpec((T_PRE, D_MODEL), lambda i: (i, 0)),
                  pl.BlockSpec((None, 6, D_MODEL), lambda i: (_mod_row(i, T_PRE), 0, 0)),
                  pl.BlockSpec((1, D_MODEL), lambda i: (0, 0))],
        out_specs=pl.BlockSpec((T_PRE, D_MODEL), lambda i: (i, 0)),
        out_shape=jax.ShapeDtypeStruct((N_TOK, D_MODEL), BF16),
        compiler_params=_params("parallel"),
        name="hnorm",
    )(x, mod_l, norm_g)

    hspec = pl.BlockSpec((T_PRE, D_MODEL), lambda i, j: (i, 0))
    ospec = pl.BlockSpec((T_PRE, CB), lambda i, j: (i, j))

    zp = pl.pallas_call(
        _proj_plain_kernel,
        grid=(nt, len(PLAIN_ZB)),
        in_specs=[hspec, pl.BlockSpec((D_MODEL, CB), lambda i, j: (0, _pick(j, PLAIN_ZB)))],
        out_specs=ospec,
        out_shape=jax.ShapeDtypeStruct((N_TOK, len(PLAIN_ZB) * CB), F32),
        compiler_params=_params("parallel", "arbitrary"),
        name="proj_plain",
    )(h, w_in)

    def rope_idx(i, j):
        use = ((j == Q_DQ) | (j == Q_DK)) & (i >= nct)
        return jnp.where(use, (i - nct) % per, ident_blk), 0

    rope_spec = pl.BlockSpec((T_PRE, LANE), rope_idx)
    zn = pl.pallas_call(
        _proj_norm_kernel,
        grid=(nt, len(NORM_ZB)),
        in_specs=[hspec, pl.BlockSpec((D_MODEL, CB), lambda i, j: (0, _pick(j, NORM_ZB))),
                  pl.BlockSpec((1, 1, CB), lambda i, j: (j, 0, 0)),
                  pl.BlockSpec((CB, CB), lambda i, j: (0, 0)),
                  rope_spec, rope_spec, rope_spec],
        out_specs=ospec,
        out_shape=jax.ShapeDtypeStruct((N_TOK, len(NORM_ZB) * CB), F32),
        compiler_params=_params("parallel", "arbitrary"),
        name="proj_norm",
    )(h, w_in, gg, gmat, *rope)

    return h, zp, zn


def _seq_kernel(zp_ref, zp_prev, zp_next, za_ref, za_prev, za_next, zb_ref, zb_prev, zb_next,
                pw_ref, ps_ref, cw_ref, cb_ref, lg_ref, lb_ref,
                yp_ref, yc_ref, pbuf, ubuf, ybuf, shbuf):
    i = pl.program_id(0)
    nct = N_CTX // T_SEQ
    per = DEC_SEQ // T_SEQ
    is_lat = i >= nct
    kk = (i - nct) % per
    has_prev = is_lat & (kk != 0)
    has_next = is_lat & (kk != per - 1)
    pos0 = jnp.where(is_lat, kk * T_SEQ, 0)
    seq_len = jnp.where(is_lat, DEC_SEQ, SEQ)
    fp = jnp.where(has_prev, 1.0, 0.0).astype(F32)
    fn = jnp.where(has_next, 1.0, 0.0).astype(F32)

    pbuf[0:HALO, :] = zp_prev[...] * fp
    pbuf[HALO:HALO + T_SEQ, :] = zp_ref[...]
    pbuf[HALO + T_SEQ:, :] = zp_next[...] * fn
    ubuf[0:HALO, :] = za_prev[...] * _sigmoid(zb_prev[...]) * fp
    ubuf[HALO:HALO + T_SEQ, :] = za_ref[...] * _sigmoid(zb_ref[...])
    ubuf[HALO + T_SEQ:, :] = za_next[...] * _sigmoid(zb_next[...]) * fn

    pos = pos0 + lax.broadcasted_iota(jnp.int32, (T_SEQ, LANE), 0)
    for gi, w in enumerate(POOL_WINDOWS):
        ls = slice(gi * LANE, (gi + 1) * LANE)
        acc = None
        for s in range(-(w // 2), w - w // 2):
            v = pbuf[HALO + s:HALO + s + T_SEQ, ls]
            acc = v if acc is None else acc + v
        lo = jnp.maximum(pos - w // 2, 0)
        hi = jnp.minimum(pos + (w - w // 2), seq_len)
        mean = acc / (hi - lo).astype(F32)
        d = mean - pbuf[HALO:HALO + T_SEQ, ls]
        y = _dot(d.astype(BF16), pw_ref[gi].astype(BF16)) * ps_ref[:, ls]
        yp_ref[:, ls] = y.astype(yp_ref.dtype)

    base = HALO - CONV_W // 2
    span = shbuf.shape[0]
    for b in range(8):
        shbuf[...] = ubuf[base + b:base + b + span, :]
        for c in range(BRANCH_W // LANE):
            ls = slice(c * LANE, (c + 1) * LANE)
            acc = cb_ref[:, ls] if b == 0 else ybuf[:, ls]
            for k in range(b, CONV_W, 8):
                acc = acc + shbuf[k - b:k - b + T_SEQ, ls] * cw_ref[k:k + 1, ls]
            ybuf[:, ls] = acc
    y = ybuf[...]
    mu = jnp.mean(y, axis=-1, keepdims=True)
    yc = y - mu
    var = jnp.mean(yc * yc, axis=-1, keepdims=True)
    yn = yc * lax.rsqrt(var + EPS) * lg_ref[...] + lb_ref[...]
    yc_ref[...] = (yn * _sigmoid(yn)).astype(yc_ref.dtype)


def _seq_mixers(zp, pool_w, pool_scale, conv_w, conv_b, ln_g, ln_b):
    nt = N_TOK // T_SEQ
    hb = T_SEQ // HALO
    n_hblk = N_TOK // HALO

    def specs(cb):
        return [
            pl.BlockSpec((T_SEQ, CB), lambda i: (i, cb)),
            pl.BlockSpec((HALO, CB), lambda i: (jnp.maximum(i * hb - 1, 0), cb)),
            pl.BlockSpec((HALO, CB), lambda i: (jnp.minimum((i + 1) * hb, n_hblk - 1), cb)),
        ]

    def full(shape):
        return pl.BlockSpec(shape, lambda i: (0,) * len(shape))

    cw = jnp.concatenate([conv_w, jnp.zeros((1, BRANCH_W), F32)], axis=0)
    out_spec = pl.BlockSpec((T_SEQ, BRANCH_W), lambda i: (i, 0))
    return pl.pallas_call(
        _seq_kernel,
        grid=(nt,),
        in_specs=(specs(P_POOL) + specs(P_CA) + specs(P_CGATE)
                  + [full((4, POOL_GROUP, POOL_GROUP)), full((1, BRANCH_W)), full((CONV_W + 1, BRANCH_W)),
                     full((1, BRANCH_W)), full((1, BRANCH_W)), full((1, BRANCH_W))]),
        out_specs=[out_spec, out_spec],
        out_shape=[jax.ShapeDtypeStruct((N_TOK, BRANCH_W), BF16)] * 2,
        scratch_shapes=[pltpu.VMEM((T_SEQ + 2 * HALO, BRANCH_W), F32),
                        pltpu.VMEM((T_SEQ + 2 * HALO, BRANCH_W), F32),
                        pltpu.VMEM((T_SEQ, BRANCH_W), F32),
                        pltpu.VMEM((T_SEQ + 8 * ((CONV_W - 1) // 8), BRANCH_W), F32)],
        compiler_params=_params("parallel"),
        name="pool_conv",
    )(zp, zp, zp, zp, zp, zp, zp, zp, zp, pool_w, pool_scale.reshape(1, -1), cw,
      conv_b.reshape(1, -1), ln_g.reshape(1, -1), ln_b.reshape(1, -1))


def _nat_ctx_kernel(q_ref, k_ref, v_ref, o_ref):
    for h in range(NAT_HEADS):
        sl = slice(h * NAT_HD, (h + 1) * NAT_HD)
        q = (q_ref[:, sl] * (NAT_HD ** -0.5 * LOG2E)).astype(BF16)
        s = _dot_nt(q, k_ref[:, sl].astype(BF16))
        m = jnp.max(s, axis=-1, keepdims=True)
        e = jnp.exp2(s - m)
        den = jnp.sum(e, axis=-1, keepdims=True)
        o = _dot(e.astype(BF16), v_ref[:, sl].astype(BF16)) / den
        o_ref[:, sl] = o.astype(o_ref.dtype)


def _nat_ctx(zp, zn):
    def spec(cb):
        return pl.BlockSpec((SEQ, CB), lambda b: (b, cb))

    return pl.pallas_call(
        _nat_ctx_kernel,
        grid=(BATCH,),
        in_specs=[spec(Q_NQ), spec(Q_NK), spec(P_NV)],
        out_specs=pl.BlockSpec((SEQ, BRANCH_W), lambda b: (b, 0)),
        out_shape=jax.ShapeDtypeStruct((N_CTX, BRANCH_W), BF16),
        compiler_params=_params("parallel"),
        name="nat_ctx",
    )(zn, zn, zp)


NAT_QROWS = 4
NAT_KROWS = 12


def _nat_row0(r):
    return jnp.clip(r - NAT_WIN_R // 2, 0, GRID_ROWS - NAT_WIN_R)


def _nat_lat_kernel(q_ref, k_ref, v_ref, ck_ref, cv_ref, bias_ref, o_ref):
    g = pl.program_id(1)
    w0 = jnp.minimum(_nat_row0(g * NAT_QROWS), GRID_ROWS - NAT_KROWS)
    start = pl.multiple_of(w0 * GRID_W, GRID_W)
    nwin = NAT_KROWS * GRID_W
    for h in range(NAT_HEADS):
        sl = slice(h * NAT_HD, (h + 1) * NAT_HD)
        q = (q_ref[:, sl] * (NAT_HD ** -0.5 * LOG2E)).astype(BF16)
        kw = k_ref[pl.ds(start, nwin), sl].astype(BF16)
        vw = v_ref[pl.ds(start, nwin), sl].astype(BF16)
        s_loc = _dot_nt(q, kw) + bias_ref[h]
        s_ctx = _dot_nt(q, ck_ref[:, sl].astype(BF16))
        m = jnp.maximum(jnp.max(s_loc, axis=-1, keepdims=True), jnp.max(s_ctx, axis=-1, keepdims=True))
        e_loc = jnp.exp2(s_loc - m)
        e_ctx = jnp.exp2(s_ctx - m)
        den = jnp.sum(e_loc, axis=-1, keepdims=True) + jnp.sum(e_ctx, axis=-1, keepdims=True)
        o = (_dot(e_loc.astype(BF16), vw) + _dot(e_ctx.astype(BF16), cv_ref[:, sl].astype(BF16))) / den
        o_ref[:, sl] = o.astype(o_ref.dtype)


def _nat_bias_kernel(rb_ref, oh_ref, o_ref):
    x = rb_ref[...]
    hi = x.astype(BF16)
    r1 = x - hi.astype(F32)
    mid = r1.astype(BF16)
    lo = (r1 - mid.astype(F32)).astype(BF16)
    oh = oh_ref[...]
    o_ref[...] = _dot(hi, oh) + _dot(mid, oh) + _dot(lo, oh)


def _nat_bias_table(rel_bias):
    ndr = 2 * NAT_WIN_R - 1
    ndc = 2 * NAT_WIN_C - 1
    q = jnp.arange(GRID_W)
    kc = jnp.arange(GRID_W)
    dcol = jnp.clip(kc[None, :] - q[:, None] + NAT_WIN_C - 1, 0, ndc - 1)
    wstart = jnp.clip(q - NAT_WIN_C // 2, 0, GRID_W - NAT_WIN_C)
    valid = (kc[None, :] >= wstart[:, None]) & (kc[None, :] < wstart[:, None] + NAT_WIN_C)
    d = jnp.arange(LANE)
    onehot = jnp.where(d[:, None, None] == ndc, jnp.logical_not(valid)[None],
                       (d[:, None, None] == dcol[None]) & valid[None])
    onehot = onehot.reshape(LANE, GRID_W * GRID_W).astype(BF16)
    nrow = NAT_HEADS * ndr
    rb = jnp.concatenate([rel_bias.reshape(nrow, ndc).astype(F32) * LOG2E, jnp.full((nrow, 1), NEG_INF, F32),
                          jnp.zeros((nrow, LANE - ndc - 1), F32)], axis=1)
    rb = jnp.concatenate([rb, jnp.zeros((LANE - nrow, LANE), F32)], axis=0)
    tcol = pl.pallas_call(
        _nat_bias_kernel,
        out_shape=jax.ShapeDtypeStruct((LANE, GRID_W * GRID_W), F32),
        compiler_params=pltpu.CompilerParams(vmem_limit_bytes=V7X_VMEM_LIMIT),
        name="nat_bias",
    )(rb, onehot)
    tcol = tcol[:nrow].reshape(NAT_HEADS, ndr, GRID_W, GRID_W)
    neg = jnp.full((NAT_HEADS, GRID_W, GRID_W), NEG_INF, F32)
    tables, variant_of_group, seen = [], [], {}
    for g in range(GRID_ROWS // NAT_QROWS):
        r_first = g * NAT_QROWS
        w0 = min(max(r_first - NAT_WIN_R // 2, 0), GRID_ROWS - NAT_WIN_R, GRID_ROWS - NAT_KROWS)
        rows = [(r_first + j - w0, min(max(r_first + j - NAT_WIN_R // 2, 0), GRID_ROWS - NAT_WIN_R) - w0)
                for j in range(NAT_QROWS)]
        key = tuple(rows)
        if key not in seen:
            seen[key] = len(tables)
            blocks = []
            for rq, rw in rows:
                blocks.append(jnp.concatenate(
                    [tcol[:, i - rq + NAT_WIN_R - 1] if rw <= i < rw + NAT_WIN_R else neg
                     for i in range(NAT_KROWS)], axis=-1))
            tables.append(jnp.concatenate(blocks, axis=1))
        variant_of_group.append(seen[key])
    return jnp.stack(tables, axis=0), tuple(variant_of_group)


def _nat_lat(zp, zn, cache_k, cache_v, bias_tab, variant_of_group, layer):
    tq = NAT_QROWS * GRID_W
    ngroups = GRID_ROWS // NAT_QROWS
    q_blk0 = N_CTX // tq
    kv_blk0 = N_CTX // DEC_SEQ
    cache_spec = pl.BlockSpec((None, None, PAST_LEN, BRANCH_W), lambda b, g: (b, layer, 0, 0))
    return pl.pallas_call(
        _nat_lat_kernel,
        grid=(DEC_BATCH, ngroups),
        in_specs=[
            pl.BlockSpec((tq, CB), lambda b, g: (q_blk0 + b * ngroups + g, Q_NQ)),
            pl.BlockSpec((DEC_SEQ, CB), lambda b, g: (kv_blk0 + b, Q_NK)),
            pl.BlockSpec((DEC_SEQ, CB), lambda b, g: (kv_blk0 + b, P_NV)),
            cache_spec, cache_spec,
            pl.BlockSpec((None, NAT_HEADS, tq, NAT_KROWS * GRID_W),
                         lambda b, g: (_pick(g, variant_of_group), 0, 0, 0)),
        ],
        out_specs=pl.BlockSpec((tq, BRANCH_W), lambda b, g: (b * ngroups + g, 0)),
        out_shape=jax.ShapeDtypeStruct((N_LAT, BRANCH_W), BF16),
        compiler_params=_params("parallel", "arbitrary"),
        name="nat_lat",
    )(zn, zn, zp, cache_k, cache_v, bias_tab)


def _diff_kernel(has_cache, lam_init, *refs):
    if has_cache:
        q_ref, k_ref, v_ref, ck_ref, cv_ref, lamp_ref, g_ref, o_ref = refs
    else:
        q_ref, k_ref, v_ref, lamp_ref, g_ref, o_ref = refs
    lp = lamp_ref[...]
    lam = (jnp.exp(jnp.sum(lp[0:1] * lp[1:2], axis=-1, keepdims=True))
           - jnp.exp(jnp.sum(lp[2:3] * lp[3:4], axis=-1, keepdims=True)) + lam_init)
    hv = 2 * DIFF_HD
    for h in range(DIFF_HEADS):
        vs = slice(h * hv, (h + 1) * hv)
        vb = v_ref[:, vs].astype(BF16)
        if has_cache:
            cvb = cv_ref[:, vs].astype(BF16)
        parts = []
        for i in range(2):
            sl = slice(h * hv + i * DIFF_HD, h * hv + (i + 1) * DIFF_HD)
            q = (q_ref[:, sl] * (DIFF_HD ** -0.5 * LOG2E)).astype(BF16)
            s = _dot_nt(q, k_ref[:, sl].astype(BF16))
            m = jnp.max(s, axis=-1, keepdims=True)
            if has_cache:
                sc = _dot_nt(q, ck_ref[:, sl].astype(BF16))
                m = jnp.maximum(m, jnp.max(sc, axis=-1, keepdims=True))
            e = jnp.exp2(s - m)
            den = jnp.sum(e, axis=-1, keepdims=True)
            o = _dot(e.astype(BF16), vb)
            if has_cache:
                ec = jnp.exp2(sc - m)
                den = den + jnp.sum(ec, axis=-1, keepdims=True)
                o = o + _dot(ec.astype(BF16), cvb)
            parts.append((o, 1.0 / den))
        (o1, r1), (o2, r2) = parts
        o = o1 * r1 - o2 * (lam * r2)
        ms = jnp.mean(o * o, axis=-1, keepdims=True)
        y = o * lax.rsqrt(ms + EPS) * g_ref[...] * (1.0 - lam_init)
        o_ref[:, vs] = y.astype(o_ref.dtype)


def _lam_init(layer):
    return 0.8 - 0.6 * math.exp(-0.3 * layer)


def _diff_ctx(zp, zn, lam_p, subln_g, layer):
    def spec(cb):
        return pl.BlockSpec((SEQ, CB), lambda b: (b, cb))

    return pl.pallas_call(
        functools.partial(_diff_kernel, False, _lam_init(layer)),
        grid=(BATCH,),
        in_specs=[spec(Q_DQ), spec(Q_DK), spec(P_DV),
                  pl.BlockSpec((4, DIFF_HD), lambda b: (0, 0)),
                  pl.BlockSpec((1, 2 * DIFF_HD), lambda b: (0, 0))],
        out_specs=pl.BlockSpec((SEQ, BRANCH_W), lambda b: (b, 0)),
        out_shape=jax.ShapeDtypeStruct((N_CTX, BRANCH_W), BF16),
        compiler_params=_params("parallel"),
        name="diff_ctx",
    )(zn, zn, zp, lam_p, subln_g.reshape(1, -1))


T_DQ = 512


def _diff_lat(zp, zn, cache_k, cache_v, lam_p, subln_g, layer):
    nq = DEC_SEQ // T_DQ
    q_blk0 = N_CTX // T_DQ
    kv_blk0 = N_CTX // DEC_SEQ
    cache_spec = pl.BlockSpec((None, None, PAST_LEN, BRANCH_W), lambda b, t: (b, layer, 0, 0))
    return pl.pallas_call(
        functools.partial(_diff_kernel, True, _lam_init(layer)),
        grid=(DEC_BATCH, nq),
        in_specs=[
            pl.BlockSpec((T_DQ, CB), lambda b, t: (q_blk0 + b * nq + t, Q_DQ)),
            pl.BlockSpec((DEC_SEQ, CB), lambda b, t: (kv_blk0 + b, Q_DK)),
            pl.BlockSpec((DEC_SEQ, CB), lambda b, t: (kv_blk0 + b, P_DV)),
            cache_spec, cache_spec,
            pl.BlockSpec((4, DIFF_HD), lambda b, t: (0, 0)),
            pl.BlockSpec((1, 2 * DIFF_HD), lambda b, t: (0, 0)),
        ],
        out_specs=pl.BlockSpec((T_DQ, BRANCH_W), lambda b, t: (b * nq + t, 0)),
        out_shape=jax.ShapeDtypeStruct((N_LAT, BRANCH_W), BF16),
        compiler_params=_params("parallel", "arbitrary"),
        name="diff_lat",
    )(zn, zn, zp, cache_k, cache_v, lam_p, subln_g.reshape(1, -1))


def _merge_kernel(h_ref, wg_ref, bg_ref, yp_ref, ync_ref, ynl_ref, yc_ref, ydc_ref, ydl_ref, x_ref, mod_ref,
                  g2_ref, wb_ref, wo_ref, xo_ref, h2_ref, merged_scr):
    br = pl.program_id(1)
    is_ctx = pl.program_id(0) < N_CTX // T_MERGE
    y_nat = jnp.where(is_ctx, ync_ref[...], ynl_ref[...])
    y_diff = jnp.where(is_ctx, ydc_ref[...], ydl_ref[...])
    y = jnp.where(br == 0, yp_ref[...], jnp.where(br == 1, y_nat, jnp.where(br == 2, yc_ref[...], y_diff)))
    gate = _sigmoid(_dot(h_ref[...], wg_ref[...]) + bg_ref[...])
    t = gate * _dot(y, wb_ref[...])

    @pl.when(br == 0)
    def _():
        merged_scr[...] = t

    @pl.when(br > 0)
    def _():
        merged_scr[...] += t

    @pl.when(br == pl.num_programs(1) - 1)
    def _():
        out = _dot(merged_scr[...].astype(BF16), wo_ref[...])
        x = x_ref[...] + mod_ref[2:3, :] * out
        xo_ref[...] = x
        ms = jnp.mean(x * x, axis=-1, keepdims=True)
        h = x * lax.rsqrt(ms + EPS) * g2_ref[...] * (1.0 + mod_ref[4:5, :]) + mod_ref[3:4, :]
        h2_ref[...] = h.astype(BF16)


def _merge(h, wg, bg, y_pool, y_nat_ctx, y_nat_lat, y_conv, y_diff_ctx, y_diff_lat, x, mod_l, norm2_g, wb, wo):
    nct = N_CTX // T_MERGE
    nbr = wb.shape[0]
    yspec = pl.BlockSpec((T_MERGE, BRANCH_W), lambda i, br: (i, 0))
    cspec = pl.BlockSpec((T_MERGE, BRANCH_W), lambda i, br: (jnp.minimum(i, nct - 1), 0))
    lspec = pl.BlockSpec((T_MERGE, BRANCH_W), lambda i, br: (jnp.maximum(i - nct, 0), 0))
    tspec = pl.BlockSpec((T_MERGE, D_MODEL), lambda i, br: (i, 0))
    return pl.pallas_call(
        _merge_kernel,
        grid=(N_TOK // T_MERGE, nbr),
        in_specs=[tspec,
                  pl.BlockSpec((D_MODEL, D_MODEL), lambda i, br: (0, br)),
                  pl.BlockSpec((1, D_MODEL), lambda i, br: (0, br)),
                  yspec, cspec, lspec, yspec, cspec, lspec,
                  tspec,
                  pl.BlockSpec((None, 6, D_MODEL), lambda i, br: (_mod_row(i, T_MERGE), 0, 0)),
                  pl.BlockSpec((1, D_MODEL), lambda i, br: (0, 0)),
                  pl.BlockSpec((None, BRANCH_W, D_MODEL), lambda i, br: (br, 0, 0)),
                  pl.BlockSpec((D_MODEL, D_MODEL), lambda i, br: (0, 0))],
        out_specs=[tspec, tspec],
        out_shape=[jax.ShapeDtypeStruct((N_TOK, D_MODEL), F32),
                   jax.ShapeDtypeStruct((N_TOK, D_MODEL), BF16)],
        scratch_shapes=[pltpu.VMEM((T_MERGE, D_MODEL), F32)],
        compiler_params=_params("parallel", "arbitrary"),
        name="merge",
    )(h, wg, bg, y_pool, y_nat_ctx, y_nat_lat, y_conv, y_diff_ctx, y_diff_lat, x, mod_l, norm2_g, wb, wo)


SUBLANES = 8


def _merge_exchange_pairs(n):
    pairs = []
    t = (n - 1).bit_length()
    p = 1 << (t - 1)
    while p > 0:
        q, r, d = 1 << (t - 1), 0, p
        while d > 0:
            pairs += [(i, i + d) for i in range(n - d) if (i & p) == r]
            d, q, r = q - p, q >> 1, p
        p >>= 1
    return pairs


_SORT16 = _merge_exchange_pairs(PEER_TOPK)


def _cmpx(tiles, i, j):
    a, b = tiles[i], tiles[j]
    if b is None:
        return
    if a is None:
        tiles[i], tiles[j] = b, None
        return
    tiles[i], tiles[j] = jnp.maximum(a, b), jnp.minimum(a, b)


def _top16_sorted(s):
    n = PEER_TOPK
    tiles = [s[j * SUBLANES:(j + 1) * SUBLANES, :] for j in range(s.shape[0] // SUBLANES)]
    tiles += [None] * (n - len(tiles))
    for i, j in _SORT16:
        _cmpx(tiles, i, j)
    for shift in (4, 2, 1):
        merged = []
        for i in range(n):
            a, b = tiles[i], tiles[n - 1 - i]
            b = None if b is None else pltpu.roll(b, shift, 0)
            merged.append(b if a is None else a if b is None else jnp.maximum(a, b))
        tiles = merged
        d = n // 2
        while d > 0:
            for i in range(n):
                if (i & d) == 0:
                    _cmpx(tiles, i, i + d)
            d //= 2
    return tiles


def _rows_to_sublanes(tiles):
    rid = lax.broadcasted_iota(jnp.int32, (SUBLANES, LANE), 0)
    halves = []
    for base in (0, SUBLANES):
        out = tiles[base]
        for k in range(1, SUBLANES):
            out = jnp.where(rid == k, tiles[base + k], out)
        halves.append(out)
    return jnp.concatenate(halves, axis=0)


N_CAND = 16 + 7 * 8 + 8


def _peer_select_kernel(h_ref, wq_ref, sk_ref, r2_ref, e2_ref, brow_ref, crow_ref, chosen_scr):
    q = _dot(h_ref[...], wq_ref[...]).astype(BF16)
    half = PEER_NKEYS
    s1_all = _dot_nt(sk_ref[0].astype(BF16), q[:, :half])
    s2_all = _dot_nt(sk_ref[1].astype(BF16), q[:, half:])
    cid = lax.broadcasted_iota(jnp.int32, (N_CAND, LANE), 0)
    rid8 = lax.broadcasted_iota(jnp.int32, (8, LANE), 0)
    ntile = PEER_NKEYS // SUBLANES
    for c in range(T_SEL // LANE):
        ls = slice(c * LANE, (c + 1) * LANE)
        s1 = s1_all[:, ls]
        s2 = s2_all[:, ls]
        t1 = _top16_sorted(s1)
        t2 = _top16_sorted(s2)
        v1 = _rows_to_sublanes(t1)
        v2 = _rows_to_sublanes(t2)
        cand = jnp.concatenate([v1[0:1] + v2] + [v1[a:a + 1] + v2[0:8] for a in range(1, 8)]
                               + [v1[8:16] + v2[0:1]], axis=0)
        ctop = _top16_sorted(cand)
        zsum = jnp.zeros((1, LANE), F32)
        for k in range(PEER_TOPK):
            zsum = zsum + jnp.exp(ctop[k][0:1] - ctop[0][0:1])
        fast = jnp.where(cand >= ctop[PEER_TOPK - 1][0:1], 1.0, 0.0)
        chosen_scr[...] = fast
        n_fast = jnp.sum(fast, axis=0, keepdims=True)
        tied = jnp.sum(jnp.where(n_fast != float(PEER_TOPK), 1.0, 0.0)) > 0.0

        @pl.when(tied)
        def _():
            rest = cand
            walk = jnp.zeros((N_CAND, LANE), F32)
            for k in range(PEER_TOPK):
                m = jnp.max(rest, axis=0, keepdims=True)
                first = jnp.min(jnp.where(rest == m, cid, N_CAND), axis=0, keepdims=True)
                hit = cid == first
                walk = jnp.where(hit, 1.0, walk)
                rest = jnp.where(hit, -jnp.inf, rest)
            chosen_scr[...] = walk

        chosen = chosen_scr[...]
        cnt_lo = jnp.zeros((8, LANE), F32)
        cnt_lo = jnp.where(rid8 == 0, jnp.sum(chosen[0:16], axis=0, keepdims=True), cnt_lo)
        for a in range(1, 8):
            cnt_lo = jnp.where(rid8 == a, jnp.sum(chosen[8 + 8 * a:16 + 8 * a], axis=0, keepdims=True), cnt_lo)
        cnt = jnp.concatenate([cnt_lo, chosen[N_CAND - 8:N_CAND]], axis=0)
        cnt_rows = [jnp.broadcast_to(cnt[a:a + 1], (SUBLANES, LANE)) for a in range(PEER_TOPK)]
        inv_z = 1.0 / zsum
        for jj in range(ntile // 2):
            ranks, e2s = [], []
            for j in (2 * jj, 2 * jj + 1):
                rows = slice(j * SUBLANES, (j + 1) * SUBLANES)
                d1 = s1[rows]
                d2 = s2[rows]
                brow = jnp.zeros((SUBLANES, LANE), F32)
                rank2 = jnp.zeros((SUBLANES, LANE), F32)
                for a in range(PEER_TOPK):
                    brow = jnp.where(d1 == t1[a], cnt_rows[a], brow)
                    rank2 = jnp.where(t2[a] > d2, float(a + 1), rank2)
                brow_ref[rows, ls] = brow
                crow_ref[rows, ls] = jnp.exp(d1 - t1[0]) * inv_z
                ranks.append(rank2)
                e2s.append(jnp.exp(d2 - t2[0]))
            rows16 = slice(jj * 2 * SUBLANES, (jj + 1) * 2 * SUBLANES)
            r2_ref[rows16, ls] = jnp.concatenate(ranks, axis=0).astype(BF16)
            e2_ref[rows16, ls] = jnp.concatenate(e2s, axis=0).astype(BF16)


def _peer_select(h2, wq, sub_keys):
    nt = N_TOK // T_SEL
    kspec = pl.BlockSpec((None, PEER_NKEYS, T_SEL), lambda i, h: (h, 0, i))

    def kshape(dt):
        return jax.ShapeDtypeStruct((PEER_HEADS, PEER_NKEYS, N_TOK), dt)

    return pl.pallas_call(
        _peer_select_kernel,
        grid=(nt, PEER_HEADS),
        in_specs=[pl.BlockSpec((T_SEL, D_MODEL), lambda i, h: (i, 0)),
                  pl.BlockSpec((D_MODEL, 2 * PEER_NKEYS), lambda i, h: (0, h)),
                  pl.BlockSpec((None, 2, PEER_NKEYS, PEER_NKEYS), lambda i, h: (h, 0, 0, 0))],
        out_specs=[kspec, kspec, kspec, kspec],
        out_shape=[kshape(BF16), kshape(BF16), kshape(F32), kshape(F32)],
        scratch_shapes=[pltpu.VMEM((N_CAND, LANE), F32)],
        compiler_params=_params("parallel", "arbitrary"),
        name="peer_select",
    )(h2, wq, sub_keys)


E_PAIR = 2 * PEER_NKEYS
BF16_ROWS = 16


def _row_bf16(row):
    return jnp.broadcast_to(row, (BF16_ROWS, LANE)).astype(BF16)


def _gelu(x):
    return 0.5 * x * (1.0 + lax.erf(x * (2.0 ** -0.5)))


def _peer_dense_kernel(h_ref, u_ref, v_ref, brow_ref, crow_ref, r2_ref, e2_ref, x_ref, mod_ref,
                       o_ref, acc_ref, a_scr, p_scr):
    c = pl.program_id(1)

    @pl.when(c == 0)
    def _():
        acc_ref[...] = jnp.zeros_like(acc_ref)

    hb = h_ref[...]
    zero = jnp.zeros((BF16_ROWS, LANE), BF16)
    npair = E_CHUNK // E_PAIR
    for j in range(npair + 1):
        slot = j % 2
        if j < npair:
            a_scr[slot] = _dot_nt(u_ref[j * E_PAIR:(j + 1) * E_PAIR, :].astype(BF16), hb)
        if j > 0:
            acc_ref[...] += _dot_tn(p_scr[1 - slot], v_ref[(j - 1) * E_PAIR:j * E_PAIR, :].astype(BF16))
        if j == npair:
            break
        for half in range(2):
            n1l = 2 * j + half
            for tc in range(T_PEER // LANE):
                ls = slice(tc * LANE, (tc + 1) * LANE)
                b16 = [_row_bf16(brow_ref[h, n1l:n1l + 1, ls]) for h in range(PEER_HEADS)]
                c16 = [_row_bf16(crow_ref[h, n1l:n1l + 1, ls]) for h in range(PEER_HEADS)]
                e0 = half * PEER_NKEYS
                act = _gelu(a_scr[slot, e0:e0 + PEER_NKEYS, ls]).astype(BF16)
                for rg in range(PEER_NKEYS // BF16_ROWS):
                    rs = slice(rg * BF16_ROWS, (rg + 1) * BF16_ROWS)
                    g = None
                    for h in range(PEER_HEADS):
                        t = jnp.where(r2_ref[h, rs, ls] < b16[h], e2_ref[h, rs, ls], zero) * c16[h]
                        g = t if g is None else g + t
                    p_scr[slot, e0 + rg * BF16_ROWS:e0 + (rg + 1) * BF16_ROWS, ls] = g * act[rs]

    @pl.when(c == pl.num_programs(1) - 1)
    def _():
        o_ref[...] = x_ref[...] + mod_ref[5:6, :] * acc_ref[...]


def _peer_dense(h2, peer_u, peer_v, layer, r2, e2, brow, crow, x, mod_l):
    nt = N_TOK // T_PEER
    nc = PEER_N // E_CHUNK
    n1c = E_CHUNK // PEER_NKEYS
    rowspec = pl.BlockSpec((PEER_HEADS, n1c, T_PEER), lambda i, c: (0, c, i))
    fullspec = pl.BlockSpec((PEER_HEADS, PEER_NKEYS, T_PEER), lambda i, c: (0, 0, i))
    return pl.pallas_call(
        _peer_dense_kernel,
        grid=(nt, nc),
        in_specs=[pl.BlockSpec((T_PEER, D_MODEL), lambda i, c: (i, 0)),
                  pl.BlockSpec((None, E_CHUNK, D_MODEL), lambda i, c: (layer, c, 0)),
                  pl.BlockSpec((None, E_CHUNK, D_MODEL), lambda i, c: (layer, c, 0)),
                  rowspec, rowspec, fullspec, fullspec,
                  pl.BlockSpec((T_PEER, D_MODEL), lambda i, c: (i, 0)),
                  pl.BlockSpec((None, 6, D_MODEL), lambda i, c: (_mod_row(i, T_PEER), 0, 0))],
        out_specs=pl.BlockSpec((T_PEER, D_MODEL), lambda i, c: (i, 0)),
        out_shape=jax.ShapeDtypeStruct((N_TOK, D_MODEL), F32),
        scratch_shapes=[pltpu.VMEM((T_PEER, D_MODEL), F32),
                        pltpu.VMEM((2, E_PAIR, T_PEER), F32),
                        pltpu.VMEM((2, E_PAIR, T_PEER), BF16)],
        compiler_params=_params("parallel", "arbitrary"),
        name="peer_dense",
    )(h2, peer_u, peer_v, brow, crow, r2, e2, x, mod_l)


def kernel(x_prompt, x_sample, cache_nat_k, cache_nat_v, cache_diff_k, cache_diff_v, c, c_ctx, w_ada, b_ada, norm1_g, norm2_g, w_in, pool_w, pool_scale, nat_q_g, nat_k_g, nat_rel_bias, conv_w, conv_b, conv_ln_g, conv_ln_b, diff_q_g, diff_k_g, diff_lambda_p, diff_subln_g, w_branch, w_gate, b_gate, w_out, peer_w_query, peer_sub_keys, peer_u, peer_v):
    x = jnp.concatenate([x_prompt.reshape(N_CTX, D_MODEL), x_sample.reshape(N_LAT, D_MODEL)], axis=0)
    cvec = jnp.concatenate([c_ctx[None, :], c, jnp.zeros((8 - 1 - DEC_BATCH, D_MODEL), F32)], axis=0)
    mod = _modulation(cvec, w_ada, b_ada).reshape(DEPTH, 8, 6, D_MODEL)

    gid = jnp.arange(CB) // NAT_HD
    gmat = (gid[:, None] == gid[None, :]).astype(BF16)
    rope = _rope_tables(T_PRE)
    ck_n = cache_nat_k.reshape(DEC_BATCH, DEPTH, PAST_LEN, BRANCH_W)
    cv_n = cache_nat_v.reshape(DEC_BATCH, DEPTH, PAST_LEN, BRANCH_W)
    ck_d = cache_diff_k.reshape(DEC_BATCH, DEPTH, PAST_LEN, BRANCH_W)
    cv_d = cache_diff_v.reshape(DEC_BATCH, DEPTH, PAST_LEN, BRANCH_W)

    states = []
    for l in range(DEPTH):
        gg = jnp.stack([jnp.tile(nat_q_g[l], NAT_HEADS), jnp.tile(nat_k_g[l], NAT_HEADS),
                        jnp.tile(diff_q_g[l], 2 * DIFF_HEADS), jnp.tile(diff_k_g[l], 2 * DIFF_HEADS)])
        h1, zp, zn = _pre(x, mod[l], norm1_g[l].reshape(1, -1), w_in[l].astype(BF16),
                          gg.reshape(len(NORM_ZB), 1, CB), gmat, rope)

        y_pool, y_conv = _seq_mixers(zp, pool_w[l], pool_scale[l], conv_w[l], conv_b[l],
                                     conv_ln_g[l], conv_ln_b[l])
        y_nat_ctx = _nat_ctx(zp, zn)
        y_nat_lat = _nat_lat(zp, zn, ck_n, cv_n, *_nat_bias_table(nat_rel_bias[l]), l)
        y_diff_ctx = _diff_ctx(zp, zn, diff_lambda_p[l], diff_subln_g[l], l)
        y_diff_lat = _diff_lat(zp, zn, ck_d, cv_d, diff_lambda_p[l], diff_subln_g[l], l)

        x, h2 = _merge(h1, w_gate[l].astype(BF16), b_gate[l].reshape(1, -1), y_pool, y_nat_ctx, y_nat_lat,
                       y_conv, y_diff_ctx, y_diff_lat, x, mod[l], norm2_g[l].reshape(1, -1),
                       w_branch[l].astype(BF16), w_out[l].astype(BF16))
        r2, e2, brow, crow = _peer_select(h2, peer_w_query[l].astype(BF16), peer_sub_keys[l])
        x = _peer_dense(h2, peer_u, peer_v, l, r2, e2, brow, crow, x, mod[l])

        states.append([zn[:N_CTX, Q_NK * CB:(Q_NK + 1) * CB], zp[:N_CTX, P_NV * CB:(P_NV + 1) * CB],
                       zn[:N_CTX, Q_DK * CB:(Q_DK + 1) * CB], zp[:N_CTX, P_DV * CB:(P_DV + 1) * CB]])

    def stack(idx, shape):
        return jnp.stack([states[l][idx].reshape((BATCH, SEQ) + shape) for l in range(DEPTH)], axis=1)

    return (x[:N_CTX].reshape(BATCH, SEQ, D_MODEL),
            x[N_CTX:].reshape(DEC_BATCH, DEC_SEQ, D_MODEL),
            stack(0, (NAT_HEADS, NAT_HD)),
            stack(1, (NAT_HEADS, NAT_HD)),
            stack(2, (DIFF_HEADS, 2, DIFF_HD)),
            stack(3, (DIFF_HEADS, 2 * DIFF_HD)))
```

```python
import functools
import math

import jax
import jax.numpy as jnp
from jax import lax
from jax.experimental import pallas as pl
from jax.experimental.pallas import tpu as pltpu

F32 = jnp.float32
BF16 = jnp.bfloat16

D_MODEL = 1024
BATCH = 16
SEQ = 256
DEPTH = 2
DEC_BATCH = 4
DEC_SEQ = 2048
PAST_LEN = 256
GRID_W = 64
BRANCH_W = 512
POOL_WINDOWS = (2, 4, 8, 16)
POOL_GROUP = 128
NAT_HEADS = 8
NAT_HD = 64
NAT_WIN_R = 8
NAT_WIN_C = 16
CONV_W = 31
DIFF_HEADS = 4
DIFF_HD = 64
PEER_HEADS = 8
PEER_NKEYS = 128
PEER_N = PEER_NKEYS * PEER_NKEYS
PEER_TOPK = 16
ROPE_BASE = 10000.0
EPS = 1e-6
NEG_INF = -1e30
LOG2E = math.log2(math.e)

N_CTX = BATCH * SEQ
N_LAT = DEC_BATCH * DEC_SEQ
N_TOK = N_CTX + N_LAT
GRID_ROWS = DEC_SEQ // GRID_W

CB = 512
ZB_POOL, ZB_NQ, ZB_NK, ZB_NV, ZB_CA, ZB_CGATE, ZB_DQ, ZB_DK, ZB_DV = range(9)

V7X_VMEM_LIMIT = 52 * 1024 * 1024

T_PRE = 2048
T_SEQ = 256
HALO = 16
T_MERGE = 512
T_SEL = 1024
T_PEER = 512
E_CHUNK = 1024
LANE = 128


def _sigmoid(x):
    return 1.0 / (1.0 + jnp.exp(-x))


def _dot(a, b):
    return jnp.dot(a, b, preferred_element_type=F32)


def _dot_nt(a, b):
    return lax.dot_general(a, b, (((1,), (1,)), ((), ())), preferred_element_type=F32)


def _dot_tn(a, b):
    return lax.dot_general(a, b, (((0,), (0,)), ((), ())), preferred_element_type=F32)


def _split_bf16(a):
    hi = a.astype(BF16)
    lo = (a - hi.astype(F32)).astype(BF16)
    return hi, lo


def _params(*sem):
    return pltpu.CompilerParams(dimension_semantics=sem, vmem_limit_bytes=V7X_VMEM_LIMIT)


def _mod_row(i, tile):
    nct = N_CTX // tile
    per = DEC_SEQ // tile
    return jnp.where(i < nct, 0, 1 + (i - nct) // per)


def _mod_kernel(c_ref, w_ref, b_ref, o_ref):
    c = c_ref[...]
    a = c * _sigmoid(c)
    w = w_ref[0]
    a_hi, a_lo = _split_bf16(a)
    w_hi, w_lo = _split_bf16(w)
    o_ref[0] = _dot(a_hi, w_hi) + _dot(a_lo, w_hi) + _dot(a_hi, w_lo) + b_ref[0]


def _modulation(cvec, w_ada, b_ada):
    tn = 1024
    return pl.pallas_call(
        _mod_kernel,
        grid=(DEPTH, 6 * D_MODEL // tn),
        in_specs=[
            pl.BlockSpec((8, D_MODEL), lambda l, j: (0, 0)),
            pl.BlockSpec((1, D_MODEL, tn), lambda l, j: (l, 0, j)),
            pl.BlockSpec((1, 1, tn), lambda l, j: (l, 0, j)),
        ],
        out_specs=pl.BlockSpec((1, 8, tn), lambda l, j: (l, 0, j)),
        out_shape=jax.ShapeDtypeStruct((DEPTH, 8, 6 * D_MODEL), F32),
        compiler_params=_params("parallel", "parallel"),
        name="modulation",
    )(cvec, w_ada, b_ada.reshape(DEPTH, 1, 6 * D_MODEL))


def _tile4(t):
    return jnp.concatenate([t, t, t, t], axis=1)


def _hnorm_kernel(x_ref, mod_ref, g_ref, h_ref):
    x = x_ref[...]
    ms = jnp.mean(x * x, axis=-1, keepdims=True)
    y = x * lax.rsqrt(ms + EPS) * g_ref[...]
    h_ref[...] = (y * (1.0 + mod_ref[1:2, :]) + mod_ref[0:1, :]).astype(h_ref.dtype)


def _proj_plain_kernel(h_ref, w_ref, o_ref):
    o_ref[...] = _dot(h_ref[...], w_ref[...])


def _proj_norm_kernel(h_ref, w_ref, gg_ref, gmat_ref, rc_ref, rp_ref, rm_ref, o_ref):
    acc = _dot(h_ref[...], w_ref[...])
    hi, lo = _split_bf16(acc * acc)
    ss = _dot(hi, gmat_ref[...]) + _dot(lo, gmat_ref[...])
    y = acc * lax.rsqrt(ss * (1.0 / NAT_HD) + EPS) * gg_ref[0]
    o_ref[...] = (y * _tile4(rc_ref[...])
                  + pltpu.roll(y, 16, 1) * _tile4(rp_ref[...])
                  + pltpu.roll(y, CB - 16, 1) * _tile4(rm_ref[...]))


def _rope_tables(tile):
    quarter = DIFF_HD // 4
    t = jnp.arange(DEC_SEQ)
    lane = jnp.arange(LANE)
    d = lane % DIFF_HD
    freqs = ROPE_BASE ** (-jnp.arange(quarter, dtype=F32) / quarter)
    pos = jnp.where(d[None, :] < DIFF_HD // 2, (t // GRID_W)[:, None], (t % GRID_W)[:, None]).astype(F32)
    ang = pos * freqs[d % quarter][None, :]
    cos = jnp.cos(ang)
    sin = jnp.sin(ang)
    second = (d % (2 * quarter)) >= quarter
    s_plus = jnp.where(second[None, :], sin, 0.0)
    s_minus = jnp.where(second[None, :], 0.0, -sin)
    ones = jnp.ones((tile, LANE), F32)
    zeros = jnp.zeros((tile, LANE), F32)
    return (jnp.concatenate([cos, ones], 0), jnp.concatenate([s_plus, zeros], 0),
            jnp.concatenate([s_minus, zeros], 0))


PLAIN_ZB = (ZB_POOL, ZB_NV, ZB_CA, ZB_CGATE, ZB_DV)
NORM_ZB = (ZB_NQ, ZB_NK, ZB_DQ, ZB_DK)
P_POOL, P_NV, P_CA, P_CGATE, P_DV = range(5)
Q_NQ, Q_NK, Q_DQ, Q_DK = range(4)


def _pick(j, values):
    out = values[0]
    for n, v in enumerate(values[1:], start=1):
        out = jnp.where(j == n, v, out)
    return out


def _pre(x, mod_l, norm_g, w_in, gg, gmat, rope):
    nct = N_CTX // T_PRE
    per = DEC_SEQ // T_PRE
    ident_blk = DEC_SEQ // T_PRE
    nt = N_TOK // T_PRE
    h = pl.pallas_call(
        _hnorm_kernel,
        grid=(nt,),
        in_specs=[pl.BlockSpec((T_PRE, D_MODEL), lambda i: (i, 0)),
                  pl.BlockSpec((None, 6, D_MODEL), lambda i: (_mod_row(i, T_PRE), 0, 0)),
                  pl.BlockSpec((1, D_MODEL), lambda i: (0, 0))],
        out_specs=pl.BlockSpec((T_PRE, D_MODEL), lambda i: (i, 0)),
        out_shape=jax.ShapeDtypeStruct((N_TOK, D_MODEL), BF16),
        compiler_params=_params("parallel"),
        name="hnorm",
    )(x, mod_l, norm_g)

    hspec = pl.BlockSpec((T_PRE, D_MODEL), lambda i, j: (i, 0))
    ospec = pl.BlockSpec((T_PRE, CB), lambda i, j: (i, j))

    zp = pl.pallas_call(
        _proj_plain_kernel,
        grid=(nt, len(PLAIN_ZB)),
        in_specs=[hspec, pl.BlockSpec((D_MODEL, CB), lambda i, j: (0, _pick(j, PLAIN_ZB)))],
        out_specs=ospec,
        out_shape=jax.ShapeDtypeStruct((N_TOK, len(PLAIN_ZB) * CB), F32),
        compiler_params=_params("parallel", "arbitrary"),
        name="proj_plain",
    )(h, w_in)

    def rope_idx(i, j):
        use = ((j == Q_DQ) | (j == Q_DK)) & (i >= nct)
        return jnp.where(use, (i - nct) % per, ident_blk), 0

    rope_spec = pl.BlockSpec((T_PRE, LANE), rope_idx)
    zn = pl.pallas_call(
        _proj_norm_kernel,
        grid=(nt, len(NORM_ZB)),
        in_specs=[hspec, pl.BlockSpec((D_MODEL, CB), lambda i, j: (0, _pick(j, NORM_ZB))),
                  pl.BlockSpec((1, 1, CB), lambda i, j: (j, 0, 0)),
                  pl.BlockSpec((CB, CB), lambda i, j: (0, 0)),
                  rope_spec, rope_spec, rope_spec],
        out_specs=ospec,
        out_shape=jax.ShapeDtypeStruct((N_TOK, len(NORM_ZB) * CB), F32),
        compiler_params=_params("parallel", "arbitrary"),
        name="proj_norm",
    )(h, w_in, gg, gmat, *rope)

    return h, zp, zn


def _seq_kernel(zp_ref, zp_prev, zp_next, za_ref, za_prev, za_next, zb_ref, zb_prev, zb_next,
                pw_ref, ps_ref, cw_ref, cb_ref, lg_ref, lb_ref,
                yp_ref, yc_ref, pbuf, ubuf, ybuf, shbuf):
    i = pl.program_id(0)
    nct = N_CTX // T_SEQ
    per = DEC_SEQ // T_SEQ
    is_lat = i >= nct
    kk = (i - nct) % per
    has_prev = is_lat & (kk != 0)
    has_next = is_lat & (kk != per - 1)
    pos0 = jnp.where(is_lat, kk * T_SEQ, 0)
    seq_len = jnp.where(is_lat, DEC_SEQ, SEQ)
    fp = jnp.where(has_prev, 1.0, 0.0).astype(F32)
    fn = jnp.where(has_next, 1.0, 0.0).astype(F32)

    pbuf[0:HALO, :] = zp_prev[...] * fp
    pbuf[HALO:HALO + T_SEQ, :] = zp_ref[...]
    pbuf[HALO + T_SEQ:, :] = zp_next[...] * fn
    ubuf[0:HALO, :] = za_prev[...] * _sigmoid(zb_prev[...]) * fp
    ubuf[HALO:HALO + T_SEQ, :] = za_ref[...] * _sigmoid(zb_ref[...])
    ubuf[HALO + T_SEQ:, :] = za_next[...] * _sigmoid(zb_next[...]) * fn

    pos = pos0 + lax.broadcasted_iota(jnp.int32, (T_SEQ, LANE), 0)
    for gi, w in enumerate(POOL_WINDOWS):
        ls = slice(gi * LANE, (gi + 1) * LANE)
        acc = None
        for s in range(-(w // 2), w - w // 2):
            v = pbuf[HALO + s:HALO + s + T_SEQ, ls]
            acc = v if acc is None else acc + v
        lo = jnp.maximum(pos - w // 2, 0)
        hi = jnp.minimum(pos + (w - w // 2), seq_len)
        mean = acc / (hi - lo).astype(F32)
        d = mean - pbuf[HALO:HALO + T_SEQ, ls]
        y = _dot(d.astype(BF16), pw_ref[gi].astype(BF16)) * ps_ref[:, ls]
        yp_ref[:, ls] = y.astype(yp_ref.dtype)

    base = HALO - CONV_W // 2
    span = shbuf.shape[0]
    for b in range(8):
        shbuf[...] = ubuf[base + b:base + b + span, :]
        for c in range(BRANCH_W // LANE):
            ls = slice(c * LANE, (c + 1) * LANE)
            acc = cb_ref[:, ls] if b == 0 else ybuf[:, ls]
            for k in range(b, CONV_W, 8):
                acc = acc + shbuf[k - b:k - b + T_SEQ, ls] * cw_ref[k:k + 1, ls]
            ybuf[:, ls] = acc
    y = ybuf[...]
    mu = jnp.mean(y, axis=-1, keepdims=True)
    yc = y - mu
    var = jnp.mean(yc * yc, axis=-1, keepdims=True)
    yn = yc * lax.rsqrt(var + EPS) * lg_ref[...] + lb_ref[...]
    yc_ref[...] = (yn * _sigmoid(yn)).astype(yc_ref.dtype)


def _seq_mixers(zp, pool_w, pool_scale, conv_w, conv_b, ln_g, ln_b):
    nt = N_TOK // T_SEQ
    hb = T_SEQ // HALO
    n_hblk = N_TOK // HALO

    def specs(cb):
        return [
            pl.BlockSpec((T_SEQ, CB), lambda i: (i, cb)),
            pl.BlockSpec((HALO, CB), lambda i: (jnp.maximum(i * hb - 1, 0), cb)),
            pl.BlockSpec((HALO, CB), lambda i: (jnp.minimum((i + 1) * hb, n_hblk - 1), cb)),
        ]

    def full(shape):
        return pl.BlockSpec(shape, lambda i: (0,) * len(shape))

    cw = jnp.concatenate([conv_w, jnp.zeros((1, BRANCH_W), F32)], axis=0)
    out_spec = pl.BlockSpec((T_SEQ, BRANCH_W), lambda i: (i, 0))
    return pl.pallas_call(
        _seq_kernel,
        grid=(nt,),
        in_specs=(specs(P_POOL) + specs(P_CA) + specs(P_CGATE)
                  + [full((4, POOL_GROUP, POOL_GROUP)), full((1, BRANCH_W)), full((CONV_W + 1, BRANCH_W)),
                     full((1, BRANCH_W)), full((1, BRANCH_W)), full((1, BRANCH_W))]),
        out_specs=[out_spec, out_spec],
        out_shape=[jax.ShapeDtypeStruct((N_TOK, BRANCH_W), BF16)] * 2,
        scratch_shapes=[pltpu.VMEM((T_SEQ + 2 * HALO, BRANCH_W), F32),
                        pltpu.VMEM((T_SEQ + 2 * HALO, BRANCH_W), F32),
                        pltpu.VMEM((T_SEQ, BRANCH_W), F32),
                        pltpu.VMEM((T_SEQ + 8 * ((CONV_W - 1) // 8), BRANCH_W), F32)],
        compiler_params=_params("parallel"),
        name="pool_conv",
    )(zp, zp, zp, zp, zp, zp, zp, zp, zp, pool_w, pool_scale.reshape(1, -1), cw,
      conv_b.reshape(1, -1), ln_g.reshape(1, -1), ln_b.reshape(1, -1))


def _nat_ctx_kernel(q_ref, k_ref, v_ref, o_ref):
    for h in range(NAT_HEADS):
        sl = slice(h * NAT_HD, (h + 1) * NAT_HD)
        q = (q_ref[:, sl] * (NAT_HD ** -0.5 * LOG2E)).astype(BF16)
        s = _dot_nt(q, k_ref[:, sl].astype(BF16))
        m = jnp.max(s, axis=-1, keepdims=True)
        e = jnp.exp2(s - m)
        den = jnp.sum(e, axis=-1, keepdims=True)
        o = _dot(e.astype(BF16), v_ref[:, sl].astype(BF16)) / den
        o_ref[:, sl] = o.astype(o_ref.dtype)


def _nat_ctx(zp, zn):
    def spec(cb):
        return pl.BlockSpec((SEQ, CB), lambda b: (b, cb))

    return pl.pallas_call(
        _nat_ctx_kernel,
        grid=(BATCH,),
        in_specs=[spec(Q_NQ), spec(Q_NK), spec(P_NV)],
        out_specs=pl.BlockSpec((SEQ, BRANCH_W), lambda b: (b, 0)),
        out_shape=jax.ShapeDtypeStruct((N_CTX, BRANCH_W), BF16),
        compiler_params=_params("parallel"),
        name="nat_ctx",
    )(zn, zn, zp)


NAT_QROWS = 4
NAT_KROWS = 12


def _nat_row0(r):
    return jnp.clip(r - NAT_WIN_R // 2, 0, GRID_ROWS - NAT_WIN_R)


def _nat_lat_kernel(q_ref, k_ref, v_ref, ck_ref, cv_ref, bias_ref, o_ref):
    g = pl.program_id(1)
    w0 = jnp.minimum(_nat_row0(g * NAT_QROWS), GRID_ROWS - NAT_KROWS)
    start = pl.multiple_of(w0 * GRID_W, GRID_W)
    nwin = NAT_KROWS * GRID_W
    for h in range(NAT_HEADS):
        sl = slice(h * NAT_HD, (h + 1) * NAT_HD)
        q = (q_ref[:, sl] * (NAT_HD ** -0.5 * LOG2E)).astype(BF16)
        kw = k_ref[pl.ds(start, nwin), sl].astype(BF16)
        vw = v_ref[pl.ds(start, nwin), sl].astype(BF16)
        s_loc = _dot_nt(q, kw) + bias_ref[h]
        s_ctx = _dot_nt(q, ck_ref[:, sl].astype(BF16))
        m = jnp.maximum(jnp.max(s_loc, axis=-1, keepdims=True), jnp.max(s_ctx, axis=-1, keepdims=True))
        e_loc = jnp.exp2(s_loc - m)
        e_ctx = jnp.exp2(s_ctx - m)
        den = jnp.sum(e_loc, axis=-1, keepdims=True) + jnp.sum(e_ctx, axis=-1, keepdims=True)
        o = (_dot(e_loc.astype(BF16), vw) + _dot(e_ctx.astype(BF16), cv_ref[:, sl].astype(BF16))) / den
        o_ref[:, sl] = o.astype(o_ref.dtype)


def _nat_bias_kernel(rb_ref, oh_ref, o_ref):
    x = rb_ref[...]
    hi = x.astype(BF16)
    r1 = x - hi.astype(F32)
    mid = r1.astype(BF16)
    lo = (r1 - mid.astype(F32)).astype(BF16)
    oh = oh_ref[...]
    o_ref[...] = _dot(hi, oh) + _dot(mid, oh) + _dot(lo, oh)


def _nat_bias_table(rel_bias):
    ndr = 2 * NAT_WIN_R - 1
    ndc = 2 * NAT_WIN_C - 1
    q = jnp.arange(GRID_W)
    kc = jnp.arange(GRID_W)
    dcol = jnp.clip(kc[None, :] - q[:, None] + NAT_WIN_C - 1, 0, ndc - 1)
    wstart = jnp.clip(q - NAT_WIN_C // 2, 0, GRID_W - NAT_WIN_C)
    valid = (kc[None, :] >= wstart[:, None]) & (kc[None, :] < wstart[:, None] + NAT_WIN_C)
    d = jnp.arange(LANE)
    onehot = jnp.where(d[:, None, None] == ndc, jnp.logical_not(valid)[None],
                       (d[:, None, None] == dcol[None]) & valid[None])
    onehot = onehot.reshape(LANE, GRID_W * GRID_W).astype(BF16)
    nrow = NAT_HEADS * ndr
    rb = jnp.concatenate([rel_bias.reshape(nrow, ndc).astype(F32) * LOG2E, jnp.full((nrow, 1), NEG_INF, F32),
                          jnp.zeros((nrow, LANE - ndc - 1), F32)], axis=1)
    rb = jnp.concatenate([rb, jnp.zeros((LANE - nrow, LANE), F32)], axis=0)
    tcol = pl.pallas_call(
        _nat_bias_kernel,
        out_shape=jax.ShapeDtypeStruct((LANE, GRID_W * GRID_W), F32),
        compiler_params=pltpu.CompilerParams(vmem_limit_bytes=V7X_VMEM_LIMIT),
        name="nat_bias",
    )(rb, onehot)
    tcol = tcol[:nrow].reshape(NAT_HEADS, ndr, GRID_W, GRID_W)
    neg = jnp.full((NAT_HEADS, GRID_W, GRID_W), NEG_INF, F32)
    tables, variant_of_group, seen = [], [], {}
    for g in range(GRID_ROWS // NAT_QROWS):
        r_first = g * NAT_QROWS
        w0 = min(max(r_first - NAT_WIN_R // 2, 0), GRID_ROWS - NAT_WIN_R, GRID_ROWS - NAT_KROWS)
        rows = [(r_first + j - w0, min(max(r_first + j - NAT_WIN_R // 2, 0), GRID_ROWS - NAT_WIN_R) - w0)
                for j in range(NAT_QROWS)]
        key = tuple(rows)
        if key not in seen:
            seen[key] = len(tables)
            blocks = []
            for rq, rw in rows:
                blocks.append(jnp.concatenate(
                    [tcol[:, i - rq + NAT_WIN_R - 1] if rw <= i < rw + NAT_WIN_R else neg
                     for i in range(NAT_KROWS)], axis=-1))
            tables.append(jnp.concatenate(blocks, axis=1))
        variant_of_group.append(seen[key])
    return jnp.stack(tables, axis=0), tuple(variant_of_group)


def _nat_lat(zp, zn, cache_k, cache_v, bias_tab, variant_of_group, layer):
    tq = NAT_QROWS * GRID_W
    ngroups = GRID_ROWS // NAT_QROWS
    q_blk0 = N_CTX // tq
    kv_blk0 = N_CTX // DEC_SEQ
    cache_spec = pl.BlockSpec((None, None, PAST_LEN, BRANCH_W), lambda b, g: (b, layer, 0, 0))
    return pl.pallas_call(
        _nat_lat_kernel,
        grid=(DEC_BATCH, ngroups),
        in_specs=[
            pl.BlockSpec((tq, CB), lambda b, g: (q_blk0 + b * ngroups + g, Q_NQ)),
            pl.BlockSpec((DEC_SEQ, CB), lambda b, g: (kv_blk0 + b, Q_NK)),
            pl.BlockSpec((DEC_SEQ, CB), lambda b, g: (kv_blk0 + b, P_NV)),
            cache_spec, cache_spec,
            pl.BlockSpec((None, NAT_HEADS, tq, NAT_KROWS * GRID_W),
                         lambda b, g: (_pick(g, variant_of_group), 0, 0, 0)),
        ],
        out_specs=pl.BlockSpec((tq, BRANCH_W), lambda b, g: (b * ngroups + g, 0)),
        out_shape=jax.ShapeDtypeStruct((N_LAT, BRANCH_W), BF16),
        compiler_params=_params("parallel", "arbitrary"),
        name="nat_lat",
    )(zn, zn, zp, cache_k, cache_v, bias_tab)


def _diff_kernel(has_cache, lam_init, *refs):
    if has_cache:
        q_ref, k_ref, v_ref, ck_ref, cv_ref, lamp_ref, g_ref, o_ref = refs
    else:
        q_ref, k_ref, v_ref, lamp_ref, g_ref, o_ref = refs
    lp = lamp_ref[...]
    lam = (jnp.exp(jnp.sum(lp[0:1] * lp[1:2], axis=-1, keepdims=True))
           - jnp.exp(jnp.sum(lp[2:3] * lp[3:4], axis=-1, keepdims=True)) + lam_init)
    hv = 2 * DIFF_HD
    for h in range(DIFF_HEADS):
        vs = slice(h * hv, (h + 1) * hv)
        vb = v_ref[:, vs].astype(BF16)
        if has_cache:
            cvb = cv_ref[:, vs].astype(BF16)
        parts = []
        for i in range(2):
            sl = slice(h * hv + i * DIFF_HD, h * hv + (i + 1) * DIFF_HD)
            q = (q_ref[:, sl] * (DIFF_HD ** -0.5 * LOG2E)).astype(BF16)
            s = _dot_nt(q, k_ref[:, sl].astype(BF16))
            m = jnp.max(s, axis=-1, keepdims=True)
            if has_cache:
                sc = _dot_nt(q, ck_ref[:, sl].astype(BF16))
                m = jnp.maximum(m, jnp.max(sc, axis=-1, keepdims=True))
            e = jnp.exp2(s - m)
            den = jnp.sum(e, axis=-1, keepdims=True)
            o = _dot(e.astype(BF16), vb)
            if has_cache:
                ec = jnp.exp2(sc - m)
                den = den + jnp.sum(ec, axis=-1, keepdims=True)
                o = o + _dot(ec.astype(BF16), cvb)
            parts.append((o, 1.0 / den))
        (o1, r1), (o2, r2) = parts
        o = o1 * r1 - o2 * (lam * r2)
        ms = jnp.mean(o * o, axis=-1, keepdims=True)
        y = o * lax.rsqrt(ms + EPS) * g_ref[...] * (1.0 - lam_init)
        o_ref[:, vs] = y.astype(o_ref.dtype)


def _lam_init(layer):
    return 0.8 - 0.6 * math.exp(-0.3 * layer)


def _diff_ctx(zp, zn, lam_p, subln_g, layer):
    def spec(cb):
        return pl.BlockSpec((SEQ, CB), lambda b: (b, cb))

    return pl.pallas_call(
        functools.partial(_diff_kernel, False, _lam_init(layer)),
        grid=(BATCH,),
        in_specs=[spec(Q_DQ), spec(Q_DK), spec(P_DV),
                  pl.BlockSpec((4, DIFF_HD), lambda b: (0, 0)),
                  pl.BlockSpec((1, 2 * DIFF_HD), lambda b: (0, 0))],
        out_specs=pl.BlockSpec((SEQ, BRANCH_W), lambda b: (b, 0)),
        out_shape=jax.ShapeDtypeStruct((N_CTX, BRANCH_W), BF16),
        compiler_params=_params("parallel"),
        name="diff_ctx",
    )(zn, zn, zp, lam_p, subln_g.reshape(1, -1))


T_DQ = 512


def _diff_lat(zp, zn, cache_k, cache_v, lam_p, subln_g, layer):
    nq = DEC_SEQ // T_DQ
    q_blk0 = N_CTX // T_DQ
    kv_blk0 = N_CTX // DEC_SEQ
    cache_spec = pl.BlockSpec((None, None, PAST_LEN, BRANCH_W), lambda b, t: (b, layer, 0, 0))
    return pl.pallas_call(
        functools.partial(_diff_kernel, True, _lam_init(layer)),
        grid=(DEC_BATCH, nq),
        in_specs=[
            pl.BlockSpec((T_DQ, CB), lambda b, t: (q_blk0 + b * nq + t, Q_DQ)),
            pl.BlockSpec((DEC_SEQ, CB), lambda b, t: (kv_blk0 + b, Q_DK)),
            pl.BlockSpec((DEC_SEQ, CB), lambda b, t: (kv_blk0 + b, P_DV)),
            cache_spec, cache_spec,
            pl.BlockSpec((4, DIFF_HD), lambda b, t: (0, 0)),
            pl.BlockSpec((1, 2 * DIFF_HD), lambda b, t: (0, 0)),
        ],
        out_specs=pl.BlockSpec((T_DQ, BRANCH_W), lambda b, t: (b * nq + t, 0)),
        out_shape=jax.ShapeDtypeStruct((N_LAT, BRANCH_W), BF16),
        compiler_params=_params("parallel", "arbitrary"),
        name="diff_lat",
    )(zn, zn, zp, cache_k, cache_v, lam_p, subln_g.reshape(1, -1))


def _merge_kernel(h_ref, wg_ref, bg_ref, yp_ref, ync_ref, ynl_ref, yc_ref, ydc_ref, ydl_ref, x_ref, mod_ref,
                  g2_ref, wb_ref, wo_ref, wq_ref, xo_ref, h2_ref, q_ref, merged_scr):
    br = pl.program_id(1)
    is_ctx = pl.program_id(0) < N_CTX // T_MERGE
    y_nat = jnp.where(is_ctx, ync_ref[...], ynl_ref[...])
    y_diff = jnp.where(is_ctx, ydc_ref[...], ydl_ref[...])
    y = jnp.where(br == 0, yp_ref[...], jnp.where(br == 1, y_nat, jnp.where(br == 2, yc_ref[...], y_diff)))
    gate = _sigmoid(_dot(h_ref[...], wg_ref[...]) + bg_ref[...])
    t = gate * _dot(y, wb_ref[...])

    @pl.when(br == 0)
    def _():
        merged_scr[...] = t

    @pl.when(br > 0)
    def _():
        merged_scr[...] += t

    @pl.when(br == pl.num_programs(1) - 1)
    def _():
        out = _dot(merged_scr[...].astype(BF16), wo_ref[...])
        x = x_ref[...] + mod_ref[2:3, :] * out
        xo_ref[...] = x
        ms = jnp.mean(x * x, axis=-1, keepdims=True)
        h = x * lax.rsqrt(ms + EPS) * g2_ref[...] * (1.0 + mod_ref[4:5, :]) + mod_ref[3:4, :]
        hb = h.astype(BF16)
        h2_ref[...] = hb
        q_ref[...] = _dot(hb, wq_ref[...])


def _merge(h, wg, bg, y_pool, y_nat_ctx, y_nat_lat, y_conv, y_diff_ctx, y_diff_lat, x, mod_l, norm2_g, wb, wo, wq):
    nct = N_CTX // T_MERGE
    nbr = wb.shape[0]
    yspec = pl.BlockSpec((T_MERGE, BRANCH_W), lambda i, br: (i, 0))
    cspec = pl.BlockSpec((T_MERGE, BRANCH_W), lambda i, br: (jnp.minimum(i, nct - 1), 0))
    lspec = pl.BlockSpec((T_MERGE, BRANCH_W), lambda i, br: (jnp.maximum(i - nct, 0), 0))
    tspec = pl.BlockSpec((T_MERGE, D_MODEL), lambda i, br: (i, 0))
    qcols = wq.shape[1]
    return pl.pallas_call(
        _merge_kernel,
        grid=(N_TOK // T_MERGE, nbr),
        in_specs=[tspec,
                  pl.BlockSpec((D_MODEL, D_MODEL), lambda i, br: (0, br)),
                  pl.BlockSpec((1, D_MODEL), lambda i, br: (0, br)),
                  yspec, cspec, lspec, yspec, cspec, lspec,
                  tspec,
                  pl.BlockSpec((None, 6, D_MODEL), lambda i, br: (_mod_row(i, T_MERGE), 0, 0)),
                  pl.BlockSpec((1, D_MODEL), lambda i, br: (0, 0)),
                  pl.BlockSpec((None, BRANCH_W, D_MODEL), lambda i, br: (br, 0, 0)),
                  pl.BlockSpec((D_MODEL, D_MODEL), lambda i, br: (0, 0)),
                  pl.BlockSpec((D_MODEL, qcols), lambda i, br: (0, 0))],
        out_specs=[tspec, tspec, pl.BlockSpec((T_MERGE, qcols), lambda i, br: (i, 0))],
        out_shape=[jax.ShapeDtypeStruct((N_TOK, D_MODEL), F32),
                   jax.ShapeDtypeStruct((N_TOK, D_MODEL), BF16),
                   jax.ShapeDtypeStruct((N_TOK, qcols), F32)],
        scratch_shapes=[pltpu.VMEM((T_MERGE, D_MODEL), F32)],
        compiler_params=_params("parallel", "arbitrary"),
        name="merge",
    )(h, wg, bg, y_pool, y_nat_ctx, y_nat_lat, y_conv, y_diff_ctx, y_diff_lat, x, mod_l, norm2_g, wb, wo, wq)


SUBLANES = 8


def _merge_exchange_pairs(n):
    pairs = []
    t = (n - 1).bit_length()
    p = 1 << (t - 1)
    while p > 0:
        q, r, d = 1 << (t - 1), 0, p
        while d > 0:
            pairs += [(i, i + d) for i in range(n - d) if (i & p) == r]
            d, q, r = q - p, q >> 1, p
        p >>= 1
    return pairs


_SORT16 = _merge_exchange_pairs(PEER_TOPK)


def _cmpx(tiles, i, j):
    a, b = tiles[i], tiles[j]
    if b is None:
        return
    if a is None:
        tiles[i], tiles[j] = b, None
        return
    tiles[i], tiles[j] = jnp.maximum(a, b), jnp.minimum(a, b)


def _top16_sorted(s):
    n = PEER_TOPK
    tiles = [s[j * SUBLANES:(j + 1) * SUBLANES, :] for j in range(s.shape[0] // SUBLANES)]
    tiles += [None] * (n - len(tiles))
    for i, j in _SORT16:
        _cmpx(tiles, i, j)
    for shift in (4, 2, 1):
        merged = []
        for i in range(n):
            a, b = tiles[i], tiles[n - 1 - i]
            b = None if b is None else pltpu.roll(b, shift, 0)
            merged.append(b if a is None else a if b is None else jnp.maximum(a, b))
        tiles = merged
        d = n // 2
        while d > 0:
            for i in range(n):
                if (i & d) == 0:
                    _cmpx(tiles, i, i + d)
            d //= 2
    return tiles


def _rows_to_sublanes(tiles):
    rid = lax.broadcasted_iota(jnp.int32, (SUBLANES, LANE), 0)
    halves = []
    for base in (0, SUBLANES):
        out = tiles[base]
        for k in range(1, SUBLANES):
            out = jnp.where(rid == k, tiles[base + k], out)
        halves.append(out)
    return jnp.concatenate(halves, axis=0)


N_CAND = 16 + 7 * 8 + 8


def _peer_select_kernel(q_ref, sk_ref, r2_ref, e2_ref, brow_ref, crow_ref, chosen_scr):
    q = q_ref[...].astype(BF16)
    half = PEER_NKEYS
    s1_all = _dot_nt(sk_ref[0].astype(BF16), q[:, :half])
    s2_all = _dot_nt(sk_ref[1].astype(BF16), q[:, half:])
    cid = lax.broadcasted_iota(jnp.int32, (N_CAND, LANE), 0)
    rid8 = lax.broadcasted_iota(jnp.int32, (8, LANE), 0)
    ntile = PEER_NKEYS // SUBLANES
    for c in range(T_SEL // LANE):
        ls = slice(c * LANE, (c + 1) * LANE)
        s1 = s1_all[:, ls]
        s2 = s2_all[:, ls]
        t1 = _top16_sorted(s1)
        t2 = _top16_sorted(s2)
        v1 = _rows_to_sublanes(t1)
        v2 = _rows_to_sublanes(t2)
        cand = jnp.concatenate([v1[0:1] + v2] + [v1[a:a + 1] + v2[0:8] for a in range(1, 8)]
                               + [v1[8:16] + v2[0:1]], axis=0)
        ctop = _top16_sorted(cand)
        zsum = jnp.zeros((1, LANE), F32)
        for k in range(PEER_TOPK):
            zsum = zsum + jnp.exp(ctop[k][0:1] - ctop[0][0:1])
        fast = jnp.where(cand >= ctop[PEER_TOPK - 1][0:1], 1.0, 0.0)
        chosen_scr[...] = fast
        n_fast = jnp.sum(fast, axis=0, keepdims=True)
        tied = jnp.sum(jnp.where(n_fast != float(PEER_TOPK), 1.0, 0.0)) > 0.0

        @pl.when(tied)
        def _():
            rest = cand
            walk = jnp.zeros((N_CAND, LANE), F32)
            for k in range(PEER_TOPK):
                m = jnp.max(rest, axis=0, keepdims=True)
                first = jnp.min(jnp.where(rest == m, cid, N_CAND), axis=0, keepdims=True)
                hit = cid == first
                walk = jnp.where(hit, 1.0, walk)
                rest = jnp.where(hit, -jnp.inf, rest)
            chosen_scr[...] = walk

        chosen = chosen_scr[...]
        cnt_lo = jnp.zeros((8, LANE), F32)
        cnt_lo = jnp.where(rid8 == 0, jnp.sum(chosen[0:16], axis=0, keepdims=True), cnt_lo)
        for a in range(1, 8):
            cnt_lo = jnp.where(rid8 == a, jnp.sum(chosen[8 + 8 * a:16 + 8 * a], axis=0, keepdims=True), cnt_lo)
        cnt = jnp.concatenate([cnt_lo, chosen[N_CAND - 8:N_CAND]], axis=0)
        cnt_rows = [jnp.broadcast_to(cnt[a:a + 1], (SUBLANES, LANE)) for a in range(PEER_TOPK)]
        inv_z = 0.5 / zsum
        for jj in range(ntile // 2):
            ranks, e2s = [], []
            for j in (2 * jj, 2 * jj + 1):
                rows = slice(j * SUBLANES, (j + 1) * SUBLANES)
                d1 = s1[rows]
                d2 = s2[rows]
                brow = jnp.zeros((SUBLANES, LANE), F32)
                rank2 = jnp.zeros((SUBLANES, LANE), F32)
                for a in range(PEER_TOPK):
                    brow = jnp.where(d1 == t1[a], cnt_rows[a], brow)
                    rank2 = jnp.where(t2[a] > d2, float(a + 1), rank2)
                brow_ref[rows, ls] = brow
                crow_ref[rows, ls] = jnp.exp(d1 - t1[0]) * inv_z
                ranks.append(rank2)
                e2s.append(jnp.exp(d2 - t2[0]))
            rows16 = slice(jj * 2 * SUBLANES, (jj + 1) * 2 * SUBLANES)
            r2_ref[rows16, ls] = jnp.concatenate(ranks, axis=0).astype(BF16)
            e2_ref[rows16, ls] = jnp.concatenate(e2s, axis=0).astype(BF16)


def _peer_select(qry, sub_keys):
    nt = N_TOK // T_SEL
    kspec = pl.BlockSpec((None, PEER_NKEYS, T_SEL), lambda i, h: (h, 0, i))

    def kshape(dt):
        return jax.ShapeDtypeStruct((PEER_HEADS, PEER_NKEYS, N_TOK), dt)

    return pl.pallas_call(
        _peer_select_kernel,
        grid=(nt, PEER_HEADS),
        in_specs=[pl.BlockSpec((T_SEL, 2 * PEER_NKEYS), lambda i, h: (i, h)),
                  pl.BlockSpec((None, 2, PEER_NKEYS, PEER_NKEYS), lambda i, h: (h, 0, 0, 0))],
        out_specs=[kspec, kspec, kspec, kspec],
        out_shape=[kshape(BF16), kshape(BF16), kshape(F32), kshape(F32)],
        scratch_shapes=[pltpu.VMEM((N_CAND, LANE), F32)],
        compiler_params=_params("parallel", "arbitrary"),
        name="peer_select",
    )(qry, sub_keys)


E_PAIR = 2 * PEER_NKEYS
BF16_ROWS = 16


def _row_bf16(row):
    return jnp.broadcast_to(row, (BF16_ROWS, LANE)).astype(BF16)


def _gelu_x2(x):
    return x * (1.0 + lax.erf(x * (2.0 ** -0.5)))


def _peer_dense_kernel(h_ref, u_ref, v_ref, brow_ref, crow_ref, r2_ref, e2_ref, x_ref, mod_ref,
                       o_ref, acc_ref, a_scr, p_scr):
    c = pl.program_id(1)

    @pl.when(c == 0)
    def _():
        acc_ref[...] = jnp.zeros_like(acc_ref)

    hb = h_ref[...]
    zero = jnp.zeros((BF16_ROWS, LANE), BF16)
    npair = E_CHUNK // E_PAIR
    for j in range(npair + 1):
        slot = j % 2
        if j < npair:
            a_scr[slot] = _dot_nt(u_ref[j * E_PAIR:(j + 1) * E_PAIR, :].astype(BF16), hb)
        if j > 0:
            acc_ref[...] += _dot_tn(p_scr[1 - slot], v_ref[(j - 1) * E_PAIR:j * E_PAIR, :].astype(BF16))
        if j == npair:
            break
        for half in range(2):
            n1l = 2 * j + half
            for tc in range(T_PEER // LANE):
                ls = slice(tc * LANE, (tc + 1) * LANE)
                b16 = [_row_bf16(brow_ref[h, n1l:n1l + 1, ls]) for h in range(PEER_HEADS)]
                c16 = [_row_bf16(crow_ref[h, n1l:n1l + 1, ls]) for h in range(PEER_HEADS)]
                e0 = half * PEER_NKEYS
                act = _gelu_x2(a_scr[slot, e0:e0 + PEER_NKEYS, ls]).astype(BF16)
                for rg in range(PEER_NKEYS // BF16_ROWS):
                    rs = slice(rg * BF16_ROWS, (rg + 1) * BF16_ROWS)
                    g = None
                    for h in range(PEER_HEADS):
                        t = jnp.where(r2_ref[h, rs, ls] < b16[h], e2_ref[h, rs, ls], zero) * c16[h]
                        g = t if g is None else g + t
                    p_scr[slot, e0 + rg * BF16_ROWS:e0 + (rg + 1) * BF16_ROWS, ls] = g * act[rs]

    @pl.when(c == pl.num_programs(1) - 1)
    def _():
        o_ref[...] = x_ref[...] + mod_ref[5:6, :] * acc_ref[...]


def _peer_dense(h2, peer_u, peer_v, layer, r2, e2, brow, crow, x, mod_l):
    nt = N_TOK // T_PEER
    nc = PEER_N // E_CHUNK
    n1c = E_CHUNK // PEER_NKEYS
    rowspec = pl.BlockSpec((PEER_HEADS, n1c, T_PEER), lambda i, c: (0, c, i))
    fullspec = pl.BlockSpec((PEER_HEADS, PEER_NKEYS, T_PEER), lambda i, c: (0, 0, i))
    return pl.pallas_call(
        _peer_dense_kernel,
        grid=(nt, nc),
        in_specs=[pl.BlockSpec((T_PEER, D_MODEL), lambda i, c: (i, 0)),
                  pl.BlockSpec((None, E_CHUNK, D_MODEL), lambda i, c: (layer, c, 0)),
                  pl.BlockSpec((None, E_CHUNK, D_MODEL), lambda i, c: (layer, c, 0)),
                  rowspec, rowspec, fullspec, fullspec,
                  pl.BlockSpec((T_PEER, D_MODEL), lambda i, c: (i, 0)),
                  pl.BlockSpec((None, 6, D_MODEL), lambda i, c: (_mod_row(i, T_PEER), 0, 0))],
        out_specs=pl.BlockSpec((T_PEER, D_MODEL), lambda i, c: (i, 0)),
        out_shape=jax.ShapeDtypeStruct((N_TOK, D_MODEL), F32),
        scratch_shapes=[pltpu.VMEM((T_PEER, D_MODEL), F32),
                        pltpu.VMEM((2, E_PAIR, T_PEER), F32),
                        pltpu.VMEM((2, E_PAIR, T_PEER), BF16)],
        compiler_params=_params("parallel", "arbitrary"),
        name="peer_dense",
    )(h2, peer_u, peer_v, brow, crow, r2, e2, x, mod_l)


def kernel(x_prompt, x_sample, cache_nat_k, cache_nat_v, cache_diff_k, cache_diff_v, c, c_ctx, w_ada, b_ada, norm1_g, norm2_g, w_in, pool_w, pool_scale, nat_q_g, nat_k_g, nat_rel_bias, conv_w, conv_b, conv_ln_g, conv_ln_b, diff_q_g, diff_k_g, diff_lambda_p, diff_subln_g, w_branch, w_gate, b_gate, w_out, peer_w_query, peer_sub_keys, peer_u, peer_v):
    x = jnp.concatenate([x_prompt.reshape(N_CTX, D_MODEL), x_sample.reshape(N_LAT, D_MODEL)], axis=0)
    cvec = jnp.concatenate([c_ctx[None, :], c, jnp.zeros((8 - 1 - DEC_BATCH, D_MODEL), F32)], axis=0)
    mod = _modulation(cvec, w_ada, b_ada).reshape(DEPTH, 8, 6, D_MODEL)

    gid = jnp.arange(CB) // NAT_HD
    gmat = (gid[:, None] == gid[None, :]).astype(BF16)
    rope = _rope_tables(T_PRE)
    ck_n = cache_nat_k.reshape(DEC_BATCH, DEPTH, PAST_LEN, BRANCH_W)
    cv_n = cache_nat_v.reshape(DEC_BATCH, DEPTH, PAST_LEN, BRANCH_W)
    ck_d = cache_diff_k.reshape(DEC_BATCH, DEPTH, PAST_LEN, BRANCH_W)
    cv_d = cache_diff_v.reshape(DEC_BATCH, DEPTH, PAST_LEN, BRANCH_W)

    states = []
    for l in range(DEPTH):
        gg = jnp.stack([jnp.tile(nat_q_g[l], NAT_HEADS), jnp.tile(nat_k_g[l], NAT_HEADS),
                        jnp.tile(diff_q_g[l], 2 * DIFF_HEADS), jnp.tile(diff_k_g[l], 2 * DIFF_HEADS)])
        h1, zp, zn = _pre(x, mod[l], norm1_g[l].reshape(1, -1), w_in[l].astype(BF16),
                          gg.reshape(len(NORM_ZB), 1, CB), gmat, rope)

        y_pool, y_conv = _seq_mixers(zp, pool_w[l], pool_scale[l], conv_w[l], conv_b[l],
                                     conv_ln_g[l], conv_ln_b[l])
        y_nat_ctx = _nat_ctx(zp, zn)
        y_nat_lat = _nat_lat(zp, zn, ck_n, cv_n, *_nat_bias_table(nat_rel_bias[l]), l)
        y_diff_ctx = _diff_ctx(zp, zn, diff_lambda_p[l], diff_subln_g[l], l)
        y_diff_lat = _diff_lat(zp, zn, ck_d, cv_d, diff_lambda_p[l], diff_subln_g[l], l)

        x, h2, qry = _merge(h1, w_gate[l].astype(BF16), b_gate[l].reshape(1, -1), y_pool, y_nat_ctx, y_nat_lat,
                            y_conv, y_diff_ctx, y_diff_lat, x, mod[l], norm2_g[l].reshape(1, -1),
                            w_branch[l].astype(BF16), w_out[l].astype(BF16), peer_w_query[l].astype(BF16))
        r2, e2, brow, crow = _peer_select(qry, peer_sub_keys[l])
        x = _peer_dense(h2, peer_u, peer_v, l, r2, e2, brow, crow, x, mod[l])

        states.append([zn[:N_CTX, Q_NK * CB:(Q_NK + 1) * CB], zp[:N_CTX, P_NV * CB:(P_NV + 1) * CB],
                       zn[:N_CTX, Q_DK * CB:(Q_DK + 1) * CB], zp[:N_CTX, P_DV * CB:(P_DV + 1) * CB]])

    def stack(idx, shape):
        return jnp.stack([states[l][idx].reshape((BATCH, SEQ) + shape) for l in range(DEPTH)], axis=1)

    return (x[:N_CTX].reshape(BATCH, SEQ, D_MODEL),
            x[N_CTX:].reshape(DEC_BATCH, DEC_SEQ, D_MODEL),
            stack(0, (NAT_HEADS, NAT_HD)),
            stack(1, (NAT_HEADS, NAT_HD)),
            stack(2, (DIFF_HEADS, 2, DIFF_HD)),
            stack(3, (DIFF_HEADS, 2 * DIFF_HD)))
```

```python
import functools
import math

import jax
import jax.numpy as jnp
from jax import lax
from jax.experimental import pallas as pl
from jax.experimental.pallas import tpu as pltpu

F32 = jnp.float32
BF16 = jnp.bfloat16

D_MODEL = 1024
BATCH = 16
SEQ = 256
DEPTH = 2
DEC_BATCH = 4
DEC_SEQ = 2048
PAST_LEN = 256
GRID_W = 64
BRANCH_W = 512
POOL_WINDOWS = (2, 4, 8, 16)
POOL_GROUP = 128
NAT_HEADS = 8
NAT_HD = 64
NAT_WIN_R = 8
NAT_WIN_C = 16
CONV_W = 31
DIFF_HEADS = 4
DIFF_HD = 64
PEER_HEADS = 8
PEER_NKEYS = 128
PEER_N = PEER_NKEYS * PEER_NKEYS
PEER_TOPK = 16
ROPE_BASE = 10000.0
EPS = 1e-6
NEG_INF = -1e30
LOG2E = math.log2(math.e)

N_CTX = BATCH * SEQ
N_LAT = DEC_BATCH * DEC_SEQ
N_TOK = N_CTX + N_LAT
GRID_ROWS = DEC_SEQ // GRID_W

CB = 512
ZB_POOL, ZB_NQ, ZB_NK, ZB_NV, ZB_CA, ZB_CGATE, ZB_DQ, ZB_DK, ZB_DV = range(9)

V7X_VMEM_LIMIT = 52 * 1024 * 1024

T_PRE = 2048
T_SEQ = 256
HALO = 16
T_MERGE = 512
T_SEL = 1024
T_PEER = 512
E_CHUNK = 1024
LANE = 128


def _sigmoid(x):
    return 1.0 / (1.0 + jnp.exp(-x))


def _dot(a, b):
    return jnp.dot(a, b, preferred_element_type=F32)


def _dot_nt(a, b):
    return lax.dot_general(a, b, (((1,), (1,)), ((), ())), preferred_element_type=F32)


def _dot_tn(a, b):
    return lax.dot_general(a, b, (((0,), (0,)), ((), ())), preferred_element_type=F32)


def _split_bf16(a):
    hi = a.astype(BF16)
    lo = (a - hi.astype(F32)).astype(BF16)
    return hi, lo


def _params(*sem):
    return pltpu.CompilerParams(dimension_semantics=sem, vmem_limit_bytes=V7X_VMEM_LIMIT)


def _mod_row(i, tile):
    nct = N_CTX // tile
    per = DEC_SEQ // tile
    return jnp.where(i < nct, 0, 1 + (i - nct) // per)


def _mod_kernel(c_ref, w_ref, b_ref, o_ref):
    c = c_ref[...]
    a = c * _sigmoid(c)
    w = w_ref[0]
    a_hi, a_lo = _split_bf16(a)
    w_hi, w_lo = _split_bf16(w)
    o_ref[0] = _dot(a_hi, w_hi) + _dot(a_lo, w_hi) + _dot(a_hi, w_lo) + b_ref[0]


def _modulation(cvec, w_ada, b_ada):
    tn = 1024
    return pl.pallas_call(
        _mod_kernel,
        grid=(DEPTH, 6 * D_MODEL // tn),
        in_specs=[
            pl.BlockSpec((8, D_MODEL), lambda l, j: (0, 0)),
            pl.BlockSpec((1, D_MODEL, tn), lambda l, j: (l, 0, j)),
            pl.BlockSpec((1, 1, tn), lambda l, j: (l, 0, j)),
        ],
        out_specs=pl.BlockSpec((1, 8, tn), lambda l, j: (l, 0, j)),
        out_shape=jax.ShapeDtypeStruct((DEPTH, 8, 6 * D_MODEL), F32),
        compiler_params=_params("parallel", "parallel"),
        name="modulation",
    )(cvec, w_ada, b_ada.reshape(DEPTH, 1, 6 * D_MODEL))


def _tile4(t):
    return jnp.concatenate([t, t, t, t], axis=1)


def _hnorm_kernel(x_ref, mod_ref, g_ref, h_ref):
    x = x_ref[...]
    ms = jnp.mean(x * x, axis=-1, keepdims=True)
    y = x * lax.rsqrt(ms + EPS) * g_ref[...]
    h_ref[...] = (y * (1.0 + mod_ref[1:2, :]) + mod_ref[0:1, :]).astype(h_ref.dtype)


def _proj_plain_kernel(h_ref, w_ref, o_ref):
    o_ref[...] = _dot(h_ref[...], w_ref[...])


def _proj_norm_kernel(h_ref, w_ref, gg_ref, gmat_ref, rc_ref, rp_ref, rm_ref, o_ref):
    acc = _dot(h_ref[...], w_ref[...])
    hi, lo = _split_bf16(acc * acc)
    ss = _dot(hi, gmat_ref[...]) + _dot(lo, gmat_ref[...])
    y = acc * lax.rsqrt(ss * (1.0 / NAT_HD) + EPS) * gg_ref[0]
    o_ref[...] = (y * _tile4(rc_ref[...])
                  + pltpu.roll(y, 16, 1) * _tile4(rp_ref[...])
                  + pltpu.roll(y, CB - 16, 1) * _tile4(rm_ref[...]))


def _rope_tables(tile):
    quarter = DIFF_HD // 4
    t = jnp.arange(DEC_SEQ)
    lane = jnp.arange(LANE)
    d = lane % DIFF_HD
    freqs = ROPE_BASE ** (-jnp.arange(quarter, dtype=F32) / quarter)
    pos = jnp.where(d[None, :] < DIFF_HD // 2, (t // GRID_W)[:, None], (t % GRID_W)[:, None]).astype(F32)
    ang = pos * freqs[d % quarter][None, :]
    cos = jnp.cos(ang)
    sin = jnp.sin(ang)
    second = (d % (2 * quarter)) >= quarter
    s_plus = jnp.where(second[None, :], sin, 0.0)
    s_minus = jnp.where(second[None, :], 0.0, -sin)
    ones = jnp.ones((tile, LANE), F32)
    zeros = jnp.zeros((tile, LANE), F32)
    return (jnp.concatenate([cos, ones], 0), jnp.concatenate([s_plus, zeros], 0),
            jnp.concatenate([s_minus, zeros], 0))


PLAIN_ZB = (ZB_POOL, ZB_NV, ZB_CA, ZB_CGATE, ZB_DV)
NORM_ZB = (ZB_NQ, ZB_NK, ZB_DQ, ZB_DK)
P_POOL, P_NV, P_CA, P_CGATE, P_DV = range(5)
Q_NQ, Q_NK, Q_DQ, Q_DK = range(4)


def _pick(j, values):
    out = values[0]
    for n, v in enumerate(values[1:], start=1):
        out = jnp.where(j == n, v, out)
    return out


def _pre(x, mod_l, norm_g, w_in, gg, gmat, rope):
    nct = N_CTX // T_PRE
    per = DEC_SEQ // T_PRE
    ident_blk = DEC_SEQ // T_PRE
    nt = N_TOK // T_PRE
    h = pl.pallas_call(
        _hnorm_kernel,
        grid=(nt,),
        in_specs=[pl.BlockSpec((T_PRE, D_MODEL), lambda i: (i, 0)),
                  pl.BlockSpec((None, 6, D_MODEL), lambda i: (_mod_row(i, T_PRE), 0, 0)),
                  pl.BlockSpec((1, D_MODEL), lambda i: (0, 0))],
        out_specs=pl.BlockSpec((T_PRE, D_MODEL), lambda i: (i, 0)),
        out_shape=jax.ShapeDtypeStruct((N_TOK, D_MODEL), BF16),
        compiler_params=_params("parallel"),
        name="hnorm",
    )(x, mod_l, norm_g)

    hspec = pl.BlockSpec((T_PRE, D_MODEL), lambda i, j: (i, 0))
    ospec = pl.BlockSpec((T_PRE, CB), lambda i, j: (i, j))

    zp = pl.pallas_call(
        _proj_plain_kernel,
        grid=(nt, len(PLAIN_ZB)),
        in_specs=[hspec, pl.BlockSpec((D_MODEL, CB), lambda i, j: (0, _pick(j, PLAIN_ZB)))],
        out_specs=ospec,
        out_shape=jax.ShapeDtypeStruct((N_TOK, len(PLAIN_ZB) * CB), F32),
        compiler_params=_params("parallel", "arbitrary"),
        name="proj_plain",
    )(h, w_in)

    def rope_idx(i, j):
        use = ((j == Q_DQ) | (j == Q_DK)) & (i >= nct)
        return jnp.where(use, (i - nct) % per, ident_blk), 0

    rope_spec = pl.BlockSpec((T_PRE, LANE), rope_idx)
    zn = pl.pallas_call(
        _proj_norm_kernel,
        grid=(nt, len(NORM_ZB)),
        in_specs=[hspec, pl.BlockSpec((D_MODEL, CB), lambda i, j: (0, _pick(j, NORM_ZB))),
                  pl.BlockSpec((1, 1, CB), lambda i, j: (j, 0, 0)),
                  pl.BlockSpec((CB, CB), lambda i, j: (0, 0)),
                  rope_spec, rope_spec, rope_spec],
        out_specs=ospec,
        out_shape=jax.ShapeDtypeStruct((N_TOK, len(NORM_ZB) * CB), F32),
        compiler_params=_params("parallel", "arbitrary"),
        name="proj_norm",
    )(h, w_in, gg, gmat, *rope)

    return h, zp, zn


def _seq_kernel(zp_ref, zp_prev, zp_next, za_ref, za_prev, za_next, zb_ref, zb_prev, zb_next,
                pw_ref, ps_ref, cw_ref, cb_ref, lg_ref, lb_ref,
                yp_ref, yc_ref, pbuf, ubuf, ybuf, shbuf):
    i = pl.program_id(0)
    nct = N_CTX // T_SEQ
    per = DEC_SEQ // T_SEQ
    is_lat = i >= nct
    kk = (i - nct) % per
    has_prev = is_lat & (kk != 0)
    has_next = is_lat & (kk != per - 1)
    pos0 = jnp.where(is_lat, kk * T_SEQ, 0)
    seq_len = jnp.where(is_lat, DEC_SEQ, SEQ)
    fp = jnp.where(has_prev, 1.0, 0.0).astype(F32)
    fn = jnp.where(has_next, 1.0, 0.0).astype(F32)

    pbuf[0:HALO, :] = zp_prev[...] * fp
    pbuf[HALO:HALO + T_SEQ, :] = zp_ref[...]
    pbuf[HALO + T_SEQ:, :] = zp_next[...] * fn
    ubuf[0:HALO, :] = za_prev[...] * _sigmoid(zb_prev[...]) * fp
    ubuf[HALO:HALO + T_SEQ, :] = za_ref[...] * _sigmoid(zb_ref[...])
    ubuf[HALO + T_SEQ:, :] = za_next[...] * _sigmoid(zb_next[...]) * fn

    pos = pos0 + lax.broadcasted_iota(jnp.int32, (T_SEQ, LANE), 0)
    for gi, w in enumerate(POOL_WINDOWS):
        ls = slice(gi * LANE, (gi + 1) * LANE)
        acc = None
        for s in range(-(w // 2), w - w // 2):
            v = pbuf[HALO + s:HALO + s + T_SEQ, ls]
            acc = v if acc is None else acc + v
        lo = jnp.maximum(pos - w // 2, 0)
        hi = jnp.minimum(pos + (w - w // 2), seq_len)
        mean = acc / (hi - lo).astype(F32)
        d = mean - pbuf[HALO:HALO + T_SEQ, ls]
        y = _dot(d.astype(BF16), pw_ref[gi].astype(BF16)) * ps_ref[:, ls]
        yp_ref[:, ls] = y.astype(yp_ref.dtype)

    base = HALO - CONV_W // 2
    span = shbuf.shape[0]
    for b in range(8):
        shbuf[...] = ubuf[base + b:base + b + span, :]
        for c in range(BRANCH_W // LANE):
            ls = slice(c * LANE, (c + 1) * LANE)
            acc = cb_ref[:, ls] if b == 0 else ybuf[:, ls]
            for k in range(b, CONV_W, 8):
                acc = acc + shbuf[k - b:k - b + T_SEQ, ls] * cw_ref[k:k + 1, ls]
            ybuf[:, ls] = acc
    y = ybuf[...]
    mu = jnp.mean(y, axis=-1, keepdims=True)
    yc = y - mu
    var = jnp.mean(yc * yc, axis=-1, keepdims=True)
    yn = yc * lax.rsqrt(var + EPS) * lg_ref[...] + lb_ref[...]
    yc_ref[...] = (yn * _sigmoid(yn)).astype(yc_ref.dtype)


def _seq_mixers(zp, pool_w, pool_scale, conv_w, conv_b, ln_g, ln_b):
    nt = N_TOK // T_SEQ
    hb = T_SEQ // HALO
    n_hblk = N_TOK // HALO

    def specs(cb):
        return [
            pl.BlockSpec((T_SEQ, CB), lambda i: (i, cb)),
            pl.BlockSpec((HALO, CB), lambda i: (jnp.maximum(i * hb - 1, 0), cb)),
            pl.BlockSpec((HALO, CB), lambda i: (jnp.minimum((i + 1) * hb, n_hblk - 1), cb)),
        ]

    def full(shape):
        return pl.BlockSpec(shape, lambda i: (0,) * len(shape))

    cw = jnp.concatenate([conv_w, jnp.zeros((1, BRANCH_W), F32)], axis=0)
    out_spec = pl.BlockSpec((T_SEQ, BRANCH_W), lambda i: (i, 0))
    return pl.pallas_call(
        _seq_kernel,
        grid=(nt,),
        in_specs=(specs(P_POOL) + specs(P_CA) + specs(P_CGATE)
                  + [full((4, POOL_GROUP, POOL_GROUP)), full((1, BRANCH_W)), full((CONV_W + 1, BRANCH_W)),
                     full((1, BRANCH_W)), full((1, BRANCH_W)), full((1, BRANCH_W))]),
        out_specs=[out_spec, out_spec],
        out_shape=[jax.ShapeDtypeStruct((N_TOK, BRANCH_W), BF16)] * 2,
        scratch_shapes=[pltpu.VMEM((T_SEQ + 2 * HALO, BRANCH_W), F32),
                        pltpu.VMEM((T_SEQ + 2 * HALO, BRANCH_W), F32),
                        pltpu.VMEM((T_SEQ, BRANCH_W), F32),
                        pltpu.VMEM((T_SEQ + 8 * ((CONV_W - 1) // 8), BRANCH_W), F32)],
        compiler_params=_params("parallel"),
        name="pool_conv",
    )(zp, zp, zp, zp, zp, zp, zp, zp, zp, pool_w, pool_scale.reshape(1, -1), cw,
      conv_b.reshape(1, -1), ln_g.reshape(1, -1), ln_b.reshape(1, -1))


def _nat_ctx_kernel(q_ref, k_ref, v_ref, o_ref):
    for h in range(NAT_HEADS):
        sl = slice(h * NAT_HD, (h + 1) * NAT_HD)
        q = (q_ref[:, sl] * (NAT_HD ** -0.5 * LOG2E)).astype(BF16)
        s = _dot_nt(q, k_ref[:, sl].astype(BF16))
        m = jnp.max(s, axis=-1, keepdims=True)
        e = jnp.exp2(s - m)
        den = jnp.sum(e, axis=-1, keepdims=True)
        o = _dot(e.astype(BF16), v_ref[:, sl].astype(BF16)) / den
        o_ref[:, sl] = o.astype(o_ref.dtype)


def _nat_ctx(zp, zn):
    def spec(cb):
        return pl.BlockSpec((SEQ, CB), lambda b: (b, cb))

    return pl.pallas_call(
        _nat_ctx_kernel,
        grid=(BATCH,),
        in_specs=[spec(Q_NQ), spec(Q_NK), spec(P_NV)],
        out_specs=pl.BlockSpec((SEQ, BRANCH_W), lambda b: (b, 0)),
        out_shape=jax.ShapeDtypeStruct((N_CTX, BRANCH_W), BF16),
        compiler_params=_params("parallel"),
        name="nat_ctx",
    )(zn, zn, zp)


NAT_QROWS = 4
NAT_KROWS = 12


def _nat_row0(r):
    return jnp.clip(r - NAT_WIN_R // 2, 0, GRID_ROWS - NAT_WIN_R)


def _nat_lat_kernel(q_ref, k_ref, v_ref, ck_ref, cv_ref, bias_ref, o_ref):
    g = pl.program_id(1)
    w0 = jnp.minimum(_nat_row0(g * NAT_QROWS), GRID_ROWS - NAT_KROWS)
    start = pl.multiple_of(w0 * GRID_W, GRID_W)
    nwin = NAT_KROWS * GRID_W
    for h in range(NAT_HEADS):
        sl = slice(h * NAT_HD, (h + 1) * NAT_HD)
        q = (q_ref[:, sl] * (NAT_HD ** -0.5 * LOG2E)).astype(BF16)
        kw = k_ref[pl.ds(start, nwin), sl].astype(BF16)
        vw = v_ref[pl.ds(start, nwin), sl].astype(BF16)
        s_loc = _dot_nt(q, kw) + bias_ref[h]
        s_ctx = _dot_nt(q, ck_ref[:, sl].astype(BF16))
        m = jnp.maximum(jnp.max(s_loc, axis=-1, keepdims=True), jnp.max(s_ctx, axis=-1, keepdims=True))
        e_loc = jnp.exp2(s_loc - m)
        e_ctx = jnp.exp2(s_ctx - m)
        den = jnp.sum(e_loc, axis=-1, keepdims=True) + jnp.sum(e_ctx, axis=-1, keepdims=True)
        o = (_dot(e_loc.astype(BF16), vw) + _dot(e_ctx.astype(BF16), cv_ref[:, sl].astype(BF16))) / den
        o_ref[:, sl] = o.astype(o_ref.dtype)


def _nat_bias_kernel(rb_ref, oh_ref, o_ref):
    x = rb_ref[...]
    hi = x.astype(BF16)
    r1 = x - hi.astype(F32)
    mid = r1.astype(BF16)
    lo = (r1 - mid.astype(F32)).astype(BF16)
    oh = oh_ref[...]
    o_ref[...] = _dot(hi, oh) + _dot(mid, oh) + _dot(lo, oh)


def _nat_bias_table(rel_bias):
    ndr = 2 * NAT_WIN_R - 1
    ndc = 2 * NAT_WIN_C - 1
    q = jnp.arange(GRID_W)
    kc = jnp.arange(GRID_W)
    dcol = jnp.clip(kc[None, :] - q[:, None] + NAT_WIN_C - 1, 0, ndc - 1)
    wstart = jnp.clip(q - NAT_WIN_C // 2, 0, GRID_W - NAT_WIN_C)
    valid = (kc[None, :] >= wstart[:, None]) & (kc[None, :] < wstart[:, None] + NAT_WIN_C)
    d = jnp.arange(LANE)
    onehot = jnp.where(d[:, None, None] == ndc, jnp.logical_not(valid)[None],
                       (d[:, None, None] == dcol[None]) & valid[None])
    onehot = onehot.reshape(LANE, GRID_W * GRID_W).astype(BF16)
    nrow = NAT_HEADS * ndr
    rb = jnp.concatenate([rel_bias.reshape(nrow, ndc).astype(F32) * LOG2E, jnp.full((nrow, 1), NEG_INF, F32),
                          jnp.zeros((nrow, LANE - ndc - 1), F32)], axis=1)
    rb = jnp.concatenate([rb, jnp.zeros((LANE - nrow, LANE), F32)], axis=0)
    tcol = pl.pallas_call(
        _nat_bias_kernel,
        out_shape=jax.ShapeDtypeStruct((LANE, GRID_W * GRID_W), F32),
        compiler_params=pltpu.CompilerParams(vmem_limit_bytes=V7X_VMEM_LIMIT),
        name="nat_bias",
    )(rb, onehot)
    tcol = tcol[:nrow].reshape(NAT_HEADS, ndr, GRID_W, GRID_W)
    neg = jnp.full((NAT_HEADS, GRID_W, GRID_W), NEG_INF, F32)
    tables, variant_of_group, seen = [], [], {}
    for g in range(GRID_ROWS // NAT_QROWS):
        r_first = g * NAT_QROWS
        w0 = min(max(r_first - NAT_WIN_R // 2, 0), GRID_ROWS - NAT_WIN_R, GRID_ROWS - NAT_KROWS)
        rows = [(r_first + j - w0, min(max(r_first + j - NAT_WIN_R // 2, 0), GRID_ROWS - NAT_WIN_R) - w0)
                for j in range(NAT_QROWS)]
        key = tuple(rows)
        if key not in seen:
            seen[key] = len(tables)
            blocks = []
            for rq, rw in rows:
                blocks.append(jnp.concatenate(
                    [tcol[:, i - rq + NAT_WIN_R - 1] if rw <= i < rw + NAT_WIN_R else neg
                     for i in range(NAT_KROWS)], axis=-1))
            tables.append(jnp.concatenate(blocks, axis=1))
        variant_of_group.append(seen[key])
    return jnp.stack(tables, axis=0), tuple(variant_of_group)


def _nat_lat(zp, zn, cache_k, cache_v, bias_tab, variant_of_group, layer):
    tq = NAT_QROWS * GRID_W
    ngroups = GRID_ROWS // NAT_QROWS
    q_blk0 = N_CTX // tq
    kv_blk0 = N_CTX // DEC_SEQ
    cache_spec = pl.BlockSpec((None, None, PAST_LEN, BRANCH_W), lambda b, g: (b, layer, 0, 0))
    return pl.pallas_call(
        _nat_lat_kernel,
        grid=(DEC_BATCH, ngroups),
        in_specs=[
            pl.BlockSpec((tq, CB), lambda b, g: (q_blk0 + b * ngroups + g, Q_NQ)),
            pl.BlockSpec((DEC_SEQ, CB), lambda b, g: (kv_blk0 + b, Q_NK)),
            pl.BlockSpec((DEC_SEQ, CB), lambda b, g: (kv_blk0 + b, P_NV)),
            cache_spec, cache_spec,
            pl.BlockSpec((None, NAT_HEADS, tq, NAT_KROWS * GRID_W),
                         lambda b, g: (_pick(g, variant_of_group), 0, 0, 0)),
        ],
        out_specs=pl.BlockSpec((tq, BRANCH_W), lambda b, g: (b * ngroups + g, 0)),
        out_shape=jax.ShapeDtypeStruct((N_LAT, BRANCH_W), BF16),
        compiler_params=_params("parallel", "arbitrary"),
        name="nat_lat",
    )(zn, zn, zp, cache_k, cache_v, bias_tab)


def _diff_kernel(has_cache, lam_init, *refs):
    if has_cache:
        q_ref, k_ref, v_ref, ck_ref, cv_ref, lamp_ref, g_ref, o_ref = refs
    else:
        q_ref, k_ref, v_ref, lamp_ref, g_ref, o_ref = refs
    lp = lamp_ref[...]
    lam = (jnp.exp(jnp.sum(lp[0:1] * lp[1:2], axis=-1, keepdims=True))
           - jnp.exp(jnp.sum(lp[2:3] * lp[3:4], axis=-1, keepdims=True)) + lam_init)
    hv = 2 * DIFF_HD
    for h in range(DIFF_HEADS):
        vs = slice(h * hv, (h + 1) * hv)
        vb = v_ref[:, vs].astype(BF16)
        if has_cache:
            cvb = cv_ref[:, vs].astype(BF16)
        parts = []
        for i in range(2):
            sl = slice(h * hv + i * DIFF_HD, h * hv + (i + 1) * DIFF_HD)
            q = (q_ref[:, sl] * (DIFF_HD ** -0.5 * LOG2E)).astype(BF16)
            s = _dot_nt(q, k_ref[:, sl].astype(BF16))
            m = jnp.max(s, axis=-1, keepdims=True)
            if has_cache:
                sc = _dot_nt(q, ck_ref[:, sl].astype(BF16))
                m = jnp.maximum(m, jnp.max(sc, axis=-1, keepdims=True))
            e = jnp.exp2(s - m)
            den = jnp.sum(e, axis=-1, keepdims=True)
            o = _dot(e.astype(BF16), vb)
            if has_cache:
                ec = jnp.exp2(sc - m)
                den = den + jnp.sum(ec, axis=-1, keepdims=True)
                o = o + _dot(ec.astype(BF16), cvb)
            parts.append((o, 1.0 / den))
        (o1, r1), (o2, r2) = parts
        o = o1 * r1 - o2 * (lam * r2)
        ms = jnp.mean(o * o, axis=-1, keepdims=True)
        y = o * lax.rsqrt(ms + EPS) * g_ref[...] * (1.0 - lam_init)
        o_ref[:, vs] = y.astype(o_ref.dtype)


def _lam_init(layer):
    return 0.8 - 0.6 * math.exp(-0.3 * layer)


def _diff_ctx(zp, zn, lam_p, subln_g, layer):
    def spec(cb):
        return pl.BlockSpec((SEQ, CB), lambda b: (b, cb))

    return pl.pallas_call(
        functools.partial(_diff_kernel, False, _lam_init(layer)),
        grid=(BATCH,),
        in_specs=[spec(Q_DQ), spec(Q_DK), spec(P_DV),
                  pl.BlockSpec((4, DIFF_HD), lambda b: (0, 0)),
                  pl.BlockSpec((1, 2 * DIFF_HD), lambda b: (0, 0))],
        out_specs=pl.BlockSpec((SEQ, BRANCH_W), lambda b: (b, 0)),
        out_shape=jax.ShapeDtypeStruct((N_CTX, BRANCH_W), BF16),
        compiler_params=_params("parallel"),
        name="diff_ctx",
    )(zn, zn, zp, lam_p, subln_g.reshape(1, -1))


T_DQ = 512


def _diff_lat(zp, zn, cache_k, cache_v, lam_p, subln_g, layer):
    nq = DEC_SEQ // T_DQ
    q_blk0 = N_CTX // T_DQ
    kv_blk0 = N_CTX // DEC_SEQ
    cache_spec = pl.BlockSpec((None, None, PAST_LEN, BRANCH_W), lambda b, t: (b, layer, 0, 0))
    return pl.pallas_call(
        functools.partial(_diff_kernel, True, _lam_init(layer)),
        grid=(DEC_BATCH, nq),
        in_specs=[
            pl.BlockSpec((T_DQ, CB), lambda b, t: (q_blk0 + b * nq + t, Q_DQ)),
            pl.BlockSpec((DEC_SEQ, CB), lambda b, t: (kv_blk0 + b, Q_DK)),
            pl.BlockSpec((DEC_SEQ, CB), lambda b, t: (kv_blk0 + b, P_DV)),
            cache_spec, cache_spec,
            pl.BlockSpec((4, DIFF_HD), lambda b, t: (0, 0)),
            pl.BlockSpec((1, 2 * DIFF_HD), lambda b, t: (0, 0)),
        ],
        out_specs=pl.BlockSpec((T_DQ, BRANCH_W), lambda b, t: (b * nq + t, 0)),
        out_shape=jax.ShapeDtypeStruct((N_LAT, BRANCH_W), BF16),
        compiler_params=_params("parallel", "arbitrary"),
        name="diff_lat",
    )(zn, zn, zp, cache_k, cache_v, lam_p, subln_g.reshape(1, -1))


def _merge_kernel(h_ref, wg_ref, bg_ref, yp_ref, ync_ref, ynl_ref, yc_ref, ydc_ref, ydl_ref, x_ref, mod_ref,
                  g2_ref, wb_ref, wo_ref, wq_ref, xo_ref, h2_ref, q_ref, merged_scr):
    br = pl.program_id(1)
    is_ctx = pl.program_id(0) < N_CTX // T_MERGE
    y_nat = jnp.where(is_ctx, ync_ref[...], ynl_ref[...])
    y_diff = jnp.where(is_ctx, ydc_ref[...], ydl_ref[...])
    y = jnp.where(br == 0, yp_ref[...], jnp.where(br == 1, y_nat, jnp.where(br == 2, yc_ref[...], y_diff)))
    gate = _sigmoid(_dot(h_ref[...], wg_ref[...]) + bg_ref[...])
    t = gate * _dot(y, wb_ref[...])

    @pl.when(br == 0)
    def _():
        merged_scr[...] = t

    @pl.when(br > 0)
    def _():
        merged_scr[...] += t

    @pl.when(br == pl.num_programs(1) - 1)
    def _():
        out = _dot(merged_scr[...].astype(BF16), wo_ref[...])
        x = x_ref[...] + mod_ref[2:3, :] * out
        xo_ref[...] = x
        ms = jnp.mean(x * x, axis=-1, keepdims=True)
        h = x * lax.rsqrt(ms + EPS) * g2_ref[...] * (1.0 + mod_ref[4:5, :]) + mod_ref[3:4, :]
        hb = h.astype(BF16)
        h2_ref[...] = hb
        q_ref[...] = _dot(hb, wq_ref[...])


def _merge(h, wg, bg, y_pool, y_nat_ctx, y_nat_lat, y_conv, y_diff_ctx, y_diff_lat, x, mod_l, norm2_g, wb, wo, wq):
    nct = N_CTX // T_MERGE
    nbr = wb.shape[0]
    yspec = pl.BlockSpec((T_MERGE, BRANCH_W), lambda i, br: (i, 0))
    cspec = pl.BlockSpec((T_MERGE, BRANCH_W), lambda i, br: (jnp.minimum(i, nct - 1), 0))
    lspec = pl.BlockSpec((T_MERGE, BRANCH_W), lambda i, br: (jnp.maximum(i - nct, 0), 0))
    tspec = pl.BlockSpec((T_MERGE, D_MODEL), lambda i, br: (i, 0))
    qcols = wq.shape[1]
    return pl.pallas_call(
        _merge_kernel,
        grid=(N_TOK // T_MERGE, nbr),
        in_specs=[tspec,
                  pl.BlockSpec((D_MODEL, D_MODEL), lambda i, br: (0, br)),
                  pl.BlockSpec((1, D_MODEL), lambda i, br: (0, br)),
                  yspec, cspec, lspec, yspec, cspec, lspec,
                  tspec,
                  pl.BlockSpec((None, 6, D_MODEL), lambda i, br: (_mod_row(i, T_MERGE), 0, 0)),
                  pl.BlockSpec((1, D_MODEL), lambda i, br: (0, 0)),
                  pl.BlockSpec((None, BRANCH_W, D_MODEL), lambda i, br: (br, 0, 0)),
                  pl.BlockSpec((D_MODEL, D_MODEL), lambda i, br: (0, 0)),
                  pl.BlockSpec((D_MODEL, qcols), lambda i, br: (0, 0))],
        out_specs=[tspec, tspec, pl.BlockSpec((T_MERGE, qcols), lambda i, br: (i, 0))],
        out_shape=[jax.ShapeDtypeStruct((N_TOK, D_MODEL), F32),
                   jax.ShapeDtypeStruct((N_TOK, D_MODEL), BF16),
                   jax.ShapeDtypeStruct((N_TOK, qcols), F32)],
        scratch_shapes=[pltpu.VMEM((T_MERGE, D_MODEL), F32)],
        compiler_params=_params("parallel", "arbitrary"),
        name="merge",
    )(h, wg, bg, y_pool, y_nat_ctx, y_nat_lat, y_conv, y_diff_ctx, y_diff_lat, x, mod_l, norm2_g, wb, wo, wq)


SUBLANES = 8


def _merge_exchange_pairs(n):
    pairs = []
    t = (n - 1).bit_length()
    p = 1 << (t - 1)
    while p > 0:
        q, r, d = 1 << (t - 1), 0, p
        while d > 0:
            pairs += [(i, i + d) for i in range(n - d) if (i & p) == r]
            d, q, r = q - p, q >> 1, p
        p >>= 1
    return pairs


_SORT16 = _merge_exchange_pairs(PEER_TOPK)


def _cmpx(tiles, i, j):
    a, b = tiles[i], tiles[j]
    if b is None:
        return
    if a is None:
        tiles[i], tiles[j] = b, None
        return
    tiles[i], tiles[j] = jnp.maximum(a, b), jnp.minimum(a, b)


def _top16_sorted(s):
    n = PEER_TOPK
    tiles = [s[j * SUBLANES:(j + 1) * SUBLANES, :] for j in range(s.shape[0] // SUBLANES)]
    tiles += [None] * (n - len(tiles))
    for i, j in _SORT16:
        _cmpx(tiles, i, j)
    for shift in (4, 2, 1):
        merged = []
        for i in range(n):
            a, b = tiles[i], tiles[n - 1 - i]
            b = None if b is None else pltpu.roll(b, shift, 0)
            merged.append(b if a is None else a if b is None else jnp.maximum(a, b))
        tiles = merged
        d = n // 2
        while d > 0:
            for i in range(n):
                if (i & d) == 0:
                    _cmpx(tiles, i, i + d)
            d //= 2
    return tiles


def _rows_to_sublanes(tiles):
    rid = lax.broadcasted_iota(jnp.int32, (SUBLANES, LANE), 0)
    halves = []
    for base in (0, SUBLANES):
        out = tiles[base]
        for k in range(1, SUBLANES):
            out = jnp.where(rid == k, tiles[base + k], out)
        halves.append(out)
    return jnp.concatenate(halves, axis=0)


N_CAND = 16 + 7 * 8 + 8


def _peer_select_kernel(q_ref, sk_ref, r2_ref, e2_ref, brow_ref, crow_ref, chosen_scr):
    q = q_ref[...].astype(BF16)
    half = PEER_NKEYS
    s1_all = _dot_nt(sk_ref[0].astype(BF16), q[:, :half])
    s2_all = _dot_nt(sk_ref[1].astype(BF16), q[:, half:])
    cid = lax.broadcasted_iota(jnp.int32, (N_CAND, LANE), 0)
    rid8 = lax.broadcasted_iota(jnp.int32, (8, LANE), 0)
    ntile = PEER_NKEYS // SUBLANES
    for c in range(T_SEL // LANE):
        ls = slice(c * LANE, (c + 1) * LANE)
        s1 = s1_all[:, ls]
        s2 = s2_all[:, ls]
        t1 = _top16_sorted(s1)
        t2 = _top16_sorted(s2)
        v1 = _rows_to_sublanes(t1)
        v2 = _rows_to_sublanes(t2)
        cand = jnp.concatenate([v1[0:1] + v2] + [v1[a:a + 1] + v2[0:8] for a in range(1, 8)]
                               + [v1[8:16] + v2[0:1]], axis=0)
        ctop = _top16_sorted(cand)
        zsum = jnp.zeros((1, LANE), F32)
        for k in range(PEER_TOPK):
            zsum = zsum + jnp.exp(ctop[k][0:1] - ctop[0][0:1])
        fast = jnp.where(cand >= ctop[PEER_TOPK - 1][0:1], 1.0, 0.0)
        chosen_scr[...] = fast
        n_fast = jnp.sum(fast, axis=0, keepdims=True)
        tied = jnp.sum(jnp.where(n_fast != float(PEER_TOPK), 1.0, 0.0)) > 0.0

        @pl.when(tied)
        def _():
            rest = cand
            walk = jnp.zeros((N_CAND, LANE), F32)
            for k in range(PEER_TOPK):
                m = jnp.max(rest, axis=0, keepdims=True)
                first = jnp.min(jnp.where(rest == m, cid, N_CAND), axis=0, keepdims=True)
                hit = cid == first
                walk = jnp.where(hit, 1.0, walk)
                rest = jnp.where(hit, -jnp.inf, rest)
            chosen_scr[...] = walk

        chosen = chosen_scr[...]
        cnt_lo = jnp.zeros((8, LANE), F32)
        cnt_lo = jnp.where(rid8 == 0, jnp.sum(chosen[0:16], axis=0, keepdims=True), cnt_lo)
        for a in range(1, 8):
            cnt_lo = jnp.where(rid8 == a, jnp.sum(chosen[8 + 8 * a:16 + 8 * a], axis=0, keepdims=True), cnt_lo)
        cnt = jnp.concatenate([cnt_lo, chosen[N_CAND - 8:N_CAND]], axis=0)
        cnt_rows = [jnp.broadcast_to(cnt[a:a + 1], (SUBLANES, LANE)) for a in range(PEER_TOPK)]
        inv_z = 0.5 / zsum
        for jj in range(ntile // 2):
            ranks, e2s = [], []
            for j in (2 * jj, 2 * jj + 1):
                rows = slice(j * SUBLANES, (j + 1) * SUBLANES)
                d1 = s1[rows]
                d2 = s2[rows]
                brow = jnp.zeros((SUBLANES, LANE), F32)
                rank2 = jnp.zeros((SUBLANES, LANE), F32)
                for a in range(PEER_TOPK):
                    brow = jnp.where(d1 == t1[a], cnt_rows[a], brow)
                    rank2 = jnp.where(t2[a] > d2, float(a + 1), rank2)
                brow_ref[rows, ls] = brow
                crow_ref[rows, ls] = jnp.exp(d1 - t1[0]) * inv_z
                ranks.append(rank2)
                e2s.append(jnp.exp(d2 - t2[0]))
            rows16 = slice(jj * 2 * SUBLANES, (jj + 1) * 2 * SUBLANES)
            r2_ref[rows16, ls] = jnp.concatenate(ranks, axis=0).astype(BF16)
            e2_ref[rows16, ls] = jnp.concatenate(e2s, axis=0).astype(BF16)


def _peer_select(qry, sub_keys):
    nt = N_TOK // T_SEL
    kspec = pl.BlockSpec((None, PEER_NKEYS, T_SEL), lambda i, h: (h, 0, i))

    def kshape(dt):
        return jax.ShapeDtypeStruct((PEER_HEADS, PEER_NKEYS, N_TOK), dt)

    return pl.pallas_call(
        _peer_select_kernel,
        grid=(nt, PEER_HEADS),
        in_specs=[pl.BlockSpec((T_SEL, 2 * PEER_NKEYS), lambda i, h: (i, h)),
                  pl.BlockSpec((None, 2, PEER_NKEYS, PEER_NKEYS), lambda i, h: (h, 0, 0, 0))],
        out_specs=[kspec, kspec, kspec, kspec],
        out_shape=[kshape(BF16), kshape(BF16), kshape(F32), kshape(F32)],
        scratch_shapes=[pltpu.VMEM((N_CAND, LANE), F32)],
        compiler_params=_params("parallel", "arbitrary"),
        name="peer_select",
    )(qry, sub_keys)


BF16_ROWS = 16


def _row_bf16(row):
    return jnp.broadcast_to(row, (BF16_ROWS, LANE)).astype(BF16)


def _gelu_x2(x):
    return x * (1.0 + lax.erf(x * (2.0 ** -0.5)))


def _peer_dense_kernel(h_ref, u_ref, v_ref, brow_ref, crow_ref, r2_ref, e2_ref, x_ref, mod_ref,
                       o_ref, acc_ref, p_scr):
    g = pl.program_id(0)
    nc = PEER_N // E_CHUNK
    prev_chunk = jnp.maximum(g - 1, 0) % nc
    slot = g % 2

    @pl.when(g == 0)
    def _():
        p_scr[...] = jnp.zeros_like(p_scr)

    @pl.when(prev_chunk == 0)
    def _():
        acc_ref[...] = jnp.zeros_like(acc_ref)

    a_val = _dot_nt(u_ref[...].astype(BF16), h_ref[...])
    acc_ref[...] += _dot_tn(p_scr[1 - slot], v_ref[...].astype(BF16))
    zero = jnp.zeros((BF16_ROWS, LANE), BF16)
    for n1l in range(E_CHUNK // PEER_NKEYS):
        e0 = n1l * PEER_NKEYS
        for tc in range(T_PEER // LANE):
            ls = slice(tc * LANE, (tc + 1) * LANE)
            b16 = [_row_bf16(brow_ref[h, n1l:n1l + 1, ls]) for h in range(PEER_HEADS)]
            c16 = [_row_bf16(crow_ref[h, n1l:n1l + 1, ls]) for h in range(PEER_HEADS)]
            act = _gelu_x2(a_val[e0:e0 + PEER_NKEYS, ls]).astype(BF16)
            for rg in range(PEER_NKEYS // BF16_ROWS):
                rs = slice(rg * BF16_ROWS, (rg + 1) * BF16_ROWS)
                w = None
                for h in range(PEER_HEADS):
                    t = jnp.where(r2_ref[h, rs, ls] < b16[h], e2_ref[h, rs, ls], zero) * c16[h]
                    w = t if w is None else w + t
                p_scr[slot, e0 + rg * BF16_ROWS:e0 + (rg + 1) * BF16_ROWS, ls] = w * act[rs]

    @pl.when((g > 0) & (prev_chunk == nc - 1))
    def _():
        o_ref[...] = x_ref[...] + mod_ref[5:6, :] * acc_ref[...]


def _peer_dense(h2, peer_u, peer_v, layer, r2, e2, brow, crow, x, mod_l):
    nt = N_TOK // T_PEER
    nc = PEER_N // E_CHUNK
    n1c = E_CHUNK // PEER_NKEYS
    nsteps = nt * nc

    def cur(g):
        s = jnp.minimum(g, nsteps - 1)
        return s // nc, s % nc

    def prev(g):
        s = jnp.maximum(g - 1, 0)
        return s // nc, s % nc

    return pl.pallas_call(
        _peer_dense_kernel,
        grid=(nsteps + 1,),
        in_specs=[pl.BlockSpec((T_PEER, D_MODEL), lambda g: (cur(g)[0], 0)),
                  pl.BlockSpec((None, E_CHUNK, D_MODEL), lambda g: (layer, cur(g)[1], 0)),
                  pl.BlockSpec((None, E_CHUNK, D_MODEL), lambda g: (layer, prev(g)[1], 0)),
                  pl.BlockSpec((PEER_HEADS, n1c, T_PEER), lambda g: (0, cur(g)[1], cur(g)[0])),
                  pl.BlockSpec((PEER_HEADS, n1c, T_PEER), lambda g: (0, cur(g)[1], cur(g)[0])),
                  pl.BlockSpec((PEER_HEADS, PEER_NKEYS, T_PEER), lambda g: (0, 0, cur(g)[0])),
                  pl.BlockSpec((PEER_HEADS, PEER_NKEYS, T_PEER), lambda g: (0, 0, cur(g)[0])),
                  pl.BlockSpec((T_PEER, D_MODEL), lambda g: (prev(g)[0], 0)),
                  pl.BlockSpec((None, 6, D_MODEL), lambda g: (_mod_row(prev(g)[0], T_PEER), 0, 0))],
        out_specs=pl.BlockSpec((T_PEER, D_MODEL), lambda g: (prev(g)[0], 0)),
        out_shape=jax.ShapeDtypeStruct((N_TOK, D_MODEL), F32),
        scratch_shapes=[pltpu.VMEM((T_PEER, D_MODEL), F32),
                        pltpu.VMEM((2, E_CHUNK, T_PEER), BF16)],
        compiler_params=_params("arbitrary"),
        name="peer_dense",
    )(h2, peer_u, peer_v, brow, crow, r2, e2, x, mod_l)


def kernel(x_prompt, x_sample, cache_nat_k, cache_nat_v, cache_diff_k, cache_diff_v, c, c_ctx, w_ada, b_ada, norm1_g, norm2_g, w_in, pool_w, pool_scale, nat_q_g, nat_k_g, nat_rel_bias, conv_w, conv_b, conv_ln_g, conv_ln_b, diff_q_g, diff_k_g, diff_lambda_p, diff_subln_g, w_branch, w_gate, b_gate, w_out, peer_w_query, peer_sub_keys, peer_u, peer_v):
    x = jnp.concatenate([x_prompt.reshape(N_CTX, D_MODEL), x_sample.reshape(N_LAT, D_MODEL)], axis=0)
    cvec = jnp.concatenate([c_ctx[None, :], c, jnp.zeros((8 - 1 - DEC_BATCH, D_MODEL), F32)], axis=0)
    mod = _modulation(cvec, w_ada, b_ada).reshape(DEPTH, 8, 6, D_MODEL)

    gid = jnp.arange(CB) // NAT_HD
    gmat = (gid[:, None] == gid[None, :]).astype(BF16)
    rope = _rope_tables(T_PRE)
    ck_n = cache_nat_k.reshape(DEC_BATCH, DEPTH, PAST_LEN, BRANCH_W)
    cv_n = cache_nat_v.reshape(DEC_BATCH, DEPTH, PAST_LEN, BRANCH_W)
    ck_d = cache_diff_k.reshape(DEC_BATCH, DEPTH, PAST_LEN, BRANCH_W)
    cv_d = cache_diff_v.reshape(DEC_BATCH, DEPTH, PAST_LEN, BRANCH_W)

    states = []
    for l in range(DEPTH):
        gg = jnp.stack([jnp.tile(nat_q_g[l], NAT_HEADS), jnp.tile(nat_k_g[l], NAT_HEADS),
                        jnp.tile(diff_q_g[l], 2 * DIFF_HEADS), jnp.tile(diff_k_g[l], 2 * DIFF_HEADS)])
        h1, zp, zn = _pre(x, mod[l], norm1_g[l].reshape(1, -1), w_in[l].astype(BF16),
                          gg.reshape(len(NORM_ZB), 1, CB), gmat, rope)

        y_pool, y_conv = _seq_mixers(zp, pool_w[l], pool_scale[l], conv_w[l], conv_b[l],
                                     conv_ln_g[l], conv_ln_b[l])
        y_nat_ctx = _nat_ctx(zp, zn)
        y_nat_lat = _nat_lat(zp, zn, ck_n, cv_n, *_nat_bias_table(nat_rel_bias[l]), l)
        y_diff_ctx = _diff_ctx(zp, zn, diff_lambda_p[l], diff_subln_g[l], l)
        y_diff_lat = _diff_lat(zp, zn, ck_d, cv_d, diff_lambda_p[l], diff_subln_g[l], l)

        x, h2, qry = _merge(h1, w_gate[l].astype(BF16), b_gate[l].reshape(1, -1), y_pool, y_nat_ctx, y_nat_lat,
                            y_conv, y_diff_ctx, y_diff_lat, x, mod[l], norm2_g[l].reshape(1, -1),
                            w_branch[l].astype(BF16), w_out[l].astype(BF16), peer_w_query[l].astype(BF16))
        r2, e2, brow, crow = _peer_select(qry, peer_sub_keys[l])
        x = _peer_dense(h2, peer_u, peer_v, l, r2, e2, brow, crow, x, mod[l])

        states.append([zn[:N_CTX, Q_NK * CB:(Q_NK + 1) * CB], zp[:N_CTX, P_NV * CB:(P_NV + 1) * CB],
                       zn[:N_CTX, Q_DK * CB:(Q_DK + 1) * CB], zp[:N_CTX, P_DV * CB:(P_DV + 1) * CB]])

    def stack(idx, shape):
        return jnp.stack([states[l][idx].reshape((BATCH, SEQ) + shape) for l in range(DEPTH)], axis=1)

    return (x[:N_CTX].reshape(BATCH, SEQ, D_MODEL),
            x[N_CTX:].reshape(DEC_BATCH, DEC_SEQ, D_MODEL),
            stack(0, (NAT_HEADS, NAT_HD)),
            stack(1, (NAT_HEADS, NAT_HD)),
            stack(2, (DIFF_HEADS, 2, DIFF_HD)),
            stack(3, (DIFF_HEADS, 2 * DIFF_HD)))
```
